```python
import math
import jax
import jax.numpy as jnp
from jax import lax
import numpy as np

D_MODEL = 1024
BATCH = 4
SEQ = 8192
DEPTH = 2

MLA_HEADS = 8
QK_NOPE_DIM = 64
QK_ROPE_DIM = 32
V_HEAD_DIM = 64
Q_LORA_RANK = 384
KV_LORA_RANK = 256
ROPE_THETA = 10000.0
Q_BLOCK = 128
MLA_OUT = MLA_HEADS * V_HEAD_DIM
SSM_D_INNER = 1024
SSM_HEADDIM = 64
SSM_HEADS = SSM_D_INNER // SSM_HEADDIM
SSM_GROUPS = 2
SSM_D_STATE = 128
SSM_CONV = 4
SSM_CHUNK = 128
SSM_CONV_DIM = SSM_D_INNER + 2 * SSM_GROUPS * SSM_D_STATE
IN_SIZES = (Q_LORA_RANK, KV_LORA_RANK + QK_ROPE_DIM, SSM_D_INNER, SSM_CONV_DIM, SSM_HEADS, D_MODEL, D_MODEL)
IN_COLS = Q_LORA_RANK + KV_LORA_RANK + QK_ROPE_DIM + SSM_D_INNER + SSM_CONV_DIM + SSM_HEADS + 2 * D_MODEL
N_EXPERTS = 256
TOP_K = 8
N_EXPERT_GROUPS = 8
TOPK_GROUPS = 4
EXPERT_FF = 256
SHARED_FF = 256
ROUTED_SCALE = 2.5
EXPERT_BLOCK = 128
PLE_DIM = 256
DEEPNORM_ALPHA = (2 * DEPTH) ** 0.25
DEEPNORM_BETA = (8 * DEPTH) ** -0.25
LN_EPS = 1e-5
RMS_EPS = 1e-6

kernel_name = 'hybrid_mla_mamba2_moe_deepnorm'


def layer_norm(x, g, b):
    xf = x.astype(jnp.float32)
    mu = jnp.mean(xf, axis=-1, keepdims=True)
    xc = xf - mu
    var = jnp.mean(xc * xc, axis=-1, keepdims=True)
    return (xc * lax.rsqrt(var + LN_EPS) * g.astype(jnp.float32) + b.astype(jnp.float32)).astype(x.dtype)


def rms_norm(x, g):
    xf = x.astype(jnp.float32)
    y = xf * lax.rsqrt(jnp.mean(xf * xf, axis=-1, keepdims=True) + RMS_EPS)
    return (y * g.astype(jnp.float32)).astype(x.dtype)


def grouped_rms_norm(y, g, groups):
    shp = y.shape
    yg = y.astype(jnp.float32).reshape(shp[:-1] + (groups, shp[-1] // groups))
    yg = yg * lax.rsqrt(jnp.mean(yg * yg, axis=-1, keepdims=True) + RMS_EPS)
    return yg.reshape(shp) * g.astype(jnp.float32)


def split_columns(proj):
    pieces = []
    start = 0
    for size in IN_SIZES:
        pieces.append(proj[..., start:start + size])
        start += size
    return pieces


def rope_tables(positions):
    half = QK_ROPE_DIM // 2
    inv_freq = jnp.power(ROPE_THETA, -jnp.arange(half, dtype=jnp.float32) * (2.0 / QK_ROPE_DIM))
    ang = positions.astype(jnp.float32)[..., None] * inv_freq
    return jnp.cos(ang), jnp.sin(ang)


def apply_rope(x, cos, sin):
    half = QK_ROPE_DIM // 2
    xf = x.astype(jnp.float32)
    x1, x2 = xf[..., :half], xf[..., half:]
    return jnp.concatenate([x1 * cos - x2 * sin, x1 * sin + x2 * cos], axis=-1).astype(x.dtype)


def causal_mla_attention(q_nope, q_pe, k_nope, k_pe, v):
    bsz, seq = q_nope.shape[0], q_nope.shape[1]
    n_blocks = seq // Q_BLOCK
    scale = (QK_NOPE_DIM + QK_ROPE_DIM) ** -0.5
    key_pos = jnp.arange(seq)
    neg = jnp.finfo(jnp.float32).min

    def one_block(blk):
        start = blk * Q_BLOCK
        qn = lax.dynamic_slice_in_dim(q_nope, start, Q_BLOCK, axis=1)
        qp = lax.dynamic_slice_in_dim(q_pe, start, Q_BLOCK, axis=1)
        s = jnp.einsum('bqhd,bkhd->bhqk', qn, k_nope) + jnp.einsum('bqhr,bkr->bhqk', qp, k_pe)
        s = s.astype(jnp.float32) * scale
        q_pos = start + jnp.arange(Q_BLOCK)
        s = jnp.where(key_pos[None, :] <= q_pos[:, None], s, neg)
        prob = jax.nn.softmax(s, axis=-1).astype(v.dtype)
        return jnp.einsum('bhqk,bkhd->bqhd', prob, v)

    out = lax.map(one_block, jnp.arange(n_blocks))
    return jnp.moveaxis(out, 0, 1).reshape(bsz, seq, MLA_OUT)


def mla_branch(q_lat, kv_lat, q_norm, w_q_up, kv_norm, w_kv_up, cos, sin):
    bsz, seq = q_lat.shape[0], q_lat.shape[1]
    q = (rms_norm(q_lat, q_norm) @ w_q_up).reshape(bsz, seq, MLA_HEADS, QK_NOPE_DIM + QK_ROPE_DIM)
    q_nope = q[..., :QK_NOPE_DIM]
    q_pe = apply_rope(q[..., QK_NOPE_DIM:], cos[:, :, None, :], sin[:, :, None, :])
    c_kv = kv_lat[..., :KV_LORA_RANK]
    k_pe = apply_rope(kv_lat[..., KV_LORA_RANK:], cos, sin)
    kv = (rms_norm(c_kv, kv_norm) @ w_kv_up).reshape(bsz, seq, MLA_HEADS, QK_NOPE_DIM + V_HEAD_DIM)
    k_nope = kv[..., :QK_NOPE_DIM]
    v = kv[..., QK_NOPE_DIM:]
    return causal_mla_attention(q_nope, q_pe, k_nope, k_pe, v)


def causal_depthwise_conv(x, w, b):
    y = lax.conv_general_dilated(x, w[:, None, :].astype(x.dtype), window_strides=(1,),
                                 padding=[(SSM_CONV - 1, 0)], dimension_numbers=('NWC', 'WIO', 'NWC'),
                                 feature_group_count=x.shape[-1])
    return y + b.astype(x.dtype)


def ssd_chunked(xs, dt, a, bm, cm):
    bsz, seq, n_heads, hd = xs.shape
    g, n = bm.shape[2], bm.shape[3]
    r = n_heads // g
    nc, l = seq // SSM_CHUNK, SSM_CHUNK
    X = (xs.astype(jnp.float32) * dt[..., None]).reshape(bsz, nc, l, g, r, hd)
    a_dt = (dt * a).reshape(bsz, nc, l, g, r)
    a_cum = jnp.cumsum(a_dt, axis=2)
    Bc = bm.astype(jnp.float32).reshape(bsz, nc, l, g, n)
    Cc = cm.astype(jnp.float32).reshape(bsz, nc, l, g, n)
    seg = a_cum[:, :, :, None] - a_cum[:, :, None, :]
    tri = (jnp.arange(l)[:, None] >= jnp.arange(l)[None, :])[None, None, :, :, None, None]
    decay = jnp.exp(jnp.where(tri, seg, -jnp.inf))
    cb = jnp.einsum('bclgn,bcsgn->bclsg', Cc, Bc)
    y_diag = jnp.einsum('bclsg,bclsgr,bcsgrp->bclgrp', cb, decay, X)
    decay_states = jnp.exp(a_cum[:, :, -1:] - a_cum)
    states = jnp.einsum('bclgn,bclgr,bclgrp->bcgrpn', Bc, decay_states, X)
    chunk_decay = jnp.exp(a_cum[:, :, -1])

    def step(state, inp):
        chunk_state, dec = inp
        return state * dec[..., None, None] + chunk_state, state

    init = jnp.zeros((bsz, g, r, hd, n), jnp.float32)
    _, prev = lax.scan(step, init, (jnp.moveaxis(states, 1, 0), jnp.moveaxis(chunk_decay, 1, 0)))
    prev = jnp.moveaxis(prev, 0, 1)
    y_off = jnp.einsum('bclgn,bcgrpn,bclgr->bclgrp', Cc, prev, jnp.exp(a_cum))
    return (y_diag + y_off).reshape(bsz, seq, n_heads, hd)


def mamba2_branch(z, xbc, dt_raw, conv_w, conv_b, dt_bias, a_log, d_skip, norm_w):
    bsz, seq = z.shape[0], z.shape[1]
    xbc = jax.nn.silu(causal_depthwise_conv(xbc, conv_w, conv_b))
    gn = SSM_GROUPS * SSM_D_STATE
    xs = xbc[..., :SSM_D_INNER].reshape(bsz, seq, SSM_HEADS, SSM_HEADDIM)
    bm = xbc[..., SSM_D_INNER:SSM_D_INNER + gn].reshape(bsz, seq, SSM_GROUPS, SSM_D_STATE)
    cm = xbc[..., SSM_D_INNER + gn:].reshape(bsz, seq, SSM_GROUPS, SSM_D_STATE)
    dt = jax.nn.softplus(dt_raw.astype(jnp.float32) + dt_bias.astype(jnp.float32))
    a = -jnp.exp(a_log.astype(jnp.float32))
    y = ssd_chunked(xs, dt, a, bm, cm)
    y = y + xs.astype(jnp.float32) * d_skip.astype(jnp.float32)[:, None]
    y = y.reshape(bsz, seq, SSM_D_INNER) * jax.nn.silu(z.astype(jnp.float32))
    return grouped_rms_norm(y, norm_w, SSM_GROUPS).astype(z.dtype)


def sigmoid_group_router(h, w_router, router_bias):
    t = h.shape[0]
    scores = jax.nn.sigmoid((h @ w_router).astype(jnp.float32))
    sel = scores + router_bias.astype(jnp.float32)
    per_group = N_EXPERTS // N_EXPERT_GROUPS
    grp_score = jnp.sum(lax.top_k(sel.reshape(t, N_EXPERT_GROUPS, per_group), 2)[0], axis=-1)
    _, top_g = lax.top_k(grp_score, TOPK_GROUPS)
    gmask = jnp.sum(jax.nn.one_hot(top_g, N_EXPERT_GROUPS, dtype=jnp.float32), axis=1) > 0
    masked = jnp.where(jnp.repeat(gmask, per_group, axis=1), sel, -jnp.inf)
    _, idx = lax.top_k(masked, TOP_K)
    w = jnp.take_along_axis(scores, idx, axis=1)
    w = w / jnp.sum(w, axis=-1, keepdims=True) * ROUTED_SCALE
    return idx, w


def routed_experts(h, idx, wts, w_g, w_u, w_d):
    t, dm = h.shape
    n_assign = t * TOP_K
    e_flat = idx.reshape(n_assign)
    tok_flat = jnp.arange(n_assign, dtype=jnp.int32) // TOP_K
    w_flat = wts.reshape(n_assign)
    order = jnp.argsort(e_flat)
    e_sorted = e_flat[order]
    counts = jnp.bincount(e_flat, length=N_EXPERTS)
    padded = (counts + EXPERT_BLOCK - 1) // EXPERT_BLOCK * EXPERT_BLOCK
    pad_end = jnp.cumsum(padded)
    pad_start = pad_end - padded
    grp_start = jnp.cumsum(counts) - counts
    dest = pad_start[e_sorted] + jnp.arange(n_assign, dtype=jnp.int32) - grp_start[e_sorted]
    n_blocks = -(-n_assign // EXPERT_BLOCK) + N_EXPERTS
    cap = n_blocks * EXPERT_BLOCK
    slot_tok = jnp.full((cap,), t, jnp.int32).at[dest].set(tok_flat[order])
    slot_w = jnp.zeros((cap,), jnp.float32).at[dest].set(w_flat[order])
    block_expert = jnp.minimum(jnp.searchsorted(pad_end // EXPERT_BLOCK, jnp.arange(n_blocks), side='right'),
                               N_EXPERTS - 1)
    h_pad = jnp.concatenate([h, jnp.zeros((1, dm), h.dtype)], axis=0)

    def step(acc, inp):
        e, tok, wt = inp
        xb = h_pad[tok]
        hid = jax.nn.silu(xb @ w_g[e]) * (xb @ w_u[e])
        yb = (hid @ w_d[e]).astype(jnp.float32) * wt[:, None]
        return acc.at[tok].add(yb), None

    acc0 = jnp.zeros((t + 1, dm), jnp.float32)
    acc, _ = lax.scan(step, acc0, (block_expert, slot_tok.reshape(n_blocks, EXPERT_BLOCK),
                                   slot_w.reshape(n_blocks, EXPERT_BLOCK)))
    return acc[:t]


def swiglu(h, w_g, w_u, w_d):
    return (jax.nn.silu(h @ w_g) * (h @ w_u)) @ w_d


def setup_inputs(seed: int = 0) -> dict:
    key = jax.random.key(seed)
    ks = jax.random.split(key, 32)
    f32 = jnp.float32
    L = DEPTH

    def nrm(k, shape, scale):
        return jax.random.normal(k, shape, f32) * scale

    x = nrm(ks[0], (BATCH, SEQ, D_MODEL), 1.0)
    p = nrm(ks[1], (DEPTH, BATCH, SEQ, PLE_DIM), 1.0)
    offsets = jax.random.randint(ks[2], (BATCH, 1), 0, 4096, dtype=jnp.int32)
    positions = offsets + jnp.arange(SEQ, dtype=jnp.int32)[None, :]
    w_in = nrm(ks[3], (L, D_MODEL, IN_COLS), D_MODEL ** -0.5)
    q_norm = 1.0 + nrm(ks[4], (L, Q_LORA_RANK), 0.02)
    w_q_up = nrm(ks[5], (L, Q_LORA_RANK, MLA_HEADS * (QK_NOPE_DIM + QK_ROPE_DIM)), Q_LORA_RANK ** -0.5)
    kv_norm = 1.0 + nrm(ks[6], (L, KV_LORA_RANK), 0.02)
    w_kv_up = nrm(ks[7], (L, KV_LORA_RANK, MLA_HEADS * (QK_NOPE_DIM + V_HEAD_DIM)), KV_LORA_RANK ** -0.5)
    conv_w = nrm(ks[8], (L, SSM_CONV, SSM_CONV_DIM), SSM_CONV ** -0.5)
    conv_b = nrm(ks[9], (L, SSM_CONV_DIM), 0.02)
    dt0 = jnp.exp(jax.random.uniform(ks[10], (L, SSM_HEADS), f32, math.log(1e-3), math.log(1e-1)))
    dt_bias = dt0 + jnp.log(-jnp.expm1(-dt0))
    a_log = jnp.log(jax.random.uniform(ks[11], (L, SSM_HEADS), f32, 1.0, 16.0))
    d_skip = 1.0 + nrm(ks[12], (L, SSM_HEADS), 0.1)
    ssm_norm = 1.0 + nrm(ks[13], (L, SSM_D_INNER), 0.02)
    w_attn_br = nrm(ks[14], (L, MLA_OUT, D_MODEL), MLA_OUT ** -0.5 * DEEPNORM_BETA)
    w_ssm_br = nrm(ks[15], (L, SSM_D_INNER, D_MODEL), SSM_D_INNER ** -0.5 * DEEPNORM_BETA)
    w_o = nrm(ks[16], (L, D_MODEL, D_MODEL), D_MODEL ** -0.5 * DEEPNORM_BETA)
    ln1_g = 1.0 + nrm(ks[17], (L, D_MODEL), 0.02)
    ln1_b = nrm(ks[18], (L, D_MODEL), 0.02)
    w_router = nrm(ks[19], (L, D_MODEL, N_EXPERTS), D_MODEL ** -0.5)
    router_bias = nrm(ks[20], (L, N_EXPERTS), 0.01)
    w_exp_gate = nrm(ks[21], (L, N_EXPERTS, D_MODEL, EXPERT_FF), D_MODEL ** -0.5)
    w_exp_up = nrm(ks[22], (L, N_EXPERTS, D_MODEL, EXPERT_FF), D_MODEL ** -0.5)
    w_exp_down = nrm(ks[23], (L, N_EXPERTS, EXPERT_FF, D_MODEL), EXPERT_FF ** -0.5 * DEEPNORM_BETA)
    w_sh_gate = nrm(ks[24], (L, D_MODEL, SHARED_FF), D_MODEL ** -0.5)
    w_sh_up = nrm(ks[25], (L, D_MODEL, SHARED_FF), D_MODEL ** -0.5)
    w_sh_down = nrm(ks[26], (L, SHARED_FF, D_MODEL), SHARED_FF ** -0.5 * DEEPNORM_BETA)
    w_ple = nrm(ks[27], (L, PLE_DIM, D_MODEL), PLE_DIM ** -0.5 * DEEPNORM_BETA)
    w_ple_gate = nrm(ks[28], (L, D_MODEL, D_MODEL), D_MODEL ** -0.5)
    ln2_g = 1.0 + nrm(ks[29], (L, D_MODEL), 0.02)
    ln2_b = nrm(ks[30], (L, D_MODEL), 0.02)
    return {'x': x, 'p': p, 'positions': positions, 'w_in': w_in, 'q_norm': q_norm, 'w_q_up': w_q_up,
            'kv_norm': kv_norm, 'w_kv_up': w_kv_up, 'conv_w': conv_w, 'conv_b': conv_b, 'dt_bias': dt_bias,
            'a_log': a_log, 'd_skip': d_skip, 'ssm_norm': ssm_norm, 'w_attn_br': w_attn_br, 'w_ssm_br': w_ssm_br,
            'w_o': w_o, 'ln1_g': ln1_g, 'ln1_b': ln1_b, 'w_router': w_router, 'router_bias': router_bias,
            'w_exp_gate': w_exp_gate, 'w_exp_up': w_exp_up, 'w_exp_down': w_exp_down, 'w_sh_gate': w_sh_gate,
            'w_sh_up': w_sh_up, 'w_sh_down': w_sh_down, 'w_ple': w_ple, 'w_ple_gate': w_ple_gate,
            'ln2_g': ln2_g, 'ln2_b': ln2_b}


def reference(x, p, positions, w_in, q_norm, w_q_up, kv_norm, w_kv_up, conv_w, conv_b, dt_bias, a_log, d_skip,
              ssm_norm, w_attn_br, w_ssm_br, w_o, ln1_g, ln1_b, w_router, router_bias, w_exp_gate, w_exp_up,
              w_exp_down, w_sh_gate, w_sh_up, w_sh_down, w_ple, w_ple_gate, ln2_g, ln2_b):
    bsz, seq, dm = x.shape
    n_tok = bsz * seq
    cos, sin = rope_tables(positions)
    h = x
    for i in range(DEPTH):
        proj = h @ w_in[i]
        q_lat, kv_lat, z, xbc, dt_raw, gate_a, gate_b = split_columns(proj)
        ya = mla_branch(q_lat, kv_lat, q_norm[i], w_q_up[i], kv_norm[i], w_kv_up[i], cos, sin) @ w_attn_br[i]
        yb = mamba2_branch(z, xbc, dt_raw, conv_w[i], conv_b[i], dt_bias[i], a_log[i], d_skip[i],
                           ssm_norm[i]) @ w_ssm_br[i]
        mixed = (jax.nn.sigmoid(gate_a) * ya + jax.nn.sigmoid(gate_b) * yb) @ w_o[i]
        h = layer_norm(DEEPNORM_ALPHA * h + mixed, ln1_g[i], ln1_b[i])
        ht = h.reshape(n_tok, dm)
        idx, wts = sigmoid_group_router(ht, w_router[i], router_bias[i])
        routed = routed_experts(ht, idx, wts, w_exp_gate[i], w_exp_up[i], w_exp_down[i]).astype(h.dtype)
        ffn = (routed + swiglu(ht, w_sh_gate[i], w_sh_up[i], w_sh_down[i])).reshape(bsz, seq, dm)
        ple = (p[i] @ w_ple[i]) * jax.nn.sigmoid(h @ w_ple_gate[i])
        h = layer_norm(DEEPNORM_ALPHA * h + ffn + ple, ln2_g[i], ln2_b[i])
    return h
```

```python
import functools
import math

import jax
import jax.numpy as jnp
from jax import lax
from jax.experimental import pallas as pl
from jax.experimental.pallas import tpu as pltpu

F32 = jnp.float32
BF16 = jnp.bfloat16
I32 = jnp.int32
HIGHEST = lax.Precision.HIGHEST

D_MODEL = 1024
BATCH = 4
SEQ = 8192
DEPTH = 2
N_TOK = BATCH * SEQ
MLA_HEADS = 8
QK_NOPE = 64
QK_ROPE = 32
V_DIM = 64
Q_RANK = 384
KV_RANK = 256
ROPE_THETA = 10000.0
SSM_INNER = 1024
SSM_HEADDIM = 64
SSM_HEADS = 16
SSM_GROUPS = 2
SSM_STATE = 128
SSM_CONV = 4
SSM_CHUNK = 128
SSM_CONV_DIM = 1536
N_EXPERTS = 256
TOP_K = 8
N_GROUPS = 8
TOPK_GROUPS = 4
PER_GROUP = N_EXPERTS // N_GROUPS
EXPERT_FF = 256
ROUTED_SCALE = 2.5
PLE_DIM = 256
ALPHA = (2 * DEPTH) ** 0.25
LN_EPS = 1e-5
RMS_EPS = 1e-6

LANES = 128
SUBLANES = 8
VMEM_LIMIT = 48 * 1024 * 1024

COL_GATE_A = 0
COL_GATE_B = 1024
COL_Z = 2048
COL_XBC = 3072
COL_QLAT = 4608
COL_SMALL = 4992
COL_CKV = 5120
PACK_COLS = 5376
KPE_LANE = 64
DT_LANE = 96

HEAD_PAD = 128
N_PAIRS = MLA_HEADS // 2

SLOT_BLOCK = 128
N_ASSIGN = N_TOK * TOP_K
N_BLOCKS = N_ASSIGN // SLOT_BLOCK + N_EXPERTS
CAP = N_BLOCKS * SLOT_BLOCK
D_CHUNKS = D_MODEL // LANES

NEG = float(jnp.finfo(jnp.float32).min)


def _cparams(sem):
    return pltpu.CompilerParams(dimension_semantics=sem, vmem_limit_bytes=VMEM_LIMIT)


def _sigmoid(x):
    return 1.0 / (1.0 + jnp.exp(-x))


def _silu(x):
    return x * _sigmoid(x)


def _layer_norm(x, g, b):
    mu = jnp.mean(x, axis=-1, keepdims=True)
    xc = x - mu
    var = jnp.mean(xc * xc, axis=-1, keepdims=True)
    return xc * lax.rsqrt(var + LN_EPS) * g + b


def _rms_norm(x, g):
    return x * lax.rsqrt(jnp.mean(x * x, axis=-1, keepdims=True) + RMS_EPS) * g


def _dot(a, b):
    return jnp.dot(a, b, preferred_element_type=F32)


def _mm_kernel(x_ref, w_ref, o_ref):
    o_ref[...] = _dot(x_ref[...], w_ref[...]).astype(o_ref.dtype)


def _in_proj(hb, w_pack):
    tm, tn = 1024, 768
    return pl.pallas_call(
        _mm_kernel,
        grid=(PACK_COLS // tn, N_TOK // tm),
        in_specs=[pl.BlockSpec((tm, D_MODEL), lambda j, i: (i, 0)),
                  pl.BlockSpec((D_MODEL, tn), lambda j, i: (0, j))],
        out_specs=pl.BlockSpec((tm, tn), lambda j, i: (i, j)),
        out_shape=jax.ShapeDtypeStruct((N_TOK, PACK_COLS), F32),
        compiler_params=_cparams(("parallel", "parallel")),
        name="in_proj",
    )(hb, w_pack)


def _rope128(x, c, s1, s2):
    return x * c + pltpu.roll(x, LANES - 16, 1) * s1 + pltpu.roll(x, 16, 1) * s2


def _q_up_kernel(ql_ref, g_ref, w_ref, c_ref, s1_ref, s2_ref, q_ref):
    y = _rms_norm(ql_ref[...], g_ref[...]).astype(BF16)
    q = _dot(y, w_ref[...])
    c, s1, s2 = c_ref[...], s1_ref[...], s2_ref[...]
    scale = (QK_NOPE + QK_ROPE) ** -0.5
    for h in range(MLA_HEADS):
        qh = _rope128(q[:, h * HEAD_PAD:(h + 1) * HEAD_PAD], c, s1, s2)
        q_ref[h] = (qh * scale).astype(BF16)


def _q_up(proj, g, wq, tc, ts1, ts2):
    tm = 512
    spt = SEQ // tm
    tab = pl.BlockSpec((tm, LANES), lambda i: (i, 0))
    return pl.pallas_call(
        _q_up_kernel,
        grid=(N_TOK // tm,),
        in_specs=[pl.BlockSpec((tm, Q_RANK), lambda i: (i, COL_QLAT // Q_RANK)),
                  pl.BlockSpec((1, Q_RANK), lambda i: (0, 0)),
                  pl.BlockSpec((Q_RANK, MLA_HEADS * HEAD_PAD), lambda i: (0, 0)),
                  tab, tab, tab],
        out_specs=pl.BlockSpec((None, MLA_HEADS, tm, HEAD_PAD), lambda i: (i // spt, 0, i % spt, 0)),
        out_shape=jax.ShapeDtypeStruct((BATCH, MLA_HEADS, SEQ, HEAD_PAD), BF16),
        compiler_params=_cparams(("parallel",)),
        name="q_up",
    )(proj, g, wq, tc, ts1, ts2)


def _kv_up_kernel(ckv_ref, small_ref, g_ref, wk_ref, wv_ref, c_ref, s1_ref, s2_ref, k_ref, v_ref):
    y = _rms_norm(ckv_ref[...], g_ref[...]).astype(BF16)
    k_all = _dot(y, wk_ref[...])
    v_all = _dot(y, wv_ref[...])
    lane = lax.broadcasted_iota(I32, small_ref.shape, 1)
    kpe_raw = jnp.where((lane >= KPE_LANE) & (lane < KPE_LANE + QK_ROPE), small_ref[...], 0.0)
    kpe = _rope128(kpe_raw, c_ref[...], s1_ref[...], s2_ref[...])
    for h in range(MLA_HEADS):
        k_ref[h] = (k_all[:, h * HEAD_PAD:(h + 1) * HEAD_PAD] + kpe).astype(BF16)
    for j in range(N_PAIRS):
        v_ref[j] = v_all[:, j * LANES:(j + 1) * LANES].astype(BF16)


def _kv_up(proj, g, wk, wv, tc, ts1, ts2):
    tm = 512
    spt = SEQ // tm
    tab = pl.BlockSpec((tm, LANES), lambda i: (i, 0))
    return pl.pallas_call(
        _kv_up_kernel,
        grid=(N_TOK // tm,),
        in_specs=[pl.BlockSpec((tm, KV_RANK), lambda i: (i, COL_CKV // KV_RANK)),
                  pl.BlockSpec((tm, LANES), lambda i: (i, COL_SMALL // LANES)),
                  pl.BlockSpec((1, KV_RANK), lambda i: (0, 0)),
                  pl.BlockSpec((KV_RANK, MLA_HEADS * HEAD_PAD), lambda i: (0, 0)),
                  pl.BlockSpec((KV_RANK, MLA_HEADS * V_DIM), lambda i: (0, 0)),
                  tab, tab, tab],
        out_specs=[pl.BlockSpec((None, MLA_HEADS, tm, HEAD_PAD), lambda i: (i // spt, 0, i % spt, 0)),
                   pl.BlockSpec((None, N_PAIRS, tm, LANES), lambda i: (i // spt, 0, i % spt, 0))],
        out_shape=[jax.ShapeDtypeStruct((BATCH, MLA_HEADS, SEQ, HEAD_PAD), BF16),
                   jax.ShapeDtypeStruct((BATCH, N_PAIRS, SEQ, LANES), BF16)],
        compiler_params=_cparams(("parallel",)),
        name="kv_up",
    )(proj, proj, g, wk, wv, tc, ts1, ts2)


ATT_T = 512


def _attn_kernel(q_ref, k_ref, v_ref, o_ref, m_ref, l_ref, acc_ref):
    qi = pl.program_id(2)
    ki = pl.program_id(3)

    @pl.when(ki == 0)
    def _():
        m_ref[...] = jnp.full(m_ref.shape, -jnp.inf, F32)
        l_ref[...] = jnp.zeros(l_ref.shape, F32)
        acc_ref[...] = jnp.zeros(acc_ref.shape, F32)

    def step(masked):
        for hh in range(2):
            s = lax.dot_general(q_ref[hh], k_ref[hh], (((1,), (1,)), ((), ())),
                                preferred_element_type=F32)
            if masked:
                row = lax.broadcasted_iota(I32, s.shape, 0)
                col = lax.broadcasted_iota(I32, s.shape, 1)
                s = jnp.where(col <= row, s, NEG)
            m_prev = m_ref[hh]
            m_new = jnp.maximum(m_prev, jnp.max(s, axis=-1, keepdims=True))
            alpha = jnp.exp(m_prev - m_new)
            p = jnp.exp(s - m_new)
            l_ref[hh] = alpha * l_ref[hh] + jnp.sum(p, axis=-1, keepdims=True)
            acc_ref[hh] = alpha * acc_ref[hh] + _dot(p.astype(BF16), v_ref[...])
            m_ref[hh] = m_new

    @pl.when(ki < qi)
    def _():
        step(False)

    @pl.when(ki == qi)
    def _():
        step(True)
        lane = lax.broadcasted_iota(I32, o_ref.shape, 1)
        o0 = acc_ref[0] / l_ref[0]
        o1 = acc_ref[1] / l_ref[1]
        o_ref[...] = jnp.where(lane < V_DIM, o0, o1).astype(o_ref.dtype)


def _attention(q, k, v):
    t = ATT_T
    n = SEQ // t
    return pl.pallas_call(
        _attn_kernel,
        grid=(BATCH, N_PAIRS, n, n),
        in_specs=[pl.BlockSpec((None, 2, t, HEAD_PAD), lambda b, j, qi, ki: (b, j, qi, 0)),
                  pl.BlockSpec((None, 2, t, HEAD_PAD), lambda b, j, qi, ki: (b, j, jnp.minimum(ki, qi), 0)),
                  pl.BlockSpec((None, None, t, LANES), lambda b, j, qi, ki: (b, j, jnp.minimum(ki, qi), 0))],
        out_specs=pl.BlockSpec((None, t, LANES), lambda b, j, qi, ki: (b, qi, j)),
        out_shape=jax.ShapeDtypeStruct((BATCH, SEQ, MLA_HEADS * V_DIM), BF16),
        scratch_shapes=[pltpu.VMEM((2, t, 1), F32), pltpu.VMEM((2, t, 1), F32),
                        pltpu.VMEM((2, t, LANES), F32)],
        compiler_params=_cparams(("parallel", "parallel", "parallel", "arbitrary")),
        name="mla_attention",
    )(q, k, v)


HALO = SUBLANES
HEADS_PER_GROUP = SSM_HEADS // SSM_GROUPS
GROUP_W = SSM_INNER // SSM_GROUPS


def _ssd_kernel(xbc_ref, halo_ref, z_ref, small_ref, cw_ref, cb_ref, dtb_ref, alog_ref, dexp_ref,
                nw_ref, e_ref, y_ref, st_ref, cat_ref):
    c = pl.program_id(1)
    L = SSM_CHUNK

    @pl.when(c == 0)
    def _():
        st_ref[...] = jnp.zeros(st_ref.shape, F32)

    cat_ref[0:HALO, :] = jnp.where(c == 0, 0.0, halo_ref[...])
    cat_ref[HALO:HALO + L, :] = xbc_ref[...]
    acc = jnp.broadcast_to(cb_ref[...], (L, SSM_CONV_DIM))
    for kk in range(SSM_CONV):
        off = HALO - (SSM_CONV - 1) + kk
        acc = acc + cw_ref[kk:kk + 1, :] * cat_ref[off:off + L, :]
    xc = _silu(acc)
    xs = xc[:, :SSM_INNER]
    bm = xc[:, SSM_INNER:SSM_INNER + SSM_GROUPS * SSM_STATE]
    cm = xc[:, SSM_INNER + SSM_GROUPS * SSM_STATE:]

    lane = lax.broadcasted_iota(I32, (L, LANES), 1)
    dt_lanes = (lane >= DT_LANE) & (lane < DT_LANE + SSM_HEADS)
    dt_in = small_ref[...] + dtb_ref[...]
    dt = jnp.maximum(dt_in, 0.0) + jnp.log1p(jnp.exp(-jnp.abs(dt_in)))
    dt = jnp.where(dt_lanes, dt, 0.0)
    a = -jnp.exp(alog_ref[...])
    a_dt = dt * a
    row = lax.broadcasted_iota(I32, (L, L), 0)
    col = lax.broadcasted_iota(I32, (L, L), 1)
    tri = row >= col
    a_cum = jnp.dot(tri.astype(F32), a_dt, precision=HIGHEST, preferred_element_type=F32)
    a_last = a_cum[L - 1:L, :]
    e = e_ref[...]

    def expand(t):
        return jnp.dot(t, e, precision=HIGHEST, preferred_element_type=F32)

    x_dt = xs * expand(dt)
    eac_x = expand(jnp.exp(a_cum))
    ds_x = expand(jnp.exp(a_last - a_cum))
    cd_x = eac_x[L - 1:L, :]
    xd = (x_dt * ds_x).astype(BF16)
    x_bf = x_dt.astype(BF16)
    a_cum_t = a_cum.T
    lane_h = lax.broadcasted_iota(I32, (L, LANES), 1)

    y_parts = []
    for g in range(SSM_GROUPS):
        bg = bm[:, g * SSM_STATE:(g + 1) * SSM_STATE].astype(BF16)
        cg = cm[:, g * SSM_STATE:(g + 1) * SSM_STATE].astype(BF16)
        cb = lax.dot_general(cg, bg, (((1,), (1,)), ((), ())), preferred_element_type=F32)
        st_prev = st_ref[g]
        y_off = _dot(cg, st_prev.astype(BF16)) * eac_x[:, g * GROUP_W:(g + 1) * GROUP_W]
        for jp in range(HEADS_PER_GROUP // 2):
            pair = g * (HEADS_PER_GROUP // 2) + jp
            xp = x_bf[:, pair * LANES:(pair + 1) * LANES]
            outs = []
            for hh in range(2):
                hl = DT_LANE + 2 * pair + hh
                seg = a_cum[:, hl:hl + 1] - a_cum_t[hl:hl + 1, :]
                decay = jnp.where(tri, jnp.exp(seg), 0.0)
                outs.append(_dot((cb * decay).astype(BF16), xp))
            y_parts.append(jnp.where(lane_h < SSM_HEADDIM, outs[0], outs[1])
                           + y_off[:, jp * LANES:(jp + 1) * LANES])
        upd = lax.dot_general(bg, xd[:, g * GROUP_W:(g + 1) * GROUP_W], (((0,), (0,)), ((), ())),
                              preferred_element_type=F32)
        st_ref[g] = st_prev * cd_x[:, g * GROUP_W:(g + 1) * GROUP_W] + upd

    y = jnp.concatenate(y_parts, axis=1) + xs * dexp_ref[...]
    y = y * _silu(z_ref[...])
    normed = []
    for g in range(SSM_GROUPS):
        yg = y[:, g * GROUP_W:(g + 1) * GROUP_W]
        normed.append(yg * lax.rsqrt(jnp.mean(yg * yg, axis=-1, keepdims=True) + RMS_EPS))
    y_ref[...] = (jnp.concatenate(normed, axis=1) * nw_ref[...]).astype(y_ref.dtype)


def _ssd(proj, cw, cb, dtb, alog, dexp, nw, e_mat):
    L = SSM_CHUNK
    nc = SEQ // L
    xbc_blk = COL_XBC // SSM_CONV_DIM

    def row1(w):
        return pl.BlockSpec((1, w), lambda b, c: (0, 0))

    return pl.pallas_call(
        _ssd_kernel,
        grid=(BATCH, nc),
        in_specs=[pl.BlockSpec((L, SSM_CONV_DIM), lambda b, c: (b * nc + c, xbc_blk)),
                  pl.BlockSpec((HALO, SSM_CONV_DIM),
                               lambda b, c: (jnp.maximum((b * nc + c) * (L // HALO) - 1, 0), xbc_blk)),
                  pl.BlockSpec((L, SSM_INNER), lambda b, c: (b * nc + c, COL_Z // SSM_INNER)),
                  pl.BlockSpec((L, LANES), lambda b, c: (b * nc + c, COL_SMALL // LANES)),
                  pl.BlockSpec((SSM_CONV, SSM_CONV_DIM), lambda b, c: (0, 0)),
                  row1(SSM_CONV_DIM), row1(LANES), row1(LANES), row1(SSM_INNER), row1(SSM_INNER),
                  pl.BlockSpec((LANES, SSM_INNER), lambda b, c: (0, 0))],
        out_specs=pl.BlockSpec((L, SSM_INNER), lambda b, c: (b * nc + c, 0)),
        out_shape=jax.ShapeDtypeStruct((N_TOK, SSM_INNER), BF16),
        scratch_shapes=[pltpu.VMEM((SSM_GROUPS, SSM_STATE, GROUP_W), F32),
                        pltpu.VMEM((HALO + L, SSM_CONV_DIM), F32)],
        compiler_params=_cparams(("parallel", "arbitrary")),
        name="mamba2_ssd",
    )(proj, proj, proj, proj, cw, cb, dtb, alog, dexp, nw, e_mat)


def _merge_kernel(attn_ref, ssm_ref, ga_ref, gb_ref, h_ref, wa_ref, ws_ref, wo_ref, g_ref, b_ref,
                  hf_ref, hb_ref):
    ya = _dot(attn_ref[...], wa_ref[...])
    yb = _dot(ssm_ref[...], ws_ref[...])
    mix = _sigmoid(ga_ref[...]) * ya + _sigmoid(gb_ref[...]) * yb
    mixed = _dot(mix.astype(BF16), wo_ref[...])
    h1 = _layer_norm(ALPHA * h_ref[...] + mixed, g_ref[...], b_ref[...])
    hf_ref[...] = h1
    hb_ref[...] = h1.astype(BF16)


def _merge(attn, ssm_y, proj, hf, wa, ws, wo, g, b):
    tm = 256
    full = lambda r, c: pl.BlockSpec((r, c), lambda i: (0, 0))
    tile = lambda c, j=0: pl.BlockSpec((tm, c), lambda i: (i, j))
    return pl.pallas_call(
        _merge_kernel,
        grid=(N_TOK // tm,),
        in_specs=[tile(MLA_HEADS * V_DIM), tile(SSM_INNER), tile(D_MODEL, COL_GATE_A // D_MODEL),
                  tile(D_MODEL, COL_GATE_B // D_MODEL), tile(D_MODEL),
                  full(MLA_HEADS * V_DIM, D_MODEL), full(SSM_INNER, D_MODEL), full(D_MODEL, D_MODEL),
                  full(1, D_MODEL), full(1, D_MODEL)],
        out_specs=[tile(D_MODEL), tile(D_MODEL)],
        out_shape=[jax.ShapeDtypeStruct((N_TOK, D_MODEL), F32),
                   jax.ShapeDtypeStruct((N_TOK, D_MODEL), BF16)],
        compiler_params=_cparams(("parallel",)),
        name="merge_ln1",
    )(attn, ssm_y, proj, proj, hf, wa, ws, wo, g, b)


def _first_argmax(vals, lane):
    m = jnp.max(vals, axis=-1, keepdims=True)
    idx = jnp.min(jnp.where(vals == m, lane, float(N_EXPERTS)), axis=-1, keepdims=True)
    return m, idx


def _post_kernel(hf_ref, hb_ref, p_ref, wr_ref, rb_ref, wsg_ref, wsu_ref, wsd_ref, wp_ref, wpg_ref,
                 base_ref, idx_ref, wt_ref):
    hf = hf_ref[...]
    hb = hb_ref[...]
    tm = hf.shape[0]
    logits = jnp.dot(hf, wr_ref[...], precision=HIGHEST, preferred_element_type=F32)
    scores = _sigmoid(logits)
    sel = scores + rb_ref[...]
    lane_i = lax.broadcasted_iota(I32, (tm, N_EXPERTS), 1)
    grp = lane_i // PER_GROUP
    lane = lane_i.astype(F32)

    grp_scores = []
    for g in range(N_GROUPS):
        vals = jnp.where(grp == g, sel, -jnp.inf)
        m1, i1 = _first_argmax(vals, lane)
        m2 = jnp.max(jnp.where(lane == i1, -jnp.inf, vals), axis=-1, keepdims=True)
        grp_scores.append(m1 + m2)
    keep = jnp.zeros((tm, N_EXPERTS), jnp.bool_)
    for g in range(N_GROUPS):
        rank = jnp.zeros((tm, 1), I32)
        for o in range(N_GROUPS):
            if o == g:
                continue
            ahead = (grp_scores[o] > grp_scores[g]) if o > g else (grp_scores[o] >= grp_scores[g])
            rank = rank + ahead.astype(I32)
        keep = keep | ((grp == g) & (rank < TOPK_GROUPS))
    masked = jnp.where(keep, sel, -jnp.inf)

    lane_k = lax.broadcasted_iota(I32, (tm, TOP_K), 1)
    idx_out = jnp.zeros((tm, TOP_K), I32)
    w_out = jnp.zeros((tm, TOP_K), F32)
    for kk in range(TOP_K):
        _, ik = _first_argmax(masked, lane)
        hit = lane == ik
        wk = jnp.sum(jnp.where(hit, scores, 0.0), axis=-1, keepdims=True)
        masked = jnp.where(hit, -jnp.inf, masked)
        idx_out = jnp.where(lane_k == kk, ik.astype(I32), idx_out)
        w_out = jnp.where(lane_k == kk, wk, w_out)
    w_out = w_out / jnp.sum(w_out, axis=-1, keepdims=True) * ROUTED_SCALE
    idx_ref[...] = idx_out
    wt_ref[...] = w_out

    shared = _dot((_silu(_dot(hb, wsg_ref[...])) * _dot(hb, wsu_ref[...])).astype(BF16), wsd_ref[...])
    ple = _dot(p_ref[...].astype(BF16), wp_ref[...]) * _sigmoid(_dot(hb, wpg_ref[...]))
    base_ref[...] = ALPHA * hf + shared + ple


def _post(hf, hb, p, wr, rb, wsg, wsu, wsd, wp, wpg):
    tm = 256
    full = lambda r, c: pl.BlockSpec((r, c), lambda i: (0, 0))
    tile = lambda c: pl.BlockSpec((tm, c), lambda i: (i, 0))
    return pl.pallas_call(
        _post_kernel,
        grid=(N_TOK // tm,),
        in_specs=[tile(D_MODEL), tile(D_MODEL), tile(PLE_DIM),
                  full(D_MODEL, N_EXPERTS), full(1, N_EXPERTS),
                  full(D_MODEL, EXPERT_FF), full(D_MODEL, EXPERT_FF), full(EXPERT_FF, D_MODEL),
                  full(PLE_DIM, D_MODEL), full(D_MODEL, D_MODEL)],
        out_specs=[tile(D_MODEL), tile(TOP_K), tile(TOP_K)],
        out_shape=[jax.ShapeDtypeStruct((N_TOK, D_MODEL), F32),
                   jax.ShapeDtypeStruct((N_TOK, TOP_K), I32),
                   jax.ShapeDtypeStruct((N_TOK, TOP_K), F32)],
        compiler_params=_cparams(("parallel",)),
        name="router_shared_ple",
    )(hf, hb, p, wr, rb, wsg, wsu, wsd, wp, wpg)


def _moe_kernel(be_ref, nu_ref, tok_hbm, h_hbm, sw_ref, wg_ref, wu_ref, wd_ref, y_ref,
                tok_smem, xbuf, wg_bf, wu_bf, wd_bf, tok_sem, row_sem):
    i = pl.program_id(0)
    n_used = nu_ref[0]

    def tok_copy(blk, slot):
        return pltpu.make_async_copy(tok_hbm.at[blk], tok_smem.at[slot], tok_sem.at[slot])

    def issue_rows(slot):
        def body(r, carry):
            tok = tok_smem[slot, r]
            pltpu.make_async_copy(h_hbm.at[tok], xbuf.at[slot, r], row_sem.at[slot]).start()
            return carry
        lax.fori_loop(0, SLOT_BLOCK, body, 0, unroll=8)

    def wait_rows(slot):
        pltpu.make_async_copy(h_hbm.at[pl.ds(0, SLOT_BLOCK)], xbuf.at[slot], row_sem.at[slot]).wait()

    @pl.when(i == 0)
    def _():
        tok_copy(0, 0).start()
        tok_copy(1, 1).start()
        tok_copy(0, 0).wait()
        issue_rows(0)

    nxt = i + 1
    nslot = nxt % 2

    @pl.when(nxt < n_used)
    def _():
        tok_copy(nxt, nslot).wait()
        issue_rows(nslot)

    @pl.when(i + 2 < n_used)
    def _():
        tok_copy(i + 2, i % 2).start()

    changed = jnp.logical_or(i == 0, be_ref[i] != be_ref[jnp.maximum(i - 1, 0)])

    @pl.when(changed)
    def _():
        wg_bf[...] = wg_ref[...].astype(BF16)
        wu_bf[...] = wu_ref[...].astype(BF16)
        wd_bf[...] = wd_ref[...].astype(BF16)

    @pl.when(i < n_used)
    def _():
        slot = i % 2
        wait_rows(slot)
        rows = xbuf.at[slot]
        x = jnp.concatenate([rows[:, cc, :] for cc in range(D_CHUNKS)], axis=1).astype(BF16)
        hid = (_silu(_dot(x, wg_bf[...])) * _dot(x, wu_bf[...])).astype(BF16)
        y = _dot(hid, wd_bf[...]) * sw_ref[...]
        for cc in range(D_CHUNKS):
            y_ref[:, cc, :] = y[:, cc * LANES:(cc + 1) * LANES]

    @pl.when(i >= n_used)
    def _():
        y_ref[...] = jnp.zeros(y_ref.shape, F32)


def _moe(block_expert, n_used, slot_tok, h3, slot_w, wg, wu, wd, layer):
    grid_spec = pltpu.PrefetchScalarGridSpec(
        num_scalar_prefetch=2,
        grid=(N_BLOCKS,),
        in_specs=[pl.BlockSpec(memory_space=pl.ANY),
                  pl.BlockSpec(memory_space=pl.ANY),
                  pl.BlockSpec((SLOT_BLOCK, 1), lambda i, be, nu: (i, 0)),
                  pl.BlockSpec((None, None, D_MODEL, EXPERT_FF), lambda i, be, nu: (layer, be[i], 0, 0)),
                  pl.BlockSpec((None, None, D_MODEL, EXPERT_FF), lambda i, be, nu: (layer, be[i], 0, 0)),
                  pl.BlockSpec((None, None, EXPERT_FF, D_MODEL), lambda i, be, nu: (layer, be[i], 0, 0))],
        out_specs=pl.BlockSpec((SLOT_BLOCK, D_CHUNKS, LANES), lambda i, be, nu: (i, 0, 0)),
        scratch_shapes=[pltpu.SMEM((2, SLOT_BLOCK), I32),
                        pltpu.VMEM((2, SLOT_BLOCK, D_CHUNKS, LANES), F32),
                        pltpu.VMEM((D_MODEL, EXPERT_FF), BF16),
                        pltpu.VMEM((D_MODEL, EXPERT_FF), BF16),
                        pltpu.VMEM((EXPERT_FF, D_MODEL), BF16),
                        pltpu.SemaphoreType.DMA((2,)),
                        pltpu.SemaphoreType.DMA((2,))])
    return pl.pallas_call(
        _moe_kernel,
        grid_spec=grid_spec,
        out_shape=jax.ShapeDtypeStruct((CAP, D_CHUNKS, LANES), F32),
        compiler_params=_cparams(("arbitrary",)),
        name="routed_experts",
    )(block_expert, n_used, slot_tok, h3, slot_w, wg, wu, wd)


COMB_T = 128


def _combine_kernel(pos_hbm, y_hbm, base_ref, g_ref, b_ref, hf_ref, hb_ref,
                    pos_smem, buf, scr, pos_sem, row_sem):
    i = pl.program_id(0)
    n = pl.num_programs(0)

    def pos_copy(blk, slot):
        return pltpu.make_async_copy(pos_hbm.at[blk], pos_smem.at[slot], pos_sem.at[slot])

    def issue_rows(slot):
        def body(t, carry):
            for kk in range(TOP_K):
                src = pos_smem[slot, t * TOP_K + kk]
                pltpu.make_async_copy(y_hbm.at[src], buf.at[slot, t, kk], row_sem.at[slot]).start()
            return carry
        lax.fori_loop(0, COMB_T, body, 0)

    def wait_rows(slot):
        for kk in range(TOP_K):
            pltpu.make_async_copy(y_hbm.at[pl.ds(0, COMB_T)], buf.at[slot, :, kk],
                                  row_sem.at[slot]).wait()

    @pl.when(i == 0)
    def _():
        pos_copy(0, 0).start()
        pos_copy(1, 1).start()
        pos_copy(0, 0).wait()
        issue_rows(0)

    nxt = i + 1
    nslot = nxt % 2

    @pl.when(nxt < n)
    def _():
        pos_copy(nxt, nslot).wait()
        issue_rows(nslot)

    @pl.when(i + 2 < n)
    def _():
        pos_copy(i + 2, i % 2).start()

    slot = i % 2
    wait_rows(slot)
    acc = buf[slot, :, 0]
    for kk in range(1, TOP_K):
        acc = acc + buf[slot, :, kk]
    scr[...] = acc
    routed = jnp.concatenate([scr[:, cc, :] for cc in range(D_CHUNKS)], axis=1)
    h2 = _layer_norm(base_ref[...] + routed, g_ref[...], b_ref[...])
    hf_ref[...] = h2
    hb_ref[...] = h2.astype(BF16)


def _combine(pos, y3, base, g, b):
    tm = COMB_T
    full = lambda r, c: pl.BlockSpec((r, c), lambda i: (0, 0))
    tile = lambda c: pl.BlockSpec((tm, c), lambda i: (i, 0))
    return pl.pallas_call(
        _combine_kernel,
        grid=(N_TOK // tm,),
        in_specs=[pl.BlockSpec(memory_space=pl.ANY), pl.BlockSpec(memory_space=pl.ANY),
                  tile(D_MODEL), full(1, D_MODEL), full(1, D_MODEL)],
        out_specs=[tile(D_MODEL), tile(D_MODEL)],
        out_shape=[jax.ShapeDtypeStruct((N_TOK, D_MODEL), F32),
                   jax.ShapeDtypeStruct((N_TOK, D_MODEL), BF16)],
        scratch_shapes=[pltpu.SMEM((2, tm * TOP_K), I32),
                        pltpu.VMEM((2, tm, TOP_K, D_CHUNKS, LANES), F32),
                        pltpu.VMEM((tm, D_CHUNKS, LANES), F32),
                        pltpu.SemaphoreType.DMA((2,)),
                        pltpu.SemaphoreType.DMA((2,))],
        compiler_params=_cparams(("arbitrary",)),
        name="combine_ln2",
    )(pos, y3, base, g, b)


def _dispatch_tables(idx, wts):
    e_flat = idx.reshape(N_ASSIGN)
    order = jnp.argsort(e_flat).astype(I32)
    e_sorted = e_flat[order]
    counts = jnp.bincount(e_flat, length=N_EXPERTS).astype(I32)
    padded = (counts + SLOT_BLOCK - 1) // SLOT_BLOCK * SLOT_BLOCK
    pad_end = jnp.cumsum(padded)
    pad_start = pad_end - padded
    grp_start = jnp.cumsum(counts) - counts
    dest = pad_start[e_sorted] + jnp.arange(N_ASSIGN, dtype=I32) - grp_start[e_sorted]
    slot_tok = jnp.zeros((CAP,), I32).at[dest].set(order // TOP_K)
    slot_w = jnp.zeros((CAP,), F32).at[dest].set(wts.reshape(N_ASSIGN)[order])
    pos = jnp.zeros((N_ASSIGN,), I32).at[order].set(dest)
    block_expert = jnp.minimum(
        jnp.searchsorted(pad_end // SLOT_BLOCK, jnp.arange(N_BLOCKS, dtype=I32), side='right'),
        N_EXPERTS - 1).astype(I32)
    n_used = (pad_end[-1] // SLOT_BLOCK).astype(I32).reshape(1)
    return (slot_tok.reshape(N_BLOCKS, SLOT_BLOCK), slot_w.reshape(CAP, 1),
            pos.reshape(N_TOK // COMB_T, COMB_T * TOP_K), block_expert, n_used)


def _pack_w_in(w):
    q_lat = w[:, 0:384]
    c_kv = w[:, 384:640]
    k_pe = w[:, 640:672]
    z = w[:, 672:1696]
    xbc = w[:, 1696:3232]
    dt = w[:, 3232:3248]
    gate_a = w[:, 3248:4272]
    gate_b = w[:, 4272:5296]
    zeros = lambda n: jnp.zeros((D_MODEL, n), w.dtype)
    small = jnp.concatenate([zeros(KPE_LANE), k_pe, dt, zeros(LANES - DT_LANE - SSM_HEADS)], axis=1)
    return jnp.concatenate([gate_a, gate_b, z, xbc, q_lat, small, c_kv], axis=1).astype(BF16)


def _pack_w_q(w):
    w = w.reshape(Q_RANK, MLA_HEADS, QK_NOPE + QK_ROPE)
    w = jnp.pad(w, ((0, 0), (0, 0), (0, HEAD_PAD - QK_NOPE - QK_ROPE)))
    return w.reshape(Q_RANK, MLA_HEADS * HEAD_PAD).astype(BF16)


def _pack_w_kv(w):
    w = w.reshape(KV_RANK, MLA_HEADS, QK_NOPE + V_DIM)
    wk = jnp.pad(w[:, :, :QK_NOPE], ((0, 0), (0, 0), (0, HEAD_PAD - QK_NOPE)))
    wv = w[:, :, QK_NOPE:]
    return (wk.reshape(KV_RANK, MLA_HEADS * HEAD_PAD).astype(BF16),
            wv.reshape(KV_RANK, MLA_HEADS * V_DIM).astype(BF16))


def _rope_tables(positions):
    half = QK_ROPE // 2
    inv_freq = jnp.power(ROPE_THETA, -jnp.arange(half, dtype=F32) * (2.0 / QK_ROPE))
    ang = positions.astype(F32).reshape(N_TOK, 1) * inv_freq
    cos, sin = jnp.cos(ang), jnp.sin(ang)
    z = lambda n: jnp.zeros((N_TOK, n), F32)
    tail = HEAD_PAD - QK_NOPE - QK_ROPE
    tc = jnp.concatenate([jnp.ones((N_TOK, QK_NOPE), F32), cos, cos, z(tail)], axis=1)
    ts1 = jnp.concatenate([z(QK_NOPE), -sin, z(half), z(tail)], axis=1)
    ts2 = jnp.concatenate([z(QK_NOPE), z(half), sin, z(tail)], axis=1)
    return tc, ts1, ts2


def _head_lane_row(v):
    return jnp.zeros((1, LANES), F32).at[0, DT_LANE:DT_LANE + SSM_HEADS].set(v.astype(F32))


def _expand_matrix():
    r = jnp.arange(LANES)[:, None]
    c = jnp.arange(SSM_INNER)[None, :]
    return ((r - DT_LANE) == (c // SSM_HEADDIM)).astype(F32)


def kernel(x, p, positions, w_in, q_norm, w_q_up, kv_norm, w_kv_up, conv_w, conv_b, dt_bias, a_log, d_skip,
           ssm_norm, w_attn_br, w_ssm_br, w_o, ln1_g, ln1_b, w_router, router_bias, w_exp_gate, w_exp_up,
           w_exp_down, w_sh_gate, w_sh_up, w_sh_down, w_ple, w_ple_gate, ln2_g, ln2_b):
    tc, ts1, ts2 = _rope_tables(positions)
    e_mat = _expand_matrix()
    hf = x.reshape(N_TOK, D_MODEL)
    hb = hf.astype(BF16)
    row = lambda v: v.reshape(1, -1).astype(F32)
    for i in range(DEPTH):
        proj = _in_proj(hb, _pack_w_in(w_in[i]))
        q = _q_up(proj, row(q_norm[i]), _pack_w_q(w_q_up[i]), tc, ts1, ts2)
        wk, wv = _pack_w_kv(w_kv_up[i])
        k, v = _kv_up(proj, row(kv_norm[i]), wk, wv, tc, ts1, ts2)
        attn = _attention(q, k, v).reshape(N_TOK, MLA_HEADS * V_DIM)
        ssm_y = _ssd(proj, conv_w[i], row(conv_b[i]), _head_lane_row(dt_bias[i]), _head_lane_row(a_log[i]),
                     row(jnp.repeat(d_skip[i], SSM_HEADDIM)), row(ssm_norm[i]), e_mat)
        hf, hb = _merge(attn, ssm_y, proj, hf, w_attn_br[i].astype(BF16), w_ssm_br[i].astype(BF16),
                        w_o[i].astype(BF16), row(ln1_g[i]), row(ln1_b[i]))
        base, idx, wts = _post(hf, hb, p[i].reshape(N_TOK, PLE_DIM), w_router[i], row(router_bias[i]),
                               w_sh_gate[i].astype(BF16), w_sh_up[i].astype(BF16), w_sh_down[i].astype(BF16),
                               w_ple[i].astype(BF16), w_ple_gate[i].astype(BF16))
        slot_tok, slot_w, pos, block_expert, n_used = _dispatch_tables(idx, wts)
        y3 = _moe(block_expert, n_used, slot_tok, hf.reshape(N_TOK, D_CHUNKS, LANES), slot_w,
                  w_exp_gate, w_exp_up, w_exp_down, i)
        hf, hb = _combine(pos, y3, base, row(ln2_g[i]), row(ln2_b[i]))
    return hf.reshape(BATCH, SEQ, D_MODEL)
```

```python
import functools
import math

import jax
import jax.numpy as jnp
from jax import lax
from jax.experimental import pallas as pl
from jax.experimental.pallas import tpu as pltpu

F32 = jnp.float32
BF16 = jnp.bfloat16
I32 = jnp.int32
HIGHEST = lax.Precision.HIGHEST

D_MODEL = 1024
BATCH = 4
SEQ = 8192
DEPTH = 2
N_TOK = BATCH * SEQ
MLA_HEADS = 8
QK_NOPE = 64
QK_ROPE = 32
V_DIM = 64
Q_RANK = 384
KV_RANK = 256
ROPE_THETA = 10000.0
SSM_INNER = 1024
SSM_HEADDIM = 64
SSM_HEADS = 16
SSM_GROUPS = 2
SSM_STATE = 128
SSM_CONV = 4
SSM_CHUNK = 128
SSM_CONV_DIM = 1536
N_EXPERTS = 256
TOP_K = 8
N_GROUPS = 8
TOPK_GROUPS = 4
PER_GROUP = N_EXPERTS // N_GROUPS
EXPERT_FF = 256
ROUTED_SCALE = 2.5
PLE_DIM = 256
ALPHA = (2 * DEPTH) ** 0.25
LN_EPS = 1e-5
RMS_EPS = 1e-6

LANES = 128
SUBLANES = 8
VMEM_LIMIT = 48 * 1024 * 1024

COL_GATE_A = 0
COL_GATE_B = 1024
COL_Z = 2048
COL_XBC = 3072
COL_QLAT = 4608
COL_SMALL = 4992
COL_CKV = 5120
PACK_COLS = 5376
KPE_LANE = 64
DT_LANE = 96

HEAD_PAD = 128
N_PAIRS = MLA_HEADS // 2

ROUTE_T = 256
N_RTILES = N_TOK // ROUTE_T
SLOT_BLOCK = 128
N_ASSIGN = N_TOK * TOP_K
N_BLOCKS = N_ASSIGN // SLOT_BLOCK + N_EXPERTS
CAP = N_BLOCKS * SLOT_BLOCK
D_CHUNKS = D_MODEL // LANES

NEG = float(jnp.finfo(jnp.float32).min)


def _cparams(sem):
    return pltpu.CompilerParams(dimension_semantics=sem, vmem_limit_bytes=VMEM_LIMIT)


def _sigmoid(x):
    return 1.0 / (1.0 + jnp.exp(-x))


def _silu(x):
    return x * _sigmoid(x)


def _layer_norm(x, g, b):
    mu = jnp.mean(x, axis=-1, keepdims=True)
    xc = x - mu
    var = jnp.mean(xc * xc, axis=-1, keepdims=True)
    return xc * lax.rsqrt(var + LN_EPS) * g + b


def _rms_norm(x, g):
    return x * lax.rsqrt(jnp.mean(x * x, axis=-1, keepdims=True) + RMS_EPS) * g


def _dot(a, b):
    return jnp.dot(a, b, preferred_element_type=F32)


def _mm_kernel(x_ref, w_ref, o_ref):
    o_ref[...] = _dot(x_ref[...], w_ref[...]).astype(o_ref.dtype)


def _in_proj(hb, w_pack):
    tm, tn = 1024, 768
    return pl.pallas_call(
        _mm_kernel,
        grid=(PACK_COLS // tn, N_TOK // tm),
        in_specs=[pl.BlockSpec((tm, D_MODEL), lambda j, i: (i, 0)),
                  pl.BlockSpec((D_MODEL, tn), lambda j, i: (0, j))],
        out_specs=pl.BlockSpec((tm, tn), lambda j, i: (i, j)),
        out_shape=jax.ShapeDtypeStruct((N_TOK, PACK_COLS), F32),
        compiler_params=_cparams(("parallel", "parallel")),
        name="in_proj",
    )(hb, w_pack)


def _rope128(x, c, s1, s2):
    return x * c + pltpu.roll(x, LANES - 16, 1) * s1 + pltpu.roll(x, 16, 1) * s2


def _q_up_kernel(ql_ref, g_ref, w_ref, c_ref, s1_ref, s2_ref, q_ref):
    y = _rms_norm(ql_ref[...], g_ref[...]).astype(BF16)
    q = _dot(y, w_ref[...])
    c, s1, s2 = c_ref[...], s1_ref[...], s2_ref[...]
    scale = (QK_NOPE + QK_ROPE) ** -0.5 * math.log2(math.e)
    for h in range(MLA_HEADS):
        qh = _rope128(q[:, h * HEAD_PAD:(h + 1) * HEAD_PAD], c, s1, s2)
        q_ref[h] = (qh * scale).astype(BF16)


def _q_up(proj, g, wq, tc, ts1, ts2):
    tm = 512
    spt = SEQ // tm
    tab = pl.BlockSpec((tm, LANES), lambda i: (i, 0))
    return pl.pallas_call(
        _q_up_kernel,
        grid=(N_TOK // tm,),
        in_specs=[pl.BlockSpec((tm, Q_RANK), lambda i: (i, COL_QLAT // Q_RANK)),
                  pl.BlockSpec((1, Q_RANK), lambda i: (0, 0)),
                  pl.BlockSpec((Q_RANK, MLA_HEADS * HEAD_PAD), lambda i: (0, 0)),
                  tab, tab, tab],
        out_specs=pl.BlockSpec((None, MLA_HEADS, tm, HEAD_PAD), lambda i: (i // spt, 0, i % spt, 0)),
        out_shape=jax.ShapeDtypeStruct((BATCH, MLA_HEADS, SEQ, HEAD_PAD), BF16),
        compiler_params=_cparams(("parallel",)),
        name="q_up",
    )(proj, g, wq, tc, ts1, ts2)


def _kv_up_kernel(ckv_ref, small_ref, g_ref, wk_ref, wv_ref, c_ref, s1_ref, s2_ref, k_ref, v_ref):
    y = _rms_norm(ckv_ref[...], g_ref[...]).astype(BF16)
    k_all = _dot(y, wk_ref[...])
    v_all = _dot(y, wv_ref[...])
    lane = lax.broadcasted_iota(I32, small_ref.shape, 1)
    kpe_raw = jnp.where((lane >= KPE_LANE) & (lane < KPE_LANE + QK_ROPE), small_ref[...], 0.0)
    kpe = _rope128(kpe_raw, c_ref[...], s1_ref[...], s2_ref[...])
    for h in range(MLA_HEADS):
        k_ref[h] = (k_all[:, h * HEAD_PAD:(h + 1) * HEAD_PAD] + kpe).astype(BF16)
    for j in range(N_PAIRS):
        v_ref[j] = v_all[:, j * LANES:(j + 1) * LANES].astype(BF16)


def _kv_up(proj, g, wk, wv, tc, ts1, ts2):
    tm = 512
    spt = SEQ // tm
    tab = pl.BlockSpec((tm, LANES), lambda i: (i, 0))
    return pl.pallas_call(
        _kv_up_kernel,
        grid=(N_TOK // tm,),
        in_specs=[pl.BlockSpec((tm, KV_RANK), lambda i: (i, COL_CKV // KV_RANK)),
                  pl.BlockSpec((tm, LANES), lambda i: (i, COL_SMALL // LANES)),
                  pl.BlockSpec((1, KV_RANK), lambda i: (0, 0)),
                  pl.BlockSpec((KV_RANK, MLA_HEADS * HEAD_PAD), lambda i: (0, 0)),
                  pl.BlockSpec((KV_RANK, MLA_HEADS * V_DIM), lambda i: (0, 0)),
                  tab, tab, tab],
        out_specs=[pl.BlockSpec((None, MLA_HEADS, tm, HEAD_PAD), lambda i: (i // spt, 0, i % spt, 0)),
                   pl.BlockSpec((None, N_PAIRS, tm, LANES), lambda i: (i // spt, 0, i % spt, 0))],
        out_shape=[jax.ShapeDtypeStruct((BATCH, MLA_HEADS, SEQ, HEAD_PAD), BF16),
                   jax.ShapeDtypeStruct((BATCH, N_PAIRS, SEQ, LANES), BF16)],
        compiler_params=_cparams(("parallel",)),
        name="kv_up",
    )(proj, proj, g, wk, wv, tc, ts1, ts2)


ATT_T = 512


def _attn_kernel(q_ref, k_ref, v_ref, o_ref, m_ref, l_ref, acc_ref):
    qi = pl.program_id(2)
    ki = pl.program_id(3)

    @pl.when(ki == 0)
    def _():
        m_ref[...] = jnp.full(m_ref.shape, -jnp.inf, F32)
        l_ref[...] = jnp.zeros(l_ref.shape, F32)
        acc_ref[...] = jnp.zeros(acc_ref.shape, F32)

    def step(masked):
        for hh in range(2):
            s = lax.dot_general(q_ref[hh], k_ref[hh], (((1,), (1,)), ((), ())),
                                preferred_element_type=F32)
            if masked:
                row = lax.broadcasted_iota(I32, s.shape, 0)
                col = lax.broadcasted_iota(I32, s.shape, 1)
                s = jnp.where(col <= row, s, NEG)
            m_prev = m_ref[hh]
            m_new = jnp.maximum(m_prev, jnp.max(s, axis=-1, keepdims=True))
            alpha = jnp.exp2(m_prev - m_new)
            p = jnp.exp2(s - m_new)
            l_ref[hh] = alpha * l_ref[hh] + jnp.sum(p, axis=-1, keepdims=True)
            acc_ref[hh] = alpha * acc_ref[hh] + _dot(p.astype(BF16), v_ref[...])
            m_ref[hh] = m_new

    @pl.when(ki < qi)
    def _():
        step(False)

    @pl.when(ki == qi)
    def _():
        step(True)
        lane = lax.broadcasted_iota(I32, o_ref.shape, 1)
        o0 = acc_ref[0] / l_ref[0]
        o1 = acc_ref[1] / l_ref[1]
        o_ref[...] = jnp.where(lane < V_DIM, o0, o1).astype(o_ref.dtype)


def _attention(q, k, v):
    t = ATT_T
    n = SEQ // t
    return pl.pallas_call(
        _attn_kernel,
        grid=(BATCH, N_PAIRS, n, n),
        in_specs=[pl.BlockSpec((None, 2, t, HEAD_PAD), lambda b, j, qi, ki: (b, j, qi, 0)),
                  pl.BlockSpec((None, 2, t, HEAD_PAD), lambda b, j, qi, ki: (b, j, jnp.minimum(ki, qi), 0)),
                  pl.BlockSpec((None, None, t, LANES), lambda b, j, qi, ki: (b, j, jnp.minimum(ki, qi), 0))],
        out_specs=pl.BlockSpec((None, t, LANES), lambda b, j, qi, ki: (b, qi, j)),
        out_shape=jax.ShapeDtypeStruct((BATCH, SEQ, MLA_HEADS * V_DIM), BF16),
        scratch_shapes=[pltpu.VMEM((2, t, 1), F32), pltpu.VMEM((2, t, 1), F32),
                        pltpu.VMEM((2, t, LANES), F32)],
        compiler_params=_cparams(("parallel", "parallel", "parallel", "arbitrary")),
        name="mla_attention",
    )(q, k, v)


HALO = SUBLANES
HEADS_PER_GROUP = SSM_HEADS // SSM_GROUPS
GROUP_W = SSM_INNER // SSM_GROUPS


def _ssd_kernel(xbc_ref, halo_ref, z_ref, small_ref, cw_ref, cb_ref, dtb_ref, alog_ref, dexp_ref,
                nw_ref, e_ref, y_ref, st_ref, cat_ref):
    c = pl.program_id(1)
    L = SSM_CHUNK

    @pl.when(c == 0)
    def _():
        st_ref[...] = jnp.zeros(st_ref.shape, F32)

    cat_ref[0:HALO, :] = jnp.where(c == 0, 0.0, halo_ref[...])
    cat_ref[HALO:HALO + L, :] = xbc_ref[...]
    acc = jnp.broadcast_to(cb_ref[...], (L, SSM_CONV_DIM))
    for kk in range(SSM_CONV):
        off = HALO - (SSM_CONV - 1) + kk
        acc = acc + cw_ref[kk:kk + 1, :] * cat_ref[off:off + L, :]
    xc = _silu(acc)
    xs = xc[:, :SSM_INNER]
    bm = xc[:, SSM_INNER:SSM_INNER + SSM_GROUPS * SSM_STATE]
    cm = xc[:, SSM_INNER + SSM_GROUPS * SSM_STATE:]

    lane = lax.broadcasted_iota(I32, (L, LANES), 1)
    dt_lanes = (lane >= DT_LANE) & (lane < DT_LANE + SSM_HEADS)
    dt_in = small_ref[...] + dtb_ref[...]
    dt = jnp.maximum(dt_in, 0.0) + jnp.log1p(jnp.exp(-jnp.abs(dt_in)))
    dt = jnp.where(dt_lanes, dt, 0.0)
    a = -jnp.exp(alog_ref[...])
    a_dt = dt * a
    row = lax.broadcasted_iota(I32, (L, L), 0)
    col = lax.broadcasted_iota(I32, (L, L), 1)
    tri = row >= col
    a_cum = jnp.dot(tri.astype(F32), a_dt, precision=HIGHEST, preferred_element_type=F32)
    a_last = a_cum[L - 1:L, :]
    e = e_ref[...]

    def expand(t):
        return jnp.dot(t, e, precision=HIGHEST, preferred_element_type=F32)

    x_dt = xs * expand(dt)
    eac_x = expand(jnp.exp(a_cum))
    ds_x = expand(jnp.exp(a_last - a_cum))
    cd_x = eac_x[L - 1:L, :]
    xd = (x_dt * ds_x).astype(BF16)
    x_bf = x_dt.astype(BF16)
    a_cum_t = a_cum.T
    lane_h = lax.broadcasted_iota(I32, (L, LANES), 1)

    y_parts = []
    for g in range(SSM_GROUPS):
        bg = bm[:, g * SSM_STATE:(g + 1) * SSM_STATE].astype(BF16)
        cg = cm[:, g * SSM_STATE:(g + 1) * SSM_STATE].astype(BF16)
        cb = lax.dot_general(cg, bg, (((1,), (1,)), ((), ())), preferred_element_type=F32)
        st_prev = st_ref[g]
        y_off = _dot(cg, st_prev.astype(BF16)) * eac_x[:, g * GROUP_W:(g + 1) * GROUP_W]
        for jp in range(HEADS_PER_GROUP // 2):
            pair = g * (HEADS_PER_GROUP // 2) + jp
            xp = x_bf[:, pair * LANES:(pair + 1) * LANES]
            outs = []
            for hh in range(2):
                hl = DT_LANE + 2 * pair + hh
                seg = a_cum[:, hl:hl + 1] - a_cum_t[hl:hl + 1, :]
                decay = jnp.where(tri, jnp.exp(seg), 0.0)
                outs.append(_dot((cb * decay).astype(BF16), xp))
            y_parts.append(jnp.where(lane_h < SSM_HEADDIM, outs[0], outs[1])
                           + y_off[:, jp * LANES:(jp + 1) * LANES])
        upd = lax.dot_general(bg, xd[:, g * GROUP_W:(g + 1) * GROUP_W], (((0,), (0,)), ((), ())),
                              preferred_element_type=F32)
        st_ref[g] = st_prev * cd_x[:, g * GROUP_W:(g + 1) * GROUP_W] + upd

    y = jnp.concatenate(y_parts, axis=1) + xs * dexp_ref[...]
    y = y * _silu(z_ref[...])
    normed = []
    for g in range(SSM_GROUPS):
        yg = y[:, g * GROUP_W:(g + 1) * GROUP_W]
        normed.append(yg * lax.rsqrt(jnp.mean(yg * yg, axis=-1, keepdims=True) + RMS_EPS))
    y_ref[...] = (jnp.concatenate(normed, axis=1) * nw_ref[...]).astype(y_ref.dtype)


def _ssd(proj, cw, cb, dtb, alog, dexp, nw, e_mat):
    L = SSM_CHUNK
    nc = SEQ // L
    xbc_blk = COL_XBC // SSM_CONV_DIM

    def row1(w):
        return pl.BlockSpec((1, w), lambda b, c: (0, 0))

    return pl.pallas_call(
        _ssd_kernel,
        grid=(BATCH, nc),
        in_specs=[pl.BlockSpec((L, SSM_CONV_DIM), lambda b, c: (b * nc + c, xbc_blk)),
                  pl.BlockSpec((HALO, SSM_CONV_DIM),
                               lambda b, c: (jnp.maximum((b * nc + c) * (L // HALO) - 1, 0), xbc_blk)),
                  pl.BlockSpec((L, SSM_INNER), lambda b, c: (b * nc + c, COL_Z // SSM_INNER)),
                  pl.BlockSpec((L, LANES), lambda b, c: (b * nc + c, COL_SMALL // LANES)),
                  pl.BlockSpec((SSM_CONV, SSM_CONV_DIM), lambda b, c: (0, 0)),
                  row1(SSM_CONV_DIM), row1(LANES), row1(LANES), row1(SSM_INNER), row1(SSM_INNER),
                  pl.BlockSpec((LANES, SSM_INNER), lambda b, c: (0, 0))],
        out_specs=pl.BlockSpec((L, SSM_INNER), lambda b, c: (b * nc + c, 0)),
        out_shape=jax.ShapeDtypeStruct((N_TOK, SSM_INNER), BF16),
        scratch_shapes=[pltpu.VMEM((SSM_GROUPS, SSM_STATE, GROUP_W), F32),
                        pltpu.VMEM((HALO + L, SSM_CONV_DIM), F32)],
        compiler_params=_cparams(("parallel", "arbitrary")),
        name="mamba2_ssd",
    )(proj, proj, proj, proj, cw, cb, dtb, alog, dexp, nw, e_mat)


def _merge_kernel(attn_ref, ssm_ref, ga_ref, gb_ref, h_ref, wa_ref, ws_ref, wo_ref, g_ref, b_ref,
                  hf_ref, hb_ref):
    ya = _dot(attn_ref[...], wa_ref[...])
    yb = _dot(ssm_ref[...], ws_ref[...])
    mix = _sigmoid(ga_ref[...]) * ya + _sigmoid(gb_ref[...]) * yb
    mixed = _dot(mix.astype(BF16), wo_ref[...])
    h1 = _layer_norm(ALPHA * h_ref[...] + mixed, g_ref[...], b_ref[...])
    hf_ref[...] = h1
    hb_ref[...] = h1.astype(BF16)


def _merge(attn, ssm_y, proj, hf, wa, ws, wo, g, b):
    tm = 256
    full = lambda r, c: pl.BlockSpec((r, c), lambda i: (0, 0))
    tile = lambda c, j=0: pl.BlockSpec((tm, c), lambda i: (i, j))
    return pl.pallas_call(
        _merge_kernel,
        grid=(N_TOK // tm,),
        in_specs=[tile(MLA_HEADS * V_DIM), tile(SSM_INNER), tile(D_MODEL, COL_GATE_A // D_MODEL),
                  tile(D_MODEL, COL_GATE_B // D_MODEL), tile(D_MODEL),
                  full(MLA_HEADS * V_DIM, D_MODEL), full(SSM_INNER, D_MODEL), full(D_MODEL, D_MODEL),
                  full(1, D_MODEL), full(1, D_MODEL)],
        out_specs=[tile(D_MODEL), tile(D_MODEL)],
        out_shape=[jax.ShapeDtypeStruct((N_TOK, D_MODEL), F32),
                   jax.ShapeDtypeStruct((N_TOK, D_MODEL), BF16)],
        compiler_params=_cparams(("parallel",)),
        name="merge_ln1",
    )(attn, ssm_y, proj, proj, hf, wa, ws, wo, g, b)


def _first_argmax(vals, lane):
    m = jnp.max(vals, axis=-1, keepdims=True)
    idx = jnp.min(jnp.where(vals == m, lane, float(N_EXPERTS)), axis=-1, keepdims=True)
    return m, idx


def _post_kernel(hf_ref, hb_ref, p_ref, wr_ref, rb_ref, wsg_ref, wsu_ref, wsd_ref, wp_ref, wpg_ref,
                 base_ref, idx_ref, wt_ref, cnt_ref):
    hf = hf_ref[...]
    hb = hb_ref[...]
    tm = hf.shape[0]
    logits = jnp.dot(hf, wr_ref[...], precision=HIGHEST, preferred_element_type=F32)
    scores = _sigmoid(logits)
    sel = scores + rb_ref[...]
    lane_i = lax.broadcasted_iota(I32, (tm, N_EXPERTS), 1)
    grp = lane_i // PER_GROUP
    lane = lane_i.astype(F32)

    grp_scores = []
    for g in range(N_GROUPS):
        vals = jnp.where(grp == g, sel, -jnp.inf)
        m1, i1 = _first_argmax(vals, lane)
        m2 = jnp.max(jnp.where(lane == i1, -jnp.inf, vals), axis=-1, keepdims=True)
        grp_scores.append(m1 + m2)
    keep = jnp.zeros((tm, N_EXPERTS), jnp.bool_)
    for g in range(N_GROUPS):
        rank = jnp.zeros((tm, 1), I32)
        for o in range(N_GROUPS):
            if o == g:
                continue
            ahead = (grp_scores[o] > grp_scores[g]) if o > g else (grp_scores[o] >= grp_scores[g])
            rank = rank + ahead.astype(I32)
        keep = keep | ((grp == g) & (rank < TOPK_GROUPS))
    masked = jnp.where(keep, sel, -jnp.inf)

    lane_k = lax.broadcasted_iota(I32, (tm, TOP_K), 1)
    idx_out = jnp.zeros((tm, TOP_K), I32)
    w_out = jnp.zeros((tm, TOP_K), F32)
    chosen = jnp.zeros((tm, N_EXPERTS), F32)
    for kk in range(TOP_K):
        _, ik = _first_argmax(masked, lane)
        hit = lane == ik
        wk = jnp.sum(jnp.where(hit, scores, 0.0), axis=-1, keepdims=True)
        masked = jnp.where(hit, -jnp.inf, masked)
        chosen = jnp.where(hit, 1.0, chosen)
        idx_out = jnp.where(lane_k == kk, ik.astype(I32), idx_out)
        w_out = jnp.where(lane_k == kk, wk, w_out)
    w_out = w_out / jnp.sum(w_out, axis=-1, keepdims=True) * ROUTED_SCALE
    idx_ref[...] = idx_out
    wt_ref[...] = w_out
    cnt_ref[...] = jnp.sum(chosen, axis=0, keepdims=True)

    shared = _dot((_silu(_dot(hb, wsg_ref[...])) * _dot(hb, wsu_ref[...])).astype(BF16), wsd_ref[...])
    ple = _dot(p_ref[...].astype(BF16), wp_ref[...]) * _sigmoid(_dot(hb, wpg_ref[...]))
    base_ref[...] = ALPHA * hf + shared + ple


def _post(hf, hb, p, wr, rb, wsg, wsu, wsd, wp, wpg):
    tm = ROUTE_T
    full = lambda r, c: pl.BlockSpec((r, c), lambda i: (0, 0))
    tile = lambda c: pl.BlockSpec((tm, c), lambda i: (i, 0))
    return pl.pallas_call(
        _post_kernel,
        grid=(N_RTILES,),
        in_specs=[tile(D_MODEL), tile(D_MODEL), tile(PLE_DIM),
                  full(D_MODEL, N_EXPERTS), full(1, N_EXPERTS),
                  full(D_MODEL, EXPERT_FF), full(D_MODEL, EXPERT_FF), full(EXPERT_FF, D_MODEL),
                  full(PLE_DIM, D_MODEL), full(D_MODEL, D_MODEL)],
        out_specs=[tile(D_MODEL), tile(TOP_K), tile(TOP_K),
                   pl.BlockSpec((None, 1, N_EXPERTS), lambda i: (i, 0, 0))],
        out_shape=[jax.ShapeDtypeStruct((N_TOK, D_MODEL), F32),
                   jax.ShapeDtypeStruct((N_TOK, TOP_K), I32),
                   jax.ShapeDtypeStruct((N_TOK, TOP_K), F32),
                   jax.ShapeDtypeStruct((N_RTILES, 1, N_EXPERTS), F32)],
        compiler_params=_cparams(("parallel",)),
        name="router_shared_ple",
    )(hf, hb, p, wr, rb, wsg, wsu, wsd, wp, wpg)


def _pos_kernel(idx_ref, base_ref, pos_ref):
    idx = idx_ref[...]
    tm = idx.shape[0]
    lane = lax.broadcasted_iota(I32, (tm, N_EXPERTS), 1)
    hits = [lane == idx[:, kk:kk + 1] for kk in range(TOP_K)]
    chosen = hits[0]
    for kk in range(1, TOP_K):
        chosen = chosen | hits[kk]
    r = lax.broadcasted_iota(I32, (tm, tm), 0)
    c = lax.broadcasted_iota(I32, (tm, tm), 1)
    earlier = jnp.where(c < r, 1.0, 0.0).astype(BF16)
    rank = _dot(earlier, jnp.where(chosen, 1.0, 0.0).astype(BF16))
    slot = rank + base_ref[...]
    lane_k = lax.broadcasted_iota(I32, (tm, TOP_K), 1)
    out = jnp.zeros((tm, TOP_K), I32)
    for kk in range(TOP_K):
        pk = jnp.sum(jnp.where(hits[kk], slot, 0.0), axis=-1, keepdims=True)
        out = jnp.where(lane_k == kk, pk.astype(I32), out)
    pos_ref[...] = out


def _slot_positions(idx, base):
    tm = ROUTE_T
    return pl.pallas_call(
        _pos_kernel,
        grid=(N_RTILES,),
        in_specs=[pl.BlockSpec((tm, TOP_K), lambda i: (i, 0)),
                  pl.BlockSpec((None, 1, N_EXPERTS), lambda i: (i, 0, 0))],
        out_specs=pl.BlockSpec((tm, TOP_K), lambda i: (i, 0)),
        out_shape=jax.ShapeDtypeStruct((N_TOK, TOP_K), I32),
        compiler_params=_cparams(("parallel",)),
        name="slot_positions",
    )(idx, base)


def _slot_layout(cnt):
    counts_te = cnt.reshape(N_RTILES, N_EXPERTS).astype(I32)
    counts = jnp.sum(counts_te, axis=0)
    padded = (counts + SLOT_BLOCK - 1) // SLOT_BLOCK * SLOT_BLOCK
    pad_end = jnp.cumsum(padded)
    pad_start = pad_end - padded
    tile_off = jnp.cumsum(counts_te, axis=0) - counts_te
    base = (pad_start[None, :] + tile_off).astype(F32).reshape(N_RTILES, 1, N_EXPERTS)
    block_expert = jnp.minimum(
        jnp.searchsorted(pad_end // SLOT_BLOCK, jnp.arange(N_BLOCKS, dtype=I32), side='right'),
        N_EXPERTS - 1).astype(I32)
    n_used = (pad_end[-1] // SLOT_BLOCK).astype(I32).reshape(1)
    return base, counts, pad_start, padded, block_expert, n_used


DISP_T = 256


def _dispatch_kernel(cnt_ref, ps_ref, pd_ref, nu_ref, pos_hbm, h3_ref, x_hbm,
                     pos_smem, zbuf, pos_sem, row_sem, pad_sem):
    i = pl.program_id(0)
    n = pl.num_programs(0)

    def pos_copy(blk, slot):
        return pltpu.make_async_copy(pos_hbm.at[blk], pos_smem.at[slot], pos_sem.at[slot])

    def pad_row(e, r):
        return pltpu.make_async_copy(zbuf.at[0], x_hbm.at[ps_ref[e] + r], pad_sem)

    def pad_block(b):
        return pltpu.make_async_copy(zbuf, x_hbm.at[pl.ds(b * SLOT_BLOCK, SLOT_BLOCK)], pad_sem)

    def for_each_pad(fn_row, fn_block):
        def per_expert(e, carry):
            def per_row(r, c2):
                fn_row(e, r)
                return c2
            return lax.fori_loop(cnt_ref[e], pd_ref[e], per_row, carry)
        lax.fori_loop(0, N_EXPERTS, per_expert, 0)

        def per_block(b, carry):
            fn_block(b)
            return carry
        lax.fori_loop(nu_ref[0], N_BLOCKS, per_block, 0)

    @pl.when(i == 0)
    def _():
        pos_copy(0, 0).start()
        pos_copy(1, 1).start()
        zbuf[...] = jnp.zeros(zbuf.shape, F32)
        for_each_pad(lambda e, r: pad_row(e, r).start(), lambda b: pad_block(b).start())
        for_each_pad(lambda e, r: pad_row(e, r).wait(), lambda b: pad_block(b).wait())

    slot = i % 2
    pos_copy(i, slot).wait()

    def body(t, carry):
        for kk in range(TOP_K):
            dst = pos_smem[slot, t * TOP_K + kk]
            pltpu.make_async_copy(h3_ref.at[t], x_hbm.at[dst], row_sem).start()
        return carry
    lax.fori_loop(0, DISP_T, body, 0)

    @pl.when(i + 2 < n)
    def _():
        pos_copy(i + 2, slot).start()

    for kk in range(TOP_K):
        pltpu.make_async_copy(h3_ref, x_hbm.at[pl.ds(0, DISP_T)], row_sem).wait()


def _dispatch(counts, pad_start, padded, n_used, pos, h3):
    n_tiles = N_TOK // DISP_T
    grid_spec = pltpu.PrefetchScalarGridSpec(
        num_scalar_prefetch=4,
        grid=(n_tiles,),
        in_specs=[pl.BlockSpec(memory_space=pl.ANY),
                  pl.BlockSpec((DISP_T, D_CHUNKS, LANES), lambda i, *_: (i, 0, 0))],
        out_specs=pl.BlockSpec(memory_space=pl.ANY),
        scratch_shapes=[pltpu.SMEM((2, DISP_T * TOP_K), I32),
                        pltpu.VMEM((SLOT_BLOCK, D_CHUNKS, LANES), F32),
                        pltpu.SemaphoreType.DMA((2,)),
                        pltpu.SemaphoreType.DMA,
                        pltpu.SemaphoreType.DMA])
    return pl.pallas_call(
        _dispatch_kernel,
        grid_spec=grid_spec,
        out_shape=jax.ShapeDtypeStruct((CAP, D_CHUNKS, LANES), F32),
        compiler_params=_cparams(("arbitrary",)),
        name="dispatch_scatter",
    )(counts, pad_start, padded, n_used, pos.reshape(n_tiles, DISP_T * TOP_K), h3)


def _moe_kernel(be_ref, nu_ref, x_ref, wg_ref, wu_ref, wd_ref, y_ref, wg_bf, wu_bf, wd_bf):
    i = pl.program_id(0)
    n_used = nu_ref[0]
    changed = jnp.logical_or(i == 0, be_ref[i] != be_ref[jnp.maximum(i - 1, 0)])

    @pl.when(changed)
    def _():
        wg_bf[...] = wg_ref[...].astype(BF16)
        wu_bf[...] = wu_ref[...].astype(BF16)
        wd_bf[...] = wd_ref[...].astype(BF16)

    @pl.when(i < n_used)
    def _():
        x = jnp.concatenate([x_ref[:, cc, :] for cc in range(D_CHUNKS)], axis=1).astype(BF16)
        hid = (_silu(_dot(x, wg_bf[...])) * _dot(x, wu_bf[...])).astype(BF16)
        y = _dot(hid, wd_bf[...])
        for cc in range(D_CHUNKS):
            y_ref[:, cc, :] = y[:, cc * LANES:(cc + 1) * LANES]

    @pl.when(i >= n_used)
    def _():
        y_ref[...] = jnp.zeros(y_ref.shape, F32)


def _moe(block_expert, n_used, xs, wg, wu, wd, layer):
    grid_spec = pltpu.PrefetchScalarGridSpec(
        num_scalar_prefetch=2,
        grid=(N_BLOCKS,),
        in_specs=[pl.BlockSpec((SLOT_BLOCK, D_CHUNKS, LANES),
                               lambda i, be, nu: (jnp.minimum(i, nu[0] - 1), 0, 0)),
                  pl.BlockSpec((None, None, D_MODEL, EXPERT_FF), lambda i, be, nu: (layer, be[i], 0, 0)),
                  pl.BlockSpec((None, None, D_MODEL, EXPERT_FF), lambda i, be, nu: (layer, be[i], 0, 0)),
                  pl.BlockSpec((None, None, EXPERT_FF, D_MODEL), lambda i, be, nu: (layer, be[i], 0, 0))],
        out_specs=pl.BlockSpec((SLOT_BLOCK, D_CHUNKS, LANES), lambda i, be, nu: (i, 0, 0)),
        scratch_shapes=[pltpu.VMEM((D_MODEL, EXPERT_FF), BF16),
                        pltpu.VMEM((D_MODEL, EXPERT_FF), BF16),
                        pltpu.VMEM((EXPERT_FF, D_MODEL), BF16)])
    return pl.pallas_call(
        _moe_kernel,
        grid_spec=grid_spec,
        out_shape=jax.ShapeDtypeStruct((CAP, D_CHUNKS, LANES), F32),
        compiler_params=_cparams(("arbitrary",)),
        name="routed_experts",
    )(block_expert, n_used, xs, wg, wu, wd)


COMB_T = 128


def _combine_kernel(pos_hbm, y_hbm, w_ref, base_ref, g_ref, b_ref, hf_ref, hb_ref,
                    pos_smem, buf, pos_sem, row_sem):
    i = pl.program_id(0)
    n = pl.num_programs(0)

    def pos_copy(blk, slot):
        return pltpu.make_async_copy(pos_hbm.at[blk], pos_smem.at[slot], pos_sem.at[slot])

    def issue_rows(slot):
        def body(t, carry):
            for kk in range(TOP_K):
                src = pos_smem[slot, t * TOP_K + kk]
                pltpu.make_async_copy(y_hbm.at[src], buf.at[slot, t, kk], row_sem.at[slot]).start()
            return carry
        lax.fori_loop(0, COMB_T, body, 0)

    def wait_rows(slot):
        for kk in range(TOP_K):
            pltpu.make_async_copy(y_hbm.at[pl.ds(0, COMB_T)], buf.at[slot, :, kk],
                                  row_sem.at[slot]).wait()

    @pl.when(i == 0)
    def _():
        pos_copy(0, 0).start()
        pos_copy(1, 1).start()
        pos_copy(0, 0).wait()
        issue_rows(0)

    nxt = i + 1
    nslot = nxt % 2

    @pl.when(nxt < n)
    def _():
        pos_copy(nxt, nslot).wait()
        issue_rows(nslot)

    @pl.when(i + 2 < n)
    def _():
        pos_copy(i + 2, i % 2).start()

    slot = i % 2
    wait_rows(slot)
    rows = buf.at[slot]
    routed = None
    for kk in range(TOP_K):
        yk = jnp.concatenate([rows[:, kk, cc, :] for cc in range(D_CHUNKS)], axis=1)
        term = w_ref[:, kk:kk + 1] * yk
        routed = term if routed is None else routed + term
    h2 = _layer_norm(base_ref[...] + routed, g_ref[...], b_ref[...])
    hf_ref[...] = h2
    hb_ref[...] = h2.astype(BF16)


def _combine(pos, y3, wts, base, g, b):
    tm = COMB_T
    n_tiles = N_TOK // tm
    full = lambda r, c: pl.BlockSpec((r, c), lambda i: (0, 0))
    tile = lambda c: pl.BlockSpec((tm, c), lambda i: (i, 0))
    return pl.pallas_call(
        _combine_kernel,
        grid=(n_tiles,),
        in_specs=[pl.BlockSpec(memory_space=pl.ANY), pl.BlockSpec(memory_space=pl.ANY),
                  tile(TOP_K), tile(D_MODEL), full(1, D_MODEL), full(1, D_MODEL)],
        out_specs=[tile(D_MODEL), tile(D_MODEL)],
        out_shape=[jax.ShapeDtypeStruct((N_TOK, D_MODEL), F32),
                   jax.ShapeDtypeStruct((N_TOK, D_MODEL), BF16)],
        scratch_shapes=[pltpu.SMEM((2, tm * TOP_K), I32),
                        pltpu.VMEM((2, tm, TOP_K, D_CHUNKS, LANES), F32),
                        pltpu.SemaphoreType.DMA((2,)),
                        pltpu.SemaphoreType.DMA((2,))],
        compiler_params=_cparams(("arbitrary",)),
        name="combine_ln2",
    )(pos.reshape(n_tiles, tm * TOP_K), y3, wts, base, g, b)


def _pack_w_in(w):
    q_lat = w[:, 0:384]
    c_kv = w[:, 384:640]
    k_pe = w[:, 640:672]
    z = w[:, 672:1696]
    xbc = w[:, 1696:3232]
    dt = w[:, 3232:3248]
    gate_a = w[:, 3248:4272]
    gate_b = w[:, 4272:5296]
    zeros = lambda n: jnp.zeros((D_MODEL, n), w.dtype)
    small = jnp.concatenate([zeros(KPE_LANE), k_pe, dt, zeros(LANES - DT_LANE - SSM_HEADS)], axis=1)
    return jnp.concatenate([gate_a, gate_b, z, xbc, q_lat, small, c_kv], axis=1).astype(BF16)


def _pack_w_q(w):
    w = w.reshape(Q_RANK, MLA_HEADS, QK_NOPE + QK_ROPE)
    w = jnp.pad(w, ((0, 0), (0, 0), (0, HEAD_PAD - QK_NOPE - QK_ROPE)))
    return w.reshape(Q_RANK, MLA_HEADS * HEAD_PAD).astype(BF16)


def _pack_w_kv(w):
    w = w.reshape(KV_RANK, MLA_HEADS, QK_NOPE + V_DIM)
    wk = jnp.pad(w[:, :, :QK_NOPE], ((0, 0), (0, 0), (0, HEAD_PAD - QK_NOPE)))
    wv = w[:, :, QK_NOPE:]
    return (wk.reshape(KV_RANK, MLA_HEADS * HEAD_PAD).astype(BF16),
            wv.reshape(KV_RANK, MLA_HEADS * V_DIM).astype(BF16))


def _rope_tables(positions):
    half = QK_ROPE // 2
    inv_freq = jnp.power(ROPE_THETA, -jnp.arange(half, dtype=F32) * (2.0 / QK_ROPE))
    ang = positions.astype(F32).reshape(N_TOK, 1) * inv_freq
    cos, sin = jnp.cos(ang), jnp.sin(ang)
    z = lambda n: jnp.zeros((N_TOK, n), F32)
    tail = HEAD_PAD - QK_NOPE - QK_ROPE
    tc = jnp.concatenate([jnp.ones((N_TOK, QK_NOPE), F32), cos, cos, z(tail)], axis=1)
    ts1 = jnp.concatenate([z(QK_NOPE), -sin, z(half), z(tail)], axis=1)
    ts2 = jnp.concatenate([z(QK_NOPE), z(half), sin, z(tail)], axis=1)
    return tc, ts1, ts2


def _head_lane_row(v):
    return jnp.zeros((1, LANES), F32).at[0, DT_LANE:DT_LANE + SSM_HEADS].set(v.astype(F32))


def _expand_matrix():
    r = jnp.arange(LANES)[:, None]
    c = jnp.arange(SSM_INNER)[None, :]
    return ((r - DT_LANE) == (c // SSM_HEADDIM)).astype(F32)


def kernel(x, p, positions, w_in, q_norm, w_q_up, kv_norm, w_kv_up, conv_w, conv_b, dt_bias, a_log, d_skip,
           ssm_norm, w_attn_br, w_ssm_br, w_o, ln1_g, ln1_b, w_router, router_bias, w_exp_gate, w_exp_up,
           w_exp_down, w_sh_gate, w_sh_up, w_sh_down, w_ple, w_ple_gate, ln2_g, ln2_b):
    tc, ts1, ts2 = _rope_tables(positions)
    e_mat = _expand_matrix()
    hf = x.reshape(N_TOK, D_MODEL)
    hb = hf.astype(BF16)
    row = lambda v: v.reshape(1, -1).astype(F32)
    for i in range(DEPTH):
        proj = _in_proj(hb, _pack_w_in(w_in[i]))
        q = _q_up(proj, row(q_norm[i]), _pack_w_q(w_q_up[i]), tc, ts1, ts2)
        wk, wv = _pack_w_kv(w_kv_up[i])
        k, v = _kv_up(proj, row(kv_norm[i]), wk, wv, tc, ts1, ts2)
        attn = _attention(q, k, v).reshape(N_TOK, MLA_HEADS * V_DIM)
        ssm_y = _ssd(proj, conv_w[i], row(conv_b[i]), _head_lane_row(dt_bias[i]), _head_lane_row(a_log[i]),
                     row(jnp.repeat(d_skip[i], SSM_HEADDIM)), row(ssm_norm[i]), e_mat)
        hf, hb = _merge(attn, ssm_y, proj, hf, w_attn_br[i].astype(BF16), w_ssm_br[i].astype(BF16),
                        w_o[i].astype(BF16), row(ln1_g[i]), row(ln1_b[i]))
        base, idx, wts, cnt = _post(hf, hb, p[i].reshape(N_TOK, PLE_DIM), w_router[i], row(router_bias[i]),
                                    w_sh_gate[i].astype(BF16), w_sh_up[i].astype(BF16),
                                    w_sh_down[i].astype(BF16), w_ple[i].astype(BF16),
                                    w_ple_gate[i].astype(BF16))
        base_slots, counts, pad_start, padded, block_expert, n_used = _slot_layout(cnt)
        pos = _slot_positions(idx, base_slots)
        xs = _dispatch(counts, pad_start, padded, n_used, pos, hf.reshape(N_TOK, D_CHUNKS, LANES))
        y3 = _moe(block_expert, n_used, xs, w_exp_gate, w_exp_up, w_exp_down, i)
        hf, hb = _combine(pos, y3, wts, base, row(ln2_g[i]), row(ln2_b[i]))
    return hf.reshape(BATCH, SEQ, D_MODEL)
```

```python
import functools
import math

import jax
import jax.numpy as jnp
from jax import lax
from jax.experimental import pallas as pl
from jax.experimental.pallas import tpu as pltpu

F32 = jnp.float32
BF16 = jnp.bfloat16
I32 = jnp.int32
HIGHEST = lax.Precision.HIGHEST

D_MODEL = 1024
BATCH = 4
SEQ = 8192
DEPTH = 2
N_TOK = BATCH * SEQ
MLA_HEADS = 8
QK_NOPE = 64
QK_ROPE = 32
V_DIM = 64
Q_RANK = 384
KV_RANK = 256
ROPE_THETA = 10000.0
SSM_INNER = 1024
SSM_HEADDIM = 64
SSM_HEADS = 16
SSM_GROUPS = 2
SSM_STATE = 128
SSM_CONV = 4
SSM_CHUNK = 128
SSM_CONV_DIM = 1536
N_EXPERTS = 256
TOP_K = 8
N_GROUPS = 8
TOPK_GROUPS = 4
PER_GROUP = N_EXPERTS // N_GROUPS
EXPERT_FF = 256
ROUTED_SCALE = 2.5
PLE_DIM = 256
ALPHA = (2 * DEPTH) ** 0.25
LN_EPS = 1e-5
RMS_EPS = 1e-6

LANES = 128
SUBLANES = 8
VMEM_LIMIT = 48 * 1024 * 1024

COL_GATE_A = 0
COL_GATE_B = 1024
COL_Z = 2048
COL_XBC = 3072
COL_QLAT = 4608
COL_SMALL = 4992
COL_CKV = 5120
PACK_COLS = 5376
KPE_LANE = 64
DT_LANE = 96

HEAD_PAD = 128
N_PAIRS = MLA_HEADS // 2
V_SLAB = 2 * LANES

ROUTE_T = 256
N_RTILES = N_TOK // ROUTE_T
SLOT_BLOCK = 128
N_ASSIGN = N_TOK * TOP_K
N_BLOCKS = N_ASSIGN // SLOT_BLOCK + N_EXPERTS
CAP = N_BLOCKS * SLOT_BLOCK
D_CHUNKS = D_MODEL // LANES

NEG = float(jnp.finfo(jnp.float32).min)


def _cparams(sem):
    return pltpu.CompilerParams(dimension_semantics=sem, vmem_limit_bytes=VMEM_LIMIT)


def _sigmoid(x):
    return 1.0 / (1.0 + jnp.exp(-x))


def _silu(x):
    return x * _sigmoid(x)


def _layer_norm(x, g, b):
    mu = jnp.mean(x, axis=-1, keepdims=True)
    xc = x - mu
    var = jnp.mean(xc * xc, axis=-1, keepdims=True)
    return xc * lax.rsqrt(var + LN_EPS) * g + b


def _rms_norm(x, g):
    return x * lax.rsqrt(jnp.mean(x * x, axis=-1, keepdims=True) + RMS_EPS) * g


def _dot(a, b):
    return jnp.dot(a, b, preferred_element_type=F32)


def _swap_sublane_major(x):
    groups = x.shape[0]
    for d in (4, 2, 1):
        y = x.reshape(groups, SUBLANES // (2 * d), 2, d, SUBLANES, LANES)
        lo, hi = y[:, :, 0], y[:, :, 1]
        shp = lo.shape
        keep = (lax.broadcasted_iota(I32, shp, 3) & d) == 0
        rot = lambda v, s: pltpu.roll(v.reshape(-1, SUBLANES, LANES), s, 1).reshape(shp)
        new_lo = jnp.where(keep, lo, rot(hi, d))
        new_hi = jnp.where(keep, rot(lo, SUBLANES - d), hi)
        x = jnp.stack([new_lo, new_hi], axis=2).reshape(groups, SUBLANES, SUBLANES, LANES)
    return x


def _rows_to_matrix(x3):
    rows = x3.shape[0]
    t = _swap_sublane_major(x3.reshape(rows // SUBLANES, SUBLANES, D_CHUNKS, LANES))
    return jnp.concatenate([t[:, cc].reshape(rows, LANES) for cc in range(D_CHUNKS)], axis=1)


def _matrix_to_rows(y):
    rows = y.shape[0]
    t = jnp.stack([y[:, cc * LANES:(cc + 1) * LANES].reshape(rows // SUBLANES, SUBLANES, LANES)
                   for cc in range(D_CHUNKS)], axis=1)
    return _swap_sublane_major(t).reshape(rows, D_CHUNKS, LANES)


def _mm_kernel(x_ref, w_ref, o_ref):
    o_ref[...] = _dot(x_ref[...], w_ref[...]).astype(o_ref.dtype)


def _in_proj(hb, w_pack):
    tm, tn = 1024, 768
    return pl.pallas_call(
        _mm_kernel,
        grid=(PACK_COLS // tn, N_TOK // tm),
        in_specs=[pl.BlockSpec((tm, D_MODEL), lambda j, i: (i, 0)),
                  pl.BlockSpec((D_MODEL, tn), lambda j, i: (0, j))],
        out_specs=pl.BlockSpec((tm, tn), lambda j, i: (i, j)),
        out_shape=jax.ShapeDtypeStruct((N_TOK, PACK_COLS), F32),
        compiler_params=_cparams(("parallel", "parallel")),
        name="in_proj",
    )(hb, w_pack)


def _rope128(x, c, s1, s2):
    return x * c + pltpu.roll(x, LANES - 16, 1) * s1 + pltpu.roll(x, 16, 1) * s2


def _q_up_kernel(ql_ref, g_ref, w_ref, c_ref, s1_ref, s2_ref, q_ref):
    y = _rms_norm(ql_ref[...], g_ref[...]).astype(BF16)
    q = _dot(y, w_ref[...])
    c, s1, s2 = c_ref[...], s1_ref[...], s2_ref[...]
    scale = (QK_NOPE + QK_ROPE) ** -0.5 * math.log2(math.e)
    for h in range(MLA_HEADS):
        qh = _rope128(q[:, h * HEAD_PAD:(h + 1) * HEAD_PAD], c, s1, s2)
        q_ref[h] = (qh * scale).astype(BF16)


def _q_up(proj, g, wq, tc, ts1, ts2):
    tm = 512
    spt = SEQ // tm
    tab = pl.BlockSpec((tm, LANES), lambda i: (i, 0))
    return pl.pallas_call(
        _q_up_kernel,
        grid=(N_TOK // tm,),
        in_specs=[pl.BlockSpec((tm, Q_RANK), lambda i: (i, COL_QLAT // Q_RANK)),
                  pl.BlockSpec((1, Q_RANK), lambda i: (0, 0)),
                  pl.BlockSpec((Q_RANK, MLA_HEADS * HEAD_PAD), lambda i: (0, 0)),
                  tab, tab, tab],
        out_specs=pl.BlockSpec((None, MLA_HEADS, tm, HEAD_PAD), lambda i: (i // spt, 0, i % spt, 0)),
        out_shape=jax.ShapeDtypeStruct((BATCH, MLA_HEADS, SEQ, HEAD_PAD), BF16),
        compiler_params=_cparams(("parallel",)),
        name="q_up",
    )(proj, g, wq, tc, ts1, ts2)


def _kv_up_kernel(ckv_ref, small_ref, g_ref, wk_ref, wv_ref, c_ref, s1_ref, s2_ref, k_ref, v_ref):
    y = _rms_norm(ckv_ref[...], g_ref[...]).astype(BF16)
    k_all = _dot(y, wk_ref[...])
    v_all = _dot(y, wv_ref[...])
    lane = lax.broadcasted_iota(I32, small_ref.shape, 1)
    kpe_raw = jnp.where((lane >= KPE_LANE) & (lane < KPE_LANE + QK_ROPE), small_ref[...], 0.0)
    kpe = _rope128(kpe_raw, c_ref[...], s1_ref[...], s2_ref[...])
    for h in range(MLA_HEADS):
        k_ref[h] = (k_all[:, h * HEAD_PAD:(h + 1) * HEAD_PAD] + kpe).astype(BF16)
    ones = jnp.ones((v_all.shape[0], LANES), F32)
    for j in range(N_PAIRS):
        v_ref[j] = jnp.concatenate([v_all[:, j * LANES:(j + 1) * LANES], ones], axis=1).astype(BF16)


def _kv_up(proj, g, wk, wv, tc, ts1, ts2):
    tm = 512
    spt = SEQ // tm
    tab = pl.BlockSpec((tm, LANES), lambda i: (i, 0))
    return pl.pallas_call(
        _kv_up_kernel,
        grid=(N_TOK // tm,),
        in_specs=[pl.BlockSpec((tm, KV_RANK), lambda i: (i, COL_CKV // KV_RANK)),
                  pl.BlockSpec((tm, LANES), lambda i: (i, COL_SMALL // LANES)),
                  pl.BlockSpec((1, KV_RANK), lambda i: (0, 0)),
                  pl.BlockSpec((KV_RANK, MLA_HEADS * HEAD_PAD), lambda i: (0, 0)),
                  pl.BlockSpec((KV_RANK, MLA_HEADS * V_DIM), lambda i: (0, 0)),
                  tab, tab, tab],
        out_specs=[pl.BlockSpec((None, MLA_HEADS, tm, HEAD_PAD), lambda i: (i // spt, 0, i % spt, 0)),
                   pl.BlockSpec((None, N_PAIRS, tm, V_SLAB), lambda i: (i // spt, 0, i % spt, 0))],
        out_shape=[jax.ShapeDtypeStruct((BATCH, MLA_HEADS, SEQ, HEAD_PAD), BF16),
                   jax.ShapeDtypeStruct((BATCH, N_PAIRS, SEQ, V_SLAB), BF16)],
        compiler_params=_cparams(("parallel",)),
        name="kv_up",
    )(proj, proj, g, wk, wv, tc, ts1, ts2)


ATT_T = 512


def _attn_kernel(q_ref, k_ref, v_ref, o_ref, m_ref, acc_ref):
    qi = pl.program_id(2)
    t = ATT_T
    m_ref[...] = jnp.full(m_ref.shape, -jnp.inf, F32)
    acc_ref[...] = jnp.zeros(acc_ref.shape, F32)

    def kv_tile(kt, masked):
        ks = pl.multiple_of(kt * t, t)
        v = v_ref[pl.ds(ks, t), :]
        for hh in range(2):
            s = lax.dot_general(q_ref[hh], k_ref[hh, pl.ds(ks, t), :], (((1,), (1,)), ((), ())),
                                preferred_element_type=F32)
            if masked:
                row = lax.broadcasted_iota(I32, s.shape, 0)
                col = lax.broadcasted_iota(I32, s.shape, 1)
                s = jnp.where(col <= row, s, NEG)
            m_prev = m_ref[hh]
            m_new = jnp.maximum(m_prev, jnp.max(s, axis=-1, keepdims=True))
            alpha = jnp.exp2(m_prev - m_new)
            p = jnp.exp2(s - jnp.concatenate([m_new] * (t // LANES), axis=1))
            acc_ref[hh] = jnp.concatenate([alpha, alpha], axis=1) * acc_ref[hh] + _dot(p.astype(BF16), v)
            m_ref[hh] = m_new

    def body(kt, carry):
        kv_tile(kt, False)
        return carry

    lax.fori_loop(0, qi, body, 0)
    kv_tile(qi, True)

    lane = lax.broadcasted_iota(I32, o_ref.shape, 1)
    a0 = acc_ref[0]
    a1 = acc_ref[1]
    o0 = a0[:, :LANES] / a0[:, LANES:]
    o1 = a1[:, :LANES] / a1[:, LANES:]
    o_ref[...] = jnp.where(lane < V_DIM, o0, o1).astype(o_ref.dtype)


def _attention(q, k, v):
    t = ATT_T
    return pl.pallas_call(
        _attn_kernel,
        grid=(BATCH, N_PAIRS, SEQ // t),
        in_specs=[pl.BlockSpec((None, 2, t, HEAD_PAD), lambda b, j, qi: (b, j, qi, 0)),
                  pl.BlockSpec((None, 2, SEQ, HEAD_PAD), lambda b, j, qi: (b, j, 0, 0)),
                  pl.BlockSpec((None, None, SEQ, V_SLAB), lambda b, j, qi: (b, j, 0, 0))],
        out_specs=pl.BlockSpec((None, t, LANES), lambda b, j, qi: (b, qi, j)),
        out_shape=jax.ShapeDtypeStruct((BATCH, SEQ, MLA_HEADS * V_DIM), BF16),
        scratch_shapes=[pltpu.VMEM((2, t, LANES), F32), pltpu.VMEM((2, t, V_SLAB), F32)],
        compiler_params=_cparams(("parallel", "parallel", "parallel")),
        name="mla_attention",
    )(q, k, v)


HALO = SUBLANES
HEADS_PER_GROUP = SSM_HEADS // SSM_GROUPS
GROUP_W = SSM_INNER // SSM_GROUPS


def _ssd_kernel(xbc_ref, halo_ref, z_ref, small_ref, cw_ref, cb_ref, dtb_ref, alog_ref, dexp_ref,
                nw_ref, e_ref, y_ref, st_ref, cat_ref):
    c = pl.program_id(1)
    L = SSM_CHUNK

    @pl.when(c == 0)
    def _():
        st_ref[...] = jnp.zeros(st_ref.shape, F32)

    cat_ref[0:HALO, :] = jnp.where(c == 0, 0.0, halo_ref[...])
    cat_ref[HALO:HALO + L, :] = xbc_ref[...]
    acc = jnp.broadcast_to(cb_ref[...], (L, SSM_CONV_DIM))
    for kk in range(SSM_CONV):
        off = HALO - (SSM_CONV - 1) + kk
        acc = acc + cw_ref[kk:kk + 1, :] * cat_ref[off:off + L, :]
    xc = _silu(acc)
    xs = xc[:, :SSM_INNER]
    bm = xc[:, SSM_INNER:SSM_INNER + SSM_GROUPS * SSM_STATE]
    cm = xc[:, SSM_INNER + SSM_GROUPS * SSM_STATE:]

    lane = lax.broadcasted_iota(I32, (L, LANES), 1)
    dt_lanes = (lane >= DT_LANE) & (lane < DT_LANE + SSM_HEADS)
    dt_in = small_ref[...] + dtb_ref[...]
    dt = jnp.maximum(dt_in, 0.0) + jnp.log1p(jnp.exp(-jnp.abs(dt_in)))
    dt = jnp.where(dt_lanes, dt, 0.0)
    a = -jnp.exp(alog_ref[...])
    a_dt = dt * a
    row = lax.broadcasted_iota(I32, (L, L), 0)
    col = lax.broadcasted_iota(I32, (L, L), 1)
    tri = row >= col
    a_cum = jnp.dot(tri.astype(F32), a_dt, precision=HIGHEST, preferred_element_type=F32)
    a_last = a_cum[L - 1:L, :]
    e = e_ref[...]

    def expand(t):
        return jnp.dot(t, e, precision=HIGHEST, preferred_element_type=F32)

    x_dt = xs * expand(dt)
    eac_x = expand(jnp.exp(a_cum))
    ds_x = expand(jnp.exp(a_last - a_cum))
    cd_x = eac_x[L - 1:L, :]
    xd = (x_dt * ds_x).astype(BF16)
    x_bf = x_dt.astype(BF16)
    a_cum_t = a_cum.T
    lane_h = lax.broadcasted_iota(I32, (L, LANES), 1)

    y_parts = []
    for g in range(SSM_GROUPS):
        bg = bm[:, g * SSM_STATE:(g + 1) * SSM_STATE].astype(BF16)
        cg = cm[:, g * SSM_STATE:(g + 1) * SSM_STATE].astype(BF16)
        cb = lax.dot_general(cg, bg, (((1,), (1,)), ((), ())), preferred_element_type=F32)
        st_prev = st_ref[g]
        y_off = _dot(cg, st_prev.astype(BF16)) * eac_x[:, g * GROUP_W:(g + 1) * GROUP_W]
        for jp in range(HEADS_PER_GROUP // 2):
            pair = g * (HEADS_PER_GROUP // 2) + jp
            xp = x_bf[:, pair * LANES:(pair + 1) * LANES]
            outs = []
            for hh in range(2):
                hl = DT_LANE + 2 * pair + hh
                seg = a_cum[:, hl:hl + 1] - a_cum_t[hl:hl + 1, :]
                decay = jnp.where(tri, jnp.exp(seg), 0.0)
                outs.append(_dot((cb * decay).astype(BF16), xp))
            y_parts.append(jnp.where(lane_h < SSM_HEADDIM, outs[0], outs[1])
                           + y_off[:, jp * LANES:(jp + 1) * LANES])
        upd = lax.dot_general(bg, xd[:, g * GROUP_W:(g + 1) * GROUP_W], (((0,), (0,)), ((), ())),
                              preferred_element_type=F32)
        st_ref[g] = st_prev * cd_x[:, g * GROUP_W:(g + 1) * GROUP_W] + upd

    y = jnp.concatenate(y_parts, axis=1) + xs * dexp_ref[...]
    y = y * _silu(z_ref[...])
    normed = []
    for g in range(SSM_GROUPS):
        yg = y[:, g * GROUP_W:(g + 1) * GROUP_W]
        normed.append(yg * lax.rsqrt(jnp.mean(yg * yg, axis=-1, keepdims=True) + RMS_EPS))
    y_ref[...] = (jnp.concatenate(normed, axis=1) * nw_ref[...]).astype(y_ref.dtype)


def _ssd(proj, cw, cb, dtb, alog, dexp, nw, e_mat):
    L = SSM_CHUNK
    nc = SEQ // L
    xbc_blk = COL_XBC // SSM_CONV_DIM

    def row1(w):
        return pl.BlockSpec((1, w), lambda b, c: (0, 0))

    return pl.pallas_call(
        _ssd_kernel,
        grid=(BATCH, nc),
        in_specs=[pl.BlockSpec((L, SSM_CONV_DIM), lambda b, c: (b * nc + c, xbc_blk)),
                  pl.BlockSpec((HALO, SSM_CONV_DIM),
                               lambda b, c: (jnp.maximum((b * nc + c) * (L // HALO) - 1, 0), xbc_blk)),
                  pl.BlockSpec((L, SSM_INNER), lambda b, c: (b * nc + c, COL_Z // SSM_INNER)),
                  pl.BlockSpec((L, LANES), lambda b, c: (b * nc + c, COL_SMALL // LANES)),
                  pl.BlockSpec((SSM_CONV, SSM_CONV_DIM), lambda b, c: (0, 0)),
                  row1(SSM_CONV_DIM), row1(LANES), row1(LANES), row1(SSM_INNER), row1(SSM_INNER),
                  pl.BlockSpec((LANES, SSM_INNER), lambda b, c: (0, 0))],
        out_specs=pl.BlockSpec((L, SSM_INNER), lambda b, c: (b * nc + c, 0)),
        out_shape=jax.ShapeDtypeStruct((N_TOK, SSM_INNER), BF16),
        scratch_shapes=[pltpu.VMEM((SSM_GROUPS, SSM_STATE, GROUP_W), F32),
                        pltpu.VMEM((HALO + L, SSM_CONV_DIM), F32)],
        compiler_params=_cparams(("parallel", "arbitrary")),
        name="mamba2_ssd",
    )(proj, proj, proj, proj, cw, cb, dtb, alog, dexp, nw, e_mat)


def _merge_kernel(attn_ref, ssm_ref, ga_ref, gb_ref, h_ref, wa_ref, ws_ref, wo_ref, g_ref, b_ref,
                  hf_ref, hb_ref):
    ya = _dot(attn_ref[...], wa_ref[...])
    yb = _dot(ssm_ref[...], ws_ref[...])
    mix = _sigmoid(ga_ref[...]) * ya + _sigmoid(gb_ref[...]) * yb
    mixed = _dot(mix.astype(BF16), wo_ref[...])
    h1 = _layer_norm(ALPHA * h_ref[...] + mixed, g_ref[...], b_ref[...])
    hf_ref[...] = h1
    hb_ref[...] = h1.astype(BF16)


def _merge(attn, ssm_y, proj, hf, wa, ws, wo, g, b):
    tm = 256
    full = lambda r, c: pl.BlockSpec((r, c), lambda i: (0, 0))
    tile = lambda c, j=0: pl.BlockSpec((tm, c), lambda i: (i, j))
    return pl.pallas_call(
        _merge_kernel,
        grid=(N_TOK // tm,),
        in_specs=[tile(MLA_HEADS * V_DIM), tile(SSM_INNER), tile(D_MODEL, COL_GATE_A // D_MODEL),
                  tile(D_MODEL, COL_GATE_B // D_MODEL), tile(D_MODEL),
                  full(MLA_HEADS * V_DIM, D_MODEL), full(SSM_INNER, D_MODEL), full(D_MODEL, D_MODEL),
                  full(1, D_MODEL), full(1, D_MODEL)],
        out_specs=[tile(D_MODEL), tile(D_MODEL)],
        out_shape=[jax.ShapeDtypeStruct((N_TOK, D_MODEL), F32),
                   jax.ShapeDtypeStruct((N_TOK, D_MODEL), BF16)],
        compiler_params=_cparams(("parallel",)),
        name="merge_ln1",
    )(attn, ssm_y, proj, proj, hf, wa, ws, wo, g, b)


def _first_argmax(vals, lane):
    m = jnp.max(vals, axis=-1, keepdims=True)
    idx = jnp.min(jnp.where(vals == m, lane, float(N_EXPERTS)), axis=-1, keepdims=True)
    return m, idx


def _post_kernel(hf_ref, hb_ref, p_ref, wr_ref, rb_ref, wsg_ref, wsu_ref, wsd_ref, wp_ref, wpg_ref,
                 base_ref, idx_ref, wt_ref, cnt_ref):
    hf = hf_ref[...]
    hb = hb_ref[...]
    tm = hf.shape[0]
    logits = jnp.dot(hf, wr_ref[...], precision=HIGHEST, preferred_element_type=F32)
    scores = _sigmoid(logits)
    sel = scores + rb_ref[...]
    lane_i = lax.broadcasted_iota(I32, (tm, N_EXPERTS), 1)
    grp = lane_i // PER_GROUP
    lane = lane_i.astype(F32)

    grp_scores = []
    for g in range(N_GROUPS):
        vals = jnp.where(grp == g, sel, -jnp.inf)
        m1, i1 = _first_argmax(vals, lane)
        m2 = jnp.max(jnp.where(lane == i1, -jnp.inf, vals), axis=-1, keepdims=True)
        grp_scores.append(m1 + m2)
    keep = jnp.zeros((tm, N_EXPERTS), jnp.bool_)
    for g in range(N_GROUPS):
        rank = jnp.zeros((tm, 1), I32)
        for o in range(N_GROUPS):
            if o == g:
                continue
            ahead = (grp_scores[o] > grp_scores[g]) if o > g else (grp_scores[o] >= grp_scores[g])
            rank = rank + ahead.astype(I32)
        keep = keep | ((grp == g) & (rank < TOPK_GROUPS))
    masked = jnp.where(keep, sel, -jnp.inf)

    lane_k = lax.broadcasted_iota(I32, (tm, TOP_K), 1)
    idx_out = jnp.zeros((tm, TOP_K), I32)
    w_out = jnp.zeros((tm, TOP_K), F32)
    chosen = jnp.zeros((tm, N_EXPERTS), F32)
    for kk in range(TOP_K):
        _, ik = _first_argmax(masked, lane)
        hit = lane == ik
        wk = jnp.sum(jnp.where(hit, scores, 0.0), axis=-1, keepdims=True)
        masked = jnp.where(hit, -jnp.inf, masked)
        chosen = jnp.where(hit, 1.0, chosen)
        idx_out = jnp.where(lane_k == kk, ik.astype(I32), idx_out)
        w_out = jnp.where(lane_k == kk, wk, w_out)
    w_out = w_out / jnp.sum(w_out, axis=-1, keepdims=True) * ROUTED_SCALE
    idx_ref[...] = idx_out
    wt_ref[...] = w_out
    cnt_ref[...] = jnp.sum(chosen, axis=0, keepdims=True)

    shared = _dot((_silu(_dot(hb, wsg_ref[...])) * _dot(hb, wsu_ref[...])).astype(BF16), wsd_ref[...])
    ple = _dot(p_ref[...].astype(BF16), wp_ref[...]) * _sigmoid(_dot(hb, wpg_ref[...]))
    base_ref[...] = ALPHA * hf + shared + ple


def _post(hf, hb, p, wr, rb, wsg, wsu, wsd, wp, wpg):
    tm = ROUTE_T
    full = lambda r, c: pl.BlockSpec((r, c), lambda i: (0, 0))
    tile = lambda c: pl.BlockSpec((tm, c), lambda i: (i, 0))
    return pl.pallas_call(
        _post_kernel,
        grid=(N_RTILES,),
        in_specs=[tile(D_MODEL), tile(D_MODEL), tile(PLE_DIM),
                  full(D_MODEL, N_EXPERTS), full(1, N_EXPERTS),
                  full(D_MODEL, EXPERT_FF), full(D_MODEL, EXPERT_FF), full(EXPERT_FF, D_MODEL),
                  full(PLE_DIM, D_MODEL), full(D_MODEL, D_MODEL)],
        out_specs=[tile(D_MODEL), tile(TOP_K), tile(TOP_K),
                   pl.BlockSpec((None, 1, N_EXPERTS), lambda i: (i, 0, 0))],
        out_shape=[jax.ShapeDtypeStruct((N_TOK, D_MODEL), F32),
                   jax.ShapeDtypeStruct((N_TOK, TOP_K), I32),
                   jax.ShapeDtypeStruct((N_TOK, TOP_K), F32),
                   jax.ShapeDtypeStruct((N_RTILES, 1, N_EXPERTS), F32)],
        compiler_params=_cparams(("parallel",)),
        name="router_shared_ple",
    )(hf, hb, p, wr, rb, wsg, wsu, wsd, wp, wpg)


def _pos_kernel(idx_ref, base_ref, pos_ref):
    idx = idx_ref[...]
    tm = idx.shape[0]
    lane = lax.broadcasted_iota(I32, (tm, N_EXPERTS), 1)
    hits = [lane == idx[:, kk:kk + 1] for kk in range(TOP_K)]
    chosen = hits[0]
    for kk in range(1, TOP_K):
        chosen = chosen | hits[kk]
    r = lax.broadcasted_iota(I32, (tm, tm), 0)
    c = lax.broadcasted_iota(I32, (tm, tm), 1)
    earlier = jnp.where(c < r, 1.0, 0.0).astype(BF16)
    rank = _dot(earlier, jnp.where(chosen, 1.0, 0.0).astype(BF16))
    slot = rank + base_ref[...]
    lane_k = lax.broadcasted_iota(I32, (tm, TOP_K), 1)
    out = jnp.zeros((tm, TOP_K), I32)
    for kk in range(TOP_K):
        pk = jnp.sum(jnp.where(hits[kk], slot, 0.0), axis=-1, keepdims=True)
        out = jnp.where(lane_k == kk, pk.astype(I32), out)
    pos_ref[...] = out


def _slot_positions(idx, base):
    tm = ROUTE_T
    return pl.pallas_call(
        _pos_kernel,
        grid=(N_RTILES,),
        in_specs=[pl.BlockSpec((tm, TOP_K), lambda i: (i, 0)),
                  pl.BlockSpec((None, 1, N_EXPERTS), lambda i: (i, 0, 0))],
        out_specs=pl.BlockSpec((tm, TOP_K), lambda i: (i, 0)),
        out_shape=jax.ShapeDtypeStruct((N_TOK, TOP_K), I32),
        compiler_params=_cparams(("parallel",)),
        name="slot_positions",
    )(idx, base)


def _slot_layout(cnt):
    counts_te = cnt.reshape(N_RTILES, N_EXPERTS).astype(I32)
    counts = jnp.sum(counts_te, axis=0)
    padded = (counts + SLOT_BLOCK - 1) // SLOT_BLOCK * SLOT_BLOCK
    pad_end = jnp.cumsum(padded)
    pad_start = pad_end - padded
    tile_off = jnp.cumsum(counts_te, axis=0) - counts_te
    base = (pad_start[None, :] + tile_off).astype(F32).reshape(N_RTILES, 1, N_EXPERTS)
    block_expert = jnp.minimum(
        jnp.searchsorted(pad_end // SLOT_BLOCK, jnp.arange(N_BLOCKS, dtype=I32), side='right'),
        N_EXPERTS - 1).astype(I32)
    n_used = (pad_end[-1] // SLOT_BLOCK).astype(I32).reshape(1)
    return base, counts, pad_start, padded, block_expert, n_used


DISP_T = 256


def _dispatch_kernel(cnt_ref, ps_ref, pd_ref, nu_ref, pos_hbm, h3_ref, x_hbm,
                     pos_smem, zbuf, pos_sem, row_sem, pad_sem):
    i = pl.program_id(0)
    n = pl.num_programs(0)

    def pos_copy(blk, slot):
        return pltpu.make_async_copy(pos_hbm.at[blk], pos_smem.at[slot], pos_sem.at[slot])

    def pad_row(e, r):
        return pltpu.make_async_copy(zbuf.at[0], x_hbm.at[ps_ref[e] + r], pad_sem)

    def pad_block(b):
        return pltpu.make_async_copy(zbuf, x_hbm.at[pl.ds(b * SLOT_BLOCK, SLOT_BLOCK)], pad_sem)

    def for_each_pad(fn_row, fn_block):
        def per_expert(e, carry):
            def per_row(r, c2):
                fn_row(e, r)
                return c2
            return lax.fori_loop(cnt_ref[e], pd_ref[e], per_row, carry)
        lax.fori_loop(0, N_EXPERTS, per_expert, 0)

        def per_block(b, carry):
            fn_block(b)
            return carry
        lax.fori_loop(nu_ref[0], N_BLOCKS, per_block, 0)

    @pl.when(i == 0)
    def _():
        pos_copy(0, 0).start()
        pos_copy(1, 1).start()
        zbuf[...] = jnp.zeros(zbuf.shape, F32)
        for_each_pad(lambda e, r: pad_row(e, r).start(), lambda b: pad_block(b).start())
        for_each_pad(lambda e, r: pad_row(e, r).wait(), lambda b: pad_block(b).wait())

    slot = i % 2
    pos_copy(i, slot).wait()

    def body(t, carry):
        for kk in range(TOP_K):
            dst = pos_smem[slot, t * TOP_K + kk]
            pltpu.make_async_copy(h3_ref.at[t], x_hbm.at[dst], row_sem).start()
        return carry
    lax.fori_loop(0, DISP_T, body, 0)

    @pl.when(i + 2 < n)
    def _():
        pos_copy(i + 2, slot).start()

    for kk in range(TOP_K):
        pltpu.make_async_copy(h3_ref, x_hbm.at[pl.ds(0, DISP_T)], row_sem).wait()


def _dispatch(counts, pad_start, padded, n_used, pos, h3):
    n_tiles = N_TOK // DISP_T
    grid_spec = pltpu.PrefetchScalarGridSpec(
        num_scalar_prefetch=4,
        grid=(n_tiles,),
        in_specs=[pl.BlockSpec(memory_space=pl.ANY),
                  pl.BlockSpec((DISP_T, D_CHUNKS, LANES), lambda i, *_: (i, 0, 0))],
        out_specs=pl.BlockSpec(memory_space=pl.ANY),
        scratch_shapes=[pltpu.SMEM((2, DISP_T * TOP_K), I32),
                        pltpu.VMEM((SLOT_BLOCK, D_CHUNKS, LANES), F32),
                        pltpu.SemaphoreType.DMA((2,)),
                        pltpu.SemaphoreType.DMA,
                        pltpu.SemaphoreType.DMA])
    return pl.pallas_call(
        _dispatch_kernel,
        grid_spec=grid_spec,
        out_shape=jax.ShapeDtypeStruct((CAP, D_CHUNKS, LANES), F32),
        compiler_params=_cparams(("arbitrary",)),
        name="dispatch_scatter",
    )(counts, pad_start, padded, n_used, pos.reshape(n_tiles, DISP_T * TOP_K), h3)


def _moe_kernel(be_ref, nu_ref, x_ref, wg_ref, wu_ref, wd_ref, y_ref, wg_bf, wu_bf, wd_bf):
    i = pl.program_id(0)
    n_used = nu_ref[0]
    changed = jnp.logical_or(i == 0, be_ref[i] != be_ref[jnp.maximum(i - 1, 0)])

    @pl.when(changed)
    def _():
        wg_bf[...] = wg_ref[...].astype(BF16)
        wu_bf[...] = wu_ref[...].astype(BF16)
        wd_bf[...] = wd_ref[...].astype(BF16)

    @pl.when(i < n_used)
    def _():
        x = _rows_to_matrix(x_ref[...]).astype(BF16)
        hid = (_silu(_dot(x, wg_bf[...])) * _dot(x, wu_bf[...])).astype(BF16)
        y_ref[...] = _matrix_to_rows(_dot(hid, wd_bf[...]))

    @pl.when(i >= n_used)
    def _():
        y_ref[...] = jnp.zeros(y_ref.shape, F32)


def _moe(block_expert, n_used, xs, wg, wu, wd, layer):
    grid_spec = pltpu.PrefetchScalarGridSpec(
        num_scalar_prefetch=2,
        grid=(N_BLOCKS,),
        in_specs=[pl.BlockSpec((SLOT_BLOCK, D_CHUNKS, LANES),
                               lambda i, be, nu: (jnp.minimum(i, nu[0] - 1), 0, 0)),
                  pl.BlockSpec((None, None, D_MODEL, EXPERT_FF), lambda i, be, nu: (layer, be[i], 0, 0)),
                  pl.BlockSpec((None, None, D_MODEL, EXPERT_FF), lambda i, be, nu: (layer, be[i], 0, 0)),
                  pl.BlockSpec((None, None, EXPERT_FF, D_MODEL), lambda i, be, nu: (layer, be[i], 0, 0))],
        out_specs=pl.BlockSpec((SLOT_BLOCK, D_CHUNKS, LANES), lambda i, be, nu: (i, 0, 0)),
        scratch_shapes=[pltpu.VMEM((D_MODEL, EXPERT_FF), BF16),
                        pltpu.VMEM((D_MODEL, EXPERT_FF), BF16),
                        pltpu.VMEM((EXPERT_FF, D_MODEL), BF16)])
    return pl.pallas_call(
        _moe_kernel,
        grid_spec=grid_spec,
        out_shape=jax.ShapeDtypeStruct((CAP, D_CHUNKS, LANES), F32),
        compiler_params=_cparams(("arbitrary",)),
        name="routed_experts",
    )(block_expert, n_used, xs, wg, wu, wd)


COMB_T = 128


def _combine_kernel(pos_hbm, y_hbm, w_ref, base_ref, g_ref, b_ref, hf_ref, hb_ref,
                    pos_smem, buf, pos_sem, row_sem):
    i = pl.program_id(0)
    n = pl.num_programs(0)

    def pos_copy(blk, slot):
        return pltpu.make_async_copy(pos_hbm.at[blk], pos_smem.at[slot], pos_sem.at[slot])

    def issue_rows(slot):
        def body(t, carry):
            for kk in range(TOP_K):
                src = pos_smem[slot, t * TOP_K + kk]
                pltpu.make_async_copy(y_hbm.at[src], buf.at[slot, t, kk], row_sem.at[slot]).start()
            return carry
        lax.fori_loop(0, COMB_T, body, 0)

    def wait_rows(slot):
        for kk in range(TOP_K):
            pltpu.make_async_copy(y_hbm.at[pl.ds(0, COMB_T)], buf.at[slot, :, kk],
                                  row_sem.at[slot]).wait()

    @pl.when(i == 0)
    def _():
        pos_copy(0, 0).start()
        pos_copy(1, 1).start()
        pos_copy(0, 0).wait()
        issue_rows(0)

    nxt = i + 1
    nslot = nxt % 2

    @pl.when(nxt < n)
    def _():
        pos_copy(nxt, nslot).wait()
        issue_rows(nslot)

    @pl.when(i + 2 < n)
    def _():
        pos_copy(i + 2, i % 2).start()

    slot = i % 2
    wait_rows(slot)
    rows = buf.at[slot]
    routed = None
    for kk in range(TOP_K):
        term = w_ref[:, kk:kk + 1] * _rows_to_matrix(rows[:, kk])
        routed = term if routed is None else routed + term
    h2 = _layer_norm(base_ref[...] + routed, g_ref[...], b_ref[...])
    hf_ref[...] = h2
    hb_ref[...] = h2.astype(BF16)


def _combine(pos, y3, wts, base, g, b):
    tm = COMB_T
    n_tiles = N_TOK // tm
    full = lambda r, c: pl.BlockSpec((r, c), lambda i: (0, 0))
    tile = lambda c: pl.BlockSpec((tm, c), lambda i: (i, 0))
    return pl.pallas_call(
        _combine_kernel,
        grid=(n_tiles,),
        in_specs=[pl.BlockSpec(memory_space=pl.ANY), pl.BlockSpec(memory_space=pl.ANY),
                  tile(TOP_K), tile(D_MODEL), full(1, D_MODEL), full(1, D_MODEL)],
        out_specs=[tile(D_MODEL), tile(D_MODEL)],
        out_shape=[jax.ShapeDtypeStruct((N_TOK, D_MODEL), F32),
                   jax.ShapeDtypeStruct((N_TOK, D_MODEL), BF16)],
        scratch_shapes=[pltpu.SMEM((2, tm * TOP_K), I32),
                        pltpu.VMEM((2, tm, TOP_K, D_CHUNKS, LANES), F32),
                        pltpu.SemaphoreType.DMA((2,)),
                        pltpu.SemaphoreType.DMA((2,))],
        compiler_params=_cparams(("arbitrary",)),
        name="combine_ln2",
    )(pos.reshape(n_tiles, tm * TOP_K), y3, wts, base, g, b)


def _pack_w_in(w):
    q_lat = w[:, 0:384]
    c_kv = w[:, 384:640]
    k_pe = w[:, 640:672]
    z = w[:, 672:1696]
    xbc = w[:, 1696:3232]
    dt = w[:, 3232:3248]
    gate_a = w[:, 3248:4272]
    gate_b = w[:, 4272:5296]
    zeros = lambda n: jnp.zeros((D_MODEL, n), w.dtype)
    small = jnp.concatenate([zeros(KPE_LANE), k_pe, dt, zeros(LANES - DT_LANE - SSM_HEADS)], axis=1)
    return jnp.concatenate([gate_a, gate_b, z, xbc, q_lat, small, c_kv], axis=1).astype(BF16)


def _pack_w_q(w):
    w = w.reshape(Q_RANK, MLA_HEADS, QK_NOPE + QK_ROPE)
    w = jnp.pad(w, ((0, 0), (0, 0), (0, HEAD_PAD - QK_NOPE - QK_ROPE)))
    return w.reshape(Q_RANK, MLA_HEADS * HEAD_PAD).astype(BF16)


def _pack_w_kv(w):
    w = w.reshape(KV_RANK, MLA_HEADS, QK_NOPE + V_DIM)
    wk = jnp.pad(w[:, :, :QK_NOPE], ((0, 0), (0, 0), (0, HEAD_PAD - QK_NOPE)))
    wv = w[:, :, QK_NOPE:]
    return (wk.reshape(KV_RANK, MLA_HEADS * HEAD_PAD).astype(BF16),
            wv.reshape(KV_RANK, MLA_HEADS * V_DIM).astype(BF16))


def _rope_tables(positions):
    half = QK_ROPE // 2
    inv_freq = jnp.power(ROPE_THETA, -jnp.arange(half, dtype=F32) * (2.0 / QK_ROPE))
    ang = positions.astype(F32).reshape(N_TOK, 1) * inv_freq
    cos, sin = jnp.cos(ang), jnp.sin(ang)
    z = lambda n: jnp.zeros((N_TOK, n), F32)
    tail = HEAD_PAD - QK_NOPE - QK_ROPE
    tc = jnp.concatenate([jnp.ones((N_TOK, QK_NOPE), F32), cos, cos, z(tail)], axis=1)
    ts1 = jnp.concatenate([z(QK_NOPE), -sin, z(half), z(tail)], axis=1)
    ts2 = jnp.concatenate([z(QK_NOPE), z(half), sin, z(tail)], axis=1)
    return tc, ts1, ts2


def _head_lane_row(v):
    return jnp.zeros((1, LANES), F32).at[0, DT_LANE:DT_LANE + SSM_HEADS].set(v.astype(F32))


def _expand_matrix():
    r = jnp.arange(LANES)[:, None]
    c = jnp.arange(SSM_INNER)[None, :]
    return ((r - DT_LANE) == (c // SSM_HEADDIM)).astype(F32)


def kernel(x, p, positions, w_in, q_norm, w_q_up, kv_norm, w_kv_up, conv_w, conv_b, dt_bias, a_log, d_skip,
           ssm_norm, w_attn_br, w_ssm_br, w_o, ln1_g, ln1_b, w_router, router_bias, w_exp_gate, w_exp_up,
           w_exp_down, w_sh_gate, w_sh_up, w_sh_down, w_ple, w_ple_gate, ln2_g, ln2_b):
    tc, ts1, ts2 = _rope_tables(positions)
    e_mat = _expand_matrix()
    hf = x.reshape(N_TOK, D_MODEL)
    hb = hf.astype(BF16)
    row = lambda v: v.reshape(1, -1).astype(F32)
    for i in range(DEPTH):
        proj = _in_proj(hb, _pack_w_in(w_in[i]))
        q = _q_up(proj, row(q_norm[i]), _pack_w_q(w_q_up[i]), tc, ts1, ts2)
        wk, wv = _pack_w_kv(w_kv_up[i])
        k, v = _kv_up(proj, row(kv_norm[i]), wk, wv, tc, ts1, ts2)
        attn = _attention(q, k, v).reshape(N_TOK, MLA_HEADS * V_DIM)
        ssm_y = _ssd(proj, conv_w[i], row(conv_b[i]), _head_lane_row(dt_bias[i]), _head_lane_row(a_log[i]),
                     row(jnp.repeat(d_skip[i], SSM_HEADDIM)), row(ssm_norm[i]), e_mat)
        hf, hb = _merge(attn, ssm_y, proj, hf, w_attn_br[i].astype(BF16), w_ssm_br[i].astype(BF16),
                        w_o[i].astype(BF16), row(ln1_g[i]), row(ln1_b[i]))
        base, idx, wts, cnt = _post(hf, hb, p[i].reshape(N_TOK, PLE_DIM), w_router[i], row(router_bias[i]),
                                    w_sh_gate[i].astype(BF16), w_sh_up[i].astype(BF16),
                                    w_sh_down[i].astype(BF16), w_ple[i].astype(BF16),
                                    w_ple_gate[i].astype(BF16))
        base_slots, counts, pad_start, padded, block_expert, n_used = _slot_layout(cnt)
        pos = _slot_positions(idx, base_slots)
        xs = _dispatch(counts, pad_start, padded, n_used, pos, hf.reshape(N_TOK, D_CHUNKS, LANES))
        y3 = _moe(block_expert, n_used, xs, w_exp_gate, w_exp_up, w_exp_down, i)
        hf, hb = _combine(pos, y3, wts, base, row(ln2_g[i]), row(ln2_b[i]))
    return hf.reshape(BATCH, SEQ, D_MODEL)
```

```python
import functools
import math

import jax
import jax.numpy as jnp
from jax import lax
from jax.experimental import pallas as pl
from jax.experimental.pallas import tpu as pltpu

F32 = jnp.float32
BF16 = jnp.bfloat16
I32 = jnp.int32
HIGHEST = lax.Precision.HIGHEST

D_MODEL = 1024
BATCH = 4
SEQ = 8192
DEPTH = 2
N_TOK = BATCH * SEQ
MLA_HEADS = 8
QK_NOPE = 64
QK_ROPE = 32
V_DIM = 64
Q_RANK = 384
KV_RANK = 256
ROPE_THETA = 10000.0
SSM_INNER = 1024
SSM_HEADDIM = 64
SSM_HEADS = 16
SSM_GROUPS = 2
SSM_STATE = 128
SSM_CONV = 4
SSM_CHUNK = 128
SSM_CONV_DIM = 1536
N_EXPERTS = 256
TOP_K = 8
N_GROUPS = 8
TOPK_GROUPS = 4
PER_GROUP = N_EXPERTS // N_GROUPS
EXPERT_FF = 256
ROUTED_SCALE = 2.5
PLE_DIM = 256
ALPHA = (2 * DEPTH) ** 0.25
LN_EPS = 1e-5
RMS_EPS = 1e-6

LANES = 128
SUBLANES = 8
VMEM_LIMIT = 48 * 1024 * 1024

COL_GATE_A = 0
COL_GATE_B = 1024
COL_Z = 2048
COL_XBC = 3072
COL_QLAT = 4608
COL_SMALL = 4992
COL_CKV = 5120
PACK_COLS = 5376
KPE_LANE = 64
DT_LANE = 96

HEAD_PAD = 128
N_PAIRS = MLA_HEADS // 2
V_SLAB = 2 * LANES

ROUTE_T = 256
N_RTILES = N_TOK // ROUTE_T
SLOT_BLOCK = 128
N_ASSIGN = N_TOK * TOP_K
N_BLOCKS = N_ASSIGN // SLOT_BLOCK + N_EXPERTS
CAP = N_BLOCKS * SLOT_BLOCK
D_CHUNKS = D_MODEL // LANES

NEG = float(jnp.finfo(jnp.float32).min)


def _cparams(sem):
    return pltpu.CompilerParams(dimension_semantics=sem, vmem_limit_bytes=VMEM_LIMIT)


def _sigmoid(x):
    return 1.0 / (1.0 + jnp.exp(-x))


def _silu(x):
    return x * _sigmoid(x)


def _layer_norm(x, g, b):
    mu = jnp.mean(x, axis=-1, keepdims=True)
    xc = x - mu
    var = jnp.mean(xc * xc, axis=-1, keepdims=True)
    return xc * lax.rsqrt(var + LN_EPS) * g + b


def _rms_norm(x, g):
    return x * lax.rsqrt(jnp.mean(x * x, axis=-1, keepdims=True) + RMS_EPS) * g


def _dot(a, b):
    return jnp.dot(a, b, preferred_element_type=F32)


def _swap_sublane_major(x):
    groups = x.shape[0]
    for d in (4, 2, 1):
        y = x.reshape(groups, SUBLANES // (2 * d), 2, d, SUBLANES, LANES)
        lo, hi = y[:, :, 0], y[:, :, 1]
        shp = lo.shape
        keep = (lax.broadcasted_iota(I32, shp, 3) & d) == 0
        rot = lambda v, s: pltpu.roll(v.reshape(-1, SUBLANES, LANES), s, 1).reshape(shp)
        new_lo = jnp.where(keep, lo, rot(hi, d))
        new_hi = jnp.where(keep, rot(lo, SUBLANES - d), hi)
        x = jnp.stack([new_lo, new_hi], axis=2).reshape(groups, SUBLANES, SUBLANES, LANES)
    return x


def _rows_to_matrix(x3):
    rows = x3.shape[0]
    t = _swap_sublane_major(x3.reshape(rows // SUBLANES, SUBLANES, D_CHUNKS, LANES))
    return jnp.concatenate([t[:, cc].reshape(rows, LANES) for cc in range(D_CHUNKS)], axis=1)


def _matrix_to_rows(y):
    rows = y.shape[0]
    t = jnp.stack([y[:, cc * LANES:(cc + 1) * LANES].reshape(rows // SUBLANES, SUBLANES, LANES)
                   for cc in range(D_CHUNKS)], axis=1)
    return _swap_sublane_major(t).reshape(rows, D_CHUNKS, LANES)


def _mm_kernel(x_ref, w_ref, o_ref):
    o_ref[...] = _dot(x_ref[...], w_ref[...]).astype(o_ref.dtype)


def _in_proj(hb, w_pack):
    tm, tn = 1024, 1792
    return pl.pallas_call(
        _mm_kernel,
        grid=(PACK_COLS // tn, N_TOK // tm),
        in_specs=[pl.BlockSpec((tm, D_MODEL), lambda j, i: (i, 0)),
                  pl.BlockSpec((D_MODEL, tn), lambda j, i: (0, j))],
        out_specs=pl.BlockSpec((tm, tn), lambda j, i: (i, j)),
        out_shape=jax.ShapeDtypeStruct((N_TOK, PACK_COLS), F32),
        compiler_params=_cparams(("parallel", "parallel")),
        name="in_proj",
    )(hb, w_pack)


def _rope128(x, c, s1, s2):
    return x * c + pltpu.roll(x, LANES - 16, 1) * s1 + pltpu.roll(x, 16, 1) * s2


def _q_up_kernel(ql_ref, g_ref, w_ref, c_ref, s1_ref, s2_ref, q_ref):
    y = _rms_norm(ql_ref[...], g_ref[...]).astype(BF16)
    q = _dot(y, w_ref[...])
    c, s1, s2 = c_ref[...], s1_ref[...], s2_ref[...]
    scale = (QK_NOPE + QK_ROPE) ** -0.5 * math.log2(math.e)
    for h in range(MLA_HEADS):
        qh = _rope128(q[:, h * HEAD_PAD:(h + 1) * HEAD_PAD], c, s1, s2)
        q_ref[h] = (qh * scale).astype(BF16)


def _q_up(proj, g, wq, tc, ts1, ts2):
    tm = 512
    spt = SEQ // tm
    tab = pl.BlockSpec((tm, LANES), lambda i: (i, 0))
    return pl.pallas_call(
        _q_up_kernel,
        grid=(N_TOK // tm,),
        in_specs=[pl.BlockSpec((tm, Q_RANK), lambda i: (i, COL_QLAT // Q_RANK)),
                  pl.BlockSpec((1, Q_RANK), lambda i: (0, 0)),
                  pl.BlockSpec((Q_RANK, MLA_HEADS * HEAD_PAD), lambda i: (0, 0)),
                  tab, tab, tab],
        out_specs=pl.BlockSpec((None, MLA_HEADS, tm, HEAD_PAD), lambda i: (i // spt, 0, i % spt, 0)),
        out_shape=jax.ShapeDtypeStruct((BATCH, MLA_HEADS, SEQ, HEAD_PAD), BF16),
        compiler_params=_cparams(("parallel",)),
        name="q_up",
    )(proj, g, wq, tc, ts1, ts2)


def _kv_up_kernel(ckv_ref, small_ref, g_ref, wk_ref, wv_ref, c_ref, s1_ref, s2_ref, k_ref, v_ref):
    y = _rms_norm(ckv_ref[...], g_ref[...]).astype(BF16)
    k_all = _dot(y, wk_ref[...])
    v_all = _dot(y, wv_ref[...])
    lane = lax.broadcasted_iota(I32, small_ref.shape, 1)
    kpe_raw = jnp.where((lane >= KPE_LANE) & (lane < KPE_LANE + QK_ROPE), small_ref[...], 0.0)
    kpe = _rope128(kpe_raw, c_ref[...], s1_ref[...], s2_ref[...])
    for h in range(MLA_HEADS):
        k_ref[h] = (k_all[:, h * HEAD_PAD:(h + 1) * HEAD_PAD] + kpe).astype(BF16)
    ones = jnp.ones((v_all.shape[0], LANES), F32)
    for j in range(N_PAIRS):
        v_ref[j] = jnp.concatenate([v_all[:, j * LANES:(j + 1) * LANES], ones], axis=1).astype(BF16)


def _kv_up(proj, g, wk, wv, tc, ts1, ts2):
    tm = 512
    spt = SEQ // tm
    tab = pl.BlockSpec((tm, LANES), lambda i: (i, 0))
    return pl.pallas_call(
        _kv_up_kernel,
        grid=(N_TOK // tm,),
        in_specs=[pl.BlockSpec((tm, KV_RANK), lambda i: (i, COL_CKV // KV_RANK)),
                  pl.BlockSpec((tm, LANES), lambda i: (i, COL_SMALL // LANES)),
                  pl.BlockSpec((1, KV_RANK), lambda i: (0, 0)),
                  pl.BlockSpec((KV_RANK, MLA_HEADS * HEAD_PAD), lambda i: (0, 0)),
                  pl.BlockSpec((KV_RANK, MLA_HEADS * V_DIM), lambda i: (0, 0)),
                  tab, tab, tab],
        out_specs=[pl.BlockSpec((None, MLA_HEADS, tm, HEAD_PAD), lambda i: (i // spt, 0, i % spt, 0)),
                   pl.BlockSpec((None, N_PAIRS, tm, V_SLAB), lambda i: (i // spt, 0, i % spt, 0))],
        out_shape=[jax.ShapeDtypeStruct((BATCH, MLA_HEADS, SEQ, HEAD_PAD), BF16),
                   jax.ShapeDtypeStruct((BATCH, N_PAIRS, SEQ, V_SLAB), BF16)],
        compiler_params=_cparams(("parallel",)),
        name="kv_up",
    )(proj, proj, g, wk, wv, tc, ts1, ts2)


ATT_T = 512


def _attn_kernel(q_ref, k_ref, v_ref, o_ref, m_ref, acc_ref):
    qi = pl.program_id(2)
    t = ATT_T
    m_ref[...] = jnp.full(m_ref.shape, -jnp.inf, F32)
    acc_ref[...] = jnp.zeros(acc_ref.shape, F32)

    def kv_tile(kt, masked):
        ks = pl.multiple_of(kt * t, t)
        v = v_ref[pl.ds(ks, t), :]
        for hh in range(2):
            s = lax.dot_general(q_ref[hh], k_ref[hh, pl.ds(ks, t), :], (((1,), (1,)), ((), ())),
                                preferred_element_type=F32)
            if masked:
                row = lax.broadcasted_iota(I32, s.shape, 0)
                col = lax.broadcasted_iota(I32, s.shape, 1)
                s = jnp.where(col <= row, s, NEG)
            m_prev = m_ref[hh]
            m_new = jnp.maximum(m_prev, jnp.max(s, axis=-1, keepdims=True))
            alpha = jnp.exp2(m_prev - m_new)
            p = jnp.exp2(s - jnp.concatenate([m_new] * (t // LANES), axis=1))
            acc_ref[hh] = jnp.concatenate([alpha, alpha], axis=1) * acc_ref[hh] + _dot(p.astype(BF16), v)
            m_ref[hh] = m_new

    def body(kt, carry):
        kv_tile(kt, False)
        return carry

    lax.fori_loop(0, qi, body, 0)
    kv_tile(qi, True)

    lane = lax.broadcasted_iota(I32, o_ref.shape, 1)
    a0 = acc_ref[0]
    a1 = acc_ref[1]
    o0 = a0[:, :LANES] / a0[:, LANES:]
    o1 = a1[:, :LANES] / a1[:, LANES:]
    o_ref[...] = jnp.where(lane < V_DIM, o0, o1).astype(o_ref.dtype)


def _attention(q, k, v):
    t = ATT_T
    return pl.pallas_call(
        _attn_kernel,
        grid=(BATCH, N_PAIRS, SEQ // t),
        in_specs=[pl.BlockSpec((None, 2, t, HEAD_PAD), lambda b, j, qi: (b, j, qi, 0)),
                  pl.BlockSpec((None, 2, SEQ, HEAD_PAD), lambda b, j, qi: (b, j, 0, 0)),
                  pl.BlockSpec((None, None, SEQ, V_SLAB), lambda b, j, qi: (b, j, 0, 0))],
        out_specs=pl.BlockSpec((None, t, LANES), lambda b, j, qi: (b, qi, j)),
        out_shape=jax.ShapeDtypeStruct((BATCH, SEQ, MLA_HEADS * V_DIM), BF16),
        scratch_shapes=[pltpu.VMEM((2, t, LANES), F32), pltpu.VMEM((2, t, V_SLAB), F32)],
        compiler_params=_cparams(("parallel", "parallel", "parallel")),
        name="mla_attention",
    )(q, k, v)


HALO = SUBLANES
HEADS_PER_GROUP = SSM_HEADS // SSM_GROUPS
GROUP_W = SSM_INNER // SSM_GROUPS


def _ssd_kernel(xbc_ref, halo_ref, z_ref, small_ref, cw_ref, cb_ref, dtb_ref, alog_ref, dexp_ref,
                nw_ref, e_ref, y_ref, st_ref, cat_ref):
    c = pl.program_id(1)
    L = SSM_CHUNK

    @pl.when(c == 0)
    def _():
        st_ref[...] = jnp.zeros(st_ref.shape, F32)

    cat_ref[0:HALO, :] = jnp.where(c == 0, 0.0, halo_ref[...])
    cat_ref[HALO:HALO + L, :] = xbc_ref[...]
    acc = jnp.broadcast_to(cb_ref[...], (L, SSM_CONV_DIM))
    for kk in range(SSM_CONV):
        off = HALO - (SSM_CONV - 1) + kk
        acc = acc + cw_ref[kk:kk + 1, :] * cat_ref[off:off + L, :]
    xc = _silu(acc)
    xs = xc[:, :SSM_INNER]
    bm = xc[:, SSM_INNER:SSM_INNER + SSM_GROUPS * SSM_STATE]
    cm = xc[:, SSM_INNER + SSM_GROUPS * SSM_STATE:]

    lane = lax.broadcasted_iota(I32, (L, LANES), 1)
    dt_lanes = (lane >= DT_LANE) & (lane < DT_LANE + SSM_HEADS)
    dt_in = small_ref[...] + dtb_ref[...]
    dt = jnp.maximum(dt_in, 0.0) + jnp.log1p(jnp.exp(-jnp.abs(dt_in)))
    dt = jnp.where(dt_lanes, dt, 0.0)
    a = -jnp.exp(alog_ref[...])
    a_dt = dt * a
    row = lax.broadcasted_iota(I32, (L, L), 0)
    col = lax.broadcasted_iota(I32, (L, L), 1)
    tri = row >= col
    a_cum = jnp.dot(tri.astype(F32), a_dt, precision=HIGHEST, preferred_element_type=F32)
    a_last = a_cum[L - 1:L, :]
    e = e_ref[...]

    def expand(t):
        return jnp.dot(t, e, precision=HIGHEST, preferred_element_type=F32)

    x_dt = xs * expand(dt)
    eac_x = expand(jnp.exp(a_cum))
    ds_x = expand(jnp.exp(a_last - a_cum))
    cd_x = eac_x[L - 1:L, :]
    xd = (x_dt * ds_x).astype(BF16)
    x_bf = x_dt.astype(BF16)
    a_cum_t = a_cum.T
    lane_h = lax.broadcasted_iota(I32, (L, LANES), 1)

    y_parts = []
    for g in range(SSM_GROUPS):
        bg = bm[:, g * SSM_STATE:(g + 1) * SSM_STATE].astype(BF16)
        cg = cm[:, g * SSM_STATE:(g + 1) * SSM_STATE].astype(BF16)
        cb = lax.dot_general(cg, bg, (((1,), (1,)), ((), ())), preferred_element_type=F32)
        st_prev = st_ref[g]
        y_off = _dot(cg, st_prev.astype(BF16)) * eac_x[:, g * GROUP_W:(g + 1) * GROUP_W]
        for jp in range(HEADS_PER_GROUP // 2):
            pair = g * (HEADS_PER_GROUP // 2) + jp
            xp = x_bf[:, pair * LANES:(pair + 1) * LANES]
            outs = []
            for hh in range(2):
                hl = DT_LANE + 2 * pair + hh
                seg = a_cum[:, hl:hl + 1] - a_cum_t[hl:hl + 1, :]
                decay = jnp.where(tri, jnp.exp(seg), 0.0)
                outs.append(_dot((cb * decay).astype(BF16), xp))
            y_parts.append(jnp.where(lane_h < SSM_HEADDIM, outs[0], outs[1])
                           + y_off[:, jp * LANES:(jp + 1) * LANES])
        upd = lax.dot_general(bg, xd[:, g * GROUP_W:(g + 1) * GROUP_W], (((0,), (0,)), ((), ())),
                              preferred_element_type=F32)
        st_ref[g] = st_prev * cd_x[:, g * GROUP_W:(g + 1) * GROUP_W] + upd

    y = jnp.concatenate(y_parts, axis=1) + xs * dexp_ref[...]
    y = y * _silu(z_ref[...])
    normed = []
    for g in range(SSM_GROUPS):
        yg = y[:, g * GROUP_W:(g + 1) * GROUP_W]
        normed.append(yg * lax.rsqrt(jnp.mean(yg * yg, axis=-1, keepdims=True) + RMS_EPS))
    y_ref[...] = (jnp.concatenate(normed, axis=1) * nw_ref[...]).astype(y_ref.dtype)


def _ssd(proj, cw, cb, dtb, alog, dexp, nw, e_mat):
    L = SSM_CHUNK
    nc = SEQ // L
    xbc_blk = COL_XBC // SSM_CONV_DIM

    def row1(w):
        return pl.BlockSpec((1, w), lambda b, c: (0, 0))

    return pl.pallas_call(
        _ssd_kernel,
        grid=(BATCH, nc),
        in_specs=[pl.BlockSpec((L, SSM_CONV_DIM), lambda b, c: (b * nc + c, xbc_blk)),
                  pl.BlockSpec((HALO, SSM_CONV_DIM),
                               lambda b, c: (jnp.maximum((b * nc + c) * (L // HALO) - 1, 0), xbc_blk)),
                  pl.BlockSpec((L, SSM_INNER), lambda b, c: (b * nc + c, COL_Z // SSM_INNER)),
                  pl.BlockSpec((L, LANES), lambda b, c: (b * nc + c, COL_SMALL // LANES)),
                  pl.BlockSpec((SSM_CONV, SSM_CONV_DIM), lambda b, c: (0, 0)),
                  row1(SSM_CONV_DIM), row1(LANES), row1(LANES), row1(SSM_INNER), row1(SSM_INNER),
                  pl.BlockSpec((LANES, SSM_INNER), lambda b, c: (0, 0))],
        out_specs=pl.BlockSpec((L, SSM_INNER), lambda b, c: (b * nc + c, 0)),
        out_shape=jax.ShapeDtypeStruct((N_TOK, SSM_INNER), BF16),
        scratch_shapes=[pltpu.VMEM((SSM_GROUPS, SSM_STATE, GROUP_W), F32),
                        pltpu.VMEM((HALO + L, SSM_CONV_DIM), F32)],
        compiler_params=_cparams(("parallel", "arbitrary")),
        name="mamba2_ssd",
    )(proj, proj, proj, proj, cw, cb, dtb, alog, dexp, nw, e_mat)


def _merge_kernel(attn_ref, ssm_ref, ga_ref, gb_ref, h_ref, wa_ref, ws_ref, wo_ref, g_ref, b_ref,
                  hf_ref, hb_ref):
    ya = _dot(attn_ref[...], wa_ref[...])
    yb = _dot(ssm_ref[...], ws_ref[...])
    mix = _sigmoid(ga_ref[...]) * ya + _sigmoid(gb_ref[...]) * yb
    mixed = _dot(mix.astype(BF16), wo_ref[...])
    h1 = _layer_norm(ALPHA * h_ref[...] + mixed, g_ref[...], b_ref[...])
    hf_ref[...] = h1
    hb_ref[...] = h1.astype(BF16)


def _merge(attn, ssm_y, proj, hf, wa, ws, wo, g, b):
    tm = 256
    full = lambda r, c: pl.BlockSpec((r, c), lambda i: (0, 0))
    tile = lambda c, j=0: pl.BlockSpec((tm, c), lambda i: (i, j))
    return pl.pallas_call(
        _merge_kernel,
        grid=(N_TOK // tm,),
        in_specs=[tile(MLA_HEADS * V_DIM), tile(SSM_INNER), tile(D_MODEL, COL_GATE_A // D_MODEL),
                  tile(D_MODEL, COL_GATE_B // D_MODEL), tile(D_MODEL),
                  full(MLA_HEADS * V_DIM, D_MODEL), full(SSM_INNER, D_MODEL), full(D_MODEL, D_MODEL),
                  full(1, D_MODEL), full(1, D_MODEL)],
        out_specs=[tile(D_MODEL), tile(D_MODEL)],
        out_shape=[jax.ShapeDtypeStruct((N_TOK, D_MODEL), F32),
                   jax.ShapeDtypeStruct((N_TOK, D_MODEL), BF16)],
        compiler_params=_cparams(("parallel",)),
        name="merge_ln1",
    )(attn, ssm_y, proj, proj, hf, wa, ws, wo, g, b)


def _first_argmax(vals, lane):
    m = jnp.max(vals, axis=-1, keepdims=True)
    idx = jnp.min(jnp.where(vals == m, lane, float(N_EXPERTS)), axis=-1, keepdims=True)
    return m, idx


def _post_kernel(hf_ref, hb_ref, p_ref, wr_ref, rb_ref, wsg_ref, wsu_ref, wsd_ref, wp_ref, wpg_ref,
                 base_ref, idx_ref, wt_ref, cnt_ref):
    hf = hf_ref[...]
    hb = hb_ref[...]
    tm = hf.shape[0]
    logits = jnp.dot(hf, wr_ref[...], precision=HIGHEST, preferred_element_type=F32)
    scores = _sigmoid(logits)
    sel = scores + rb_ref[...]
    lane_i = lax.broadcasted_iota(I32, (tm, N_EXPERTS), 1)
    grp = lane_i // PER_GROUP
    lane = lane_i.astype(F32)

    grp_scores = []
    for g in range(N_GROUPS):
        vals = jnp.where(grp == g, sel, -jnp.inf)
        m1, i1 = _first_argmax(vals, lane)
        m2 = jnp.max(jnp.where(lane == i1, -jnp.inf, vals), axis=-1, keepdims=True)
        grp_scores.append(m1 + m2)
    keep = jnp.zeros((tm, N_EXPERTS), jnp.bool_)
    for g in range(N_GROUPS):
        rank = jnp.zeros((tm, 1), I32)
        for o in range(N_GROUPS):
            if o == g:
                continue
            ahead = (grp_scores[o] > grp_scores[g]) if o > g else (grp_scores[o] >= grp_scores[g])
            rank = rank + ahead.astype(I32)
        keep = keep | ((grp == g) & (rank < TOPK_GROUPS))
    masked = jnp.where(keep, sel, -jnp.inf)

    lane_k = lax.broadcasted_iota(I32, (tm, TOP_K), 1)
    idx_out = jnp.zeros((tm, TOP_K), I32)
    w_out = jnp.zeros((tm, TOP_K), F32)
    chosen = jnp.zeros((tm, N_EXPERTS), F32)
    for kk in range(TOP_K):
        _, ik = _first_argmax(masked, lane)
        hit = lane == ik
        wk = jnp.sum(jnp.where(hit, scores, 0.0), axis=-1, keepdims=True)
        masked = jnp.where(hit, -jnp.inf, masked)
        chosen = jnp.where(hit, 1.0, chosen)
        idx_out = jnp.where(lane_k == kk, ik.astype(I32), idx_out)
        w_out = jnp.where(lane_k == kk, wk, w_out)
    w_out = w_out / jnp.sum(w_out, axis=-1, keepdims=True) * ROUTED_SCALE
    idx_ref[...] = idx_out
    wt_ref[...] = w_out
    cnt_ref[...] = jnp.sum(chosen, axis=0, keepdims=True)

    shared = _dot((_silu(_dot(hb, wsg_ref[...])) * _dot(hb, wsu_ref[...])).astype(BF16), wsd_ref[...])
    ple = _dot(p_ref[...].astype(BF16), wp_ref[...]) * _sigmoid(_dot(hb, wpg_ref[...]))
    base_ref[...] = ALPHA * hf + shared + ple


def _post(hf, hb, p, wr, rb, wsg, wsu, wsd, wp, wpg):
    tm = ROUTE_T
    full = lambda r, c: pl.BlockSpec((r, c), lambda i: (0, 0))
    tile = lambda c: pl.BlockSpec((tm, c), lambda i: (i, 0))
    return pl.pallas_call(
        _post_kernel,
        grid=(N_RTILES,),
        in_specs=[tile(D_MODEL), tile(D_MODEL), tile(PLE_DIM),
                  full(D_MODEL, N_EXPERTS), full(1, N_EXPERTS),
                  full(D_MODEL, EXPERT_FF), full(D_MODEL, EXPERT_FF), full(EXPERT_FF, D_MODEL),
                  full(PLE_DIM, D_MODEL), full(D_MODEL, D_MODEL)],
        out_specs=[tile(D_MODEL), tile(TOP_K), tile(TOP_K),
                   pl.BlockSpec((None, 1, N_EXPERTS), lambda i: (i, 0, 0))],
        out_shape=[jax.ShapeDtypeStruct((N_TOK, D_MODEL), F32),
                   jax.ShapeDtypeStruct((N_TOK, TOP_K), I32),
                   jax.ShapeDtypeStruct((N_TOK, TOP_K), F32),
                   jax.ShapeDtypeStruct((N_RTILES, 1, N_EXPERTS), F32)],
        compiler_params=_cparams(("parallel",)),
        name="router_shared_ple",
    )(hf, hb, p, wr, rb, wsg, wsu, wsd, wp, wpg)


def _pos_kernel(idx_ref, base_ref, pos_ref):
    idx = idx_ref[...]
    tm = idx.shape[0]
    lane = lax.broadcasted_iota(I32, (tm, N_EXPERTS), 1)
    hits = [lane == idx[:, kk:kk + 1] for kk in range(TOP_K)]
    chosen = hits[0]
    for kk in range(1, TOP_K):
        chosen = chosen | hits[kk]
    r = lax.broadcasted_iota(I32, (tm, tm), 0)
    c = lax.broadcasted_iota(I32, (tm, tm), 1)
    earlier = jnp.where(c < r, 1.0, 0.0).astype(BF16)
    rank = _dot(earlier, jnp.where(chosen, 1.0, 0.0).astype(BF16))
    slot = rank + base_ref[...]
    lane_k = lax.broadcasted_iota(I32, (tm, TOP_K), 1)
    out = jnp.zeros((tm, TOP_K), I32)
    for kk in range(TOP_K):
        pk = jnp.sum(jnp.where(hits[kk], slot, 0.0), axis=-1, keepdims=True)
        out = jnp.where(lane_k == kk, pk.astype(I32), out)
    pos_ref[...] = out


def _slot_positions(idx, base):
    tm = ROUTE_T
    return pl.pallas_call(
        _pos_kernel,
        grid=(N_RTILES,),
        in_specs=[pl.BlockSpec((tm, TOP_K), lambda i: (i, 0)),
                  pl.BlockSpec((None, 1, N_EXPERTS), lambda i: (i, 0, 0))],
        out_specs=pl.BlockSpec((tm, TOP_K), lambda i: (i, 0)),
        out_shape=jax.ShapeDtypeStruct((N_TOK, TOP_K), I32),
        compiler_params=_cparams(("parallel",)),
        name="slot_positions",
    )(idx, base)


def _slot_layout(cnt):
    counts_te = cnt.reshape(N_RTILES, N_EXPERTS).astype(I32)
    counts = jnp.sum(counts_te, axis=0)
    padded = (counts + SLOT_BLOCK - 1) // SLOT_BLOCK * SLOT_BLOCK
    pad_end = jnp.cumsum(padded)
    pad_start = pad_end - padded
    tile_off = jnp.cumsum(counts_te, axis=0) - counts_te
    base = (pad_start[None, :] + tile_off).astype(F32).reshape(N_RTILES, 1, N_EXPERTS)
    n_used = (pad_end[-1] // SLOT_BLOCK).astype(I32).reshape(1)
    return base, counts, pad_start, padded, n_used


DISP_T = 256


def _dispatch_kernel(cnt_ref, ps_ref, pd_ref, nu_ref, pos_hbm, h3_ref, x_hbm,
                     pos_smem, zbuf, pos_sem, row_sem, pad_sem):
    i = pl.program_id(0)
    n = pl.num_programs(0)

    def pos_copy(blk, slot):
        return pltpu.make_async_copy(pos_hbm.at[blk], pos_smem.at[slot], pos_sem.at[slot])

    def pad_row(e, r):
        return pltpu.make_async_copy(zbuf.at[0], x_hbm.at[ps_ref[e] + r], pad_sem)

    def pad_block(b):
        return pltpu.make_async_copy(zbuf, x_hbm.at[pl.ds(b * SLOT_BLOCK, SLOT_BLOCK)], pad_sem)

    def for_each_pad(fn_row, fn_block):
        def per_expert(e, carry):
            def per_row(r, c2):
                fn_row(e, r)
                return c2
            return lax.fori_loop(cnt_ref[e], pd_ref[e], per_row, carry)
        lax.fori_loop(0, N_EXPERTS, per_expert, 0)

        def per_block(b, carry):
            fn_block(b)
            return carry
        lax.fori_loop(nu_ref[0], N_BLOCKS, per_block, 0)

    @pl.when(i == 0)
    def _():
        pos_copy(0, 0).start()
        pos_copy(1, 1).start()
        zbuf[...] = jnp.zeros(zbuf.shape, F32)
        for_each_pad(lambda e, r: pad_row(e, r).start(), lambda b: pad_block(b).start())
        for_each_pad(lambda e, r: pad_row(e, r).wait(), lambda b: pad_block(b).wait())

    slot = i % 2
    pos_copy(i, slot).wait()

    def body(t, carry):
        for kk in range(TOP_K):
            dst = pos_smem[slot, t * TOP_K + kk]
            pltpu.make_async_copy(h3_ref.at[t], x_hbm.at[dst], row_sem).start()
        return carry
    lax.fori_loop(0, DISP_T, body, 0)

    @pl.when(i + 2 < n)
    def _():
        pos_copy(i + 2, slot).start()

    for kk in range(TOP_K):
        pltpu.make_async_copy(h3_ref, x_hbm.at[pl.ds(0, DISP_T)], row_sem).wait()


def _dispatch(counts, pad_start, padded, n_used, pos, h3):
    n_tiles = N_TOK // DISP_T
    grid_spec = pltpu.PrefetchScalarGridSpec(
        num_scalar_prefetch=4,
        grid=(n_tiles,),
        in_specs=[pl.BlockSpec(memory_space=pl.ANY),
                  pl.BlockSpec((DISP_T, D_CHUNKS, LANES), lambda i, *_: (i, 0, 0))],
        out_specs=pl.BlockSpec(memory_space=pl.ANY),
        scratch_shapes=[pltpu.SMEM((2, DISP_T * TOP_K), I32),
                        pltpu.VMEM((SLOT_BLOCK, D_CHUNKS, LANES), F32),
                        pltpu.SemaphoreType.DMA((2,)),
                        pltpu.SemaphoreType.DMA,
                        pltpu.SemaphoreType.DMA])
    return pl.pallas_call(
        _dispatch_kernel,
        grid_spec=grid_spec,
        out_shape=jax.ShapeDtypeStruct((CAP, D_CHUNKS, LANES), F32),
        compiler_params=_cparams(("arbitrary",)),
        name="dispatch_scatter",
    )(counts, pad_start, padded, n_used, pos.reshape(n_tiles, DISP_T * TOP_K), h3)


def _moe_kernel(fb_ref, nb_ref, nu_ref, x_hbm, wg_ref, wu_ref, wd_ref, y_hbm,
                xbuf, ybuf, wg_bf, wu_bf, wd_bf, in_sem, out_sem):
    e = pl.program_id(0)
    first = fb_ref[e]
    nb = nb_ref[e]

    def rows(b):
        return pl.ds(pl.multiple_of((first + b) * SLOT_BLOCK, SLOT_BLOCK), SLOT_BLOCK)

    def x_copy(b, slot):
        return pltpu.make_async_copy(x_hbm.at[rows(b)], xbuf.at[slot], in_sem.at[slot])

    def y_copy(b, slot):
        return pltpu.make_async_copy(ybuf.at[slot], y_hbm.at[rows(b)], out_sem.at[slot])

    @pl.when(nb > 0)
    def _():
        x_copy(0, 0).start()
        wg_bf[...] = wg_ref[...].astype(BF16)
        wu_bf[...] = wu_ref[...].astype(BF16)
        wd_bf[...] = wd_ref[...].astype(BF16)

        def body(b, carry):
            slot = b % 2
            x_copy(b, slot).wait()

            @pl.when(b + 1 < nb)
            def _():
                x_copy(b + 1, 1 - slot).start()

            x = _rows_to_matrix(xbuf[slot]).astype(BF16)
            hid = (_silu(_dot(x, wg_bf[...])) * _dot(x, wu_bf[...])).astype(BF16)
            y3 = _matrix_to_rows(_dot(hid, wd_bf[...]))

            @pl.when(b >= 2)
            def _():
                y_copy(b - 2, slot).wait()

            ybuf[slot] = y3
            y_copy(b, slot).start()
            return carry

        lax.fori_loop(0, nb, body, 0)

        @pl.when(nb >= 2)
        def _():
            y_copy(nb - 2, nb % 2).wait()

        y_copy(nb - 1, (nb - 1) % 2).wait()

    @pl.when(e == N_EXPERTS - 1)
    def _():
        ybuf[0] = jnp.zeros(ybuf.shape[1:], F32)

        def tail_copy(b):
            return pltpu.make_async_copy(
                ybuf.at[0], y_hbm.at[pl.ds(pl.multiple_of(b * SLOT_BLOCK, SLOT_BLOCK), SLOT_BLOCK)],
                out_sem.at[0])

        def start(b, carry):
            tail_copy(b).start()
            return carry

        def wait(b, carry):
            tail_copy(b).wait()
            return carry

        lax.fori_loop(nu_ref[0], N_BLOCKS, start, 0)
        lax.fori_loop(nu_ref[0], N_BLOCKS, wait, 0)


def _moe(first_block, n_blocks, n_used, xs, wg, wu, wd, layer):
    wspec = lambda r, c: pl.BlockSpec((None, None, r, c), lambda e, *_: (layer, e, 0, 0))
    grid_spec = pltpu.PrefetchScalarGridSpec(
        num_scalar_prefetch=3,
        grid=(N_EXPERTS,),
        in_specs=[pl.BlockSpec(memory_space=pl.ANY),
                  wspec(D_MODEL, EXPERT_FF), wspec(D_MODEL, EXPERT_FF), wspec(EXPERT_FF, D_MODEL)],
        out_specs=pl.BlockSpec(memory_space=pl.ANY),
        scratch_shapes=[pltpu.VMEM((2, SLOT_BLOCK, D_CHUNKS, LANES), F32),
                        pltpu.VMEM((2, SLOT_BLOCK, D_CHUNKS, LANES), F32),
                        pltpu.VMEM((D_MODEL, EXPERT_FF), BF16),
                        pltpu.VMEM((D_MODEL, EXPERT_FF), BF16),
                        pltpu.VMEM((EXPERT_FF, D_MODEL), BF16),
                        pltpu.SemaphoreType.DMA((2,)),
                        pltpu.SemaphoreType.DMA((2,))])
    return pl.pallas_call(
        _moe_kernel,
        grid_spec=grid_spec,
        out_shape=jax.ShapeDtypeStruct((CAP, D_CHUNKS, LANES), F32),
        compiler_params=_cparams(("arbitrary",)),
        name="routed_experts",
    )(first_block, n_blocks, n_used, xs, wg, wu, wd)


COMB_T = 128


def _combine_kernel(pos_hbm, y_hbm, w_ref, base_ref, g_ref, b_ref, hf_ref, hb_ref,
                    pos_smem, buf, pos_sem, row_sem):
    i = pl.program_id(0)
    n = pl.num_programs(0)

    def pos_copy(blk, slot):
        return pltpu.make_async_copy(pos_hbm.at[blk], pos_smem.at[slot], pos_sem.at[slot])

    def issue_rows(slot):
        def body(t, carry):
            for kk in range(TOP_K):
                src = pos_smem[slot, t * TOP_K + kk]
                pltpu.make_async_copy(y_hbm.at[src], buf.at[slot, t, kk], row_sem.at[slot]).start()
            return carry
        lax.fori_loop(0, COMB_T, body, 0)

    def wait_rows(slot):
        for kk in range(TOP_K):
            pltpu.make_async_copy(y_hbm.at[pl.ds(0, COMB_T)], buf.at[slot, :, kk],
                                  row_sem.at[slot]).wait()

    @pl.when(i == 0)
    def _():
        pos_copy(0, 0).start()
        pos_copy(1, 1).start()
        pos_copy(0, 0).wait()
        issue_rows(0)

    nxt = i + 1
    nslot = nxt % 2

    @pl.when(nxt < n)
    def _():
        pos_copy(nxt, nslot).wait()
        issue_rows(nslot)

    @pl.when(i + 2 < n)
    def _():
        pos_copy(i + 2, i % 2).start()

    slot = i % 2
    wait_rows(slot)
    rows = buf.at[slot]
    routed = None
    for kk in range(TOP_K):
        term = w_ref[:, kk:kk + 1] * _rows_to_matrix(rows[:, kk])
        routed = term if routed is None else routed + term
    h2 = _layer_norm(base_ref[...] + routed, g_ref[...], b_ref[...])
    hf_ref[...] = h2
    hb_ref[...] = h2.astype(BF16)


def _combine(pos, y3, wts, base, g, b):
    tm = COMB_T
    n_tiles = N_TOK // tm
    full = lambda r, c: pl.BlockSpec((r, c), lambda i: (0, 0))
    tile = lambda c: pl.BlockSpec((tm, c), lambda i: (i, 0))
    return pl.pallas_call(
        _combine_kernel,
        grid=(n_tiles,),
        in_specs=[pl.BlockSpec(memory_space=pl.ANY), pl.BlockSpec(memory_space=pl.ANY),
                  tile(TOP_K), tile(D_MODEL), full(1, D_MODEL), full(1, D_MODEL)],
        out_specs=[tile(D_MODEL), tile(D_MODEL)],
        out_shape=[jax.ShapeDtypeStruct((N_TOK, D_MODEL), F32),
                   jax.ShapeDtypeStruct((N_TOK, D_MODEL), BF16)],
        scratch_shapes=[pltpu.SMEM((2, tm * TOP_K), I32),
                        pltpu.VMEM((2, tm, TOP_K, D_CHUNKS, LANES), F32),
                        pltpu.SemaphoreType.DMA((2,)),
                        pltpu.SemaphoreType.DMA((2,))],
        compiler_params=_cparams(("arbitrary",)),
        name="combine_ln2",
    )(pos.reshape(n_tiles, tm * TOP_K), y3, wts, base, g, b)


def _pack_w_in(w):
    q_lat = w[:, 0:384]
    c_kv = w[:, 384:640]
    k_pe = w[:, 640:672]
    z = w[:, 672:1696]
    xbc = w[:, 1696:3232]
    dt = w[:, 3232:3248]
    gate_a = w[:, 3248:4272]
    gate_b = w[:, 4272:5296]
    zeros = lambda n: jnp.zeros((D_MODEL, n), w.dtype)
    small = jnp.concatenate([zeros(KPE_LANE), k_pe, dt, zeros(LANES - DT_LANE - SSM_HEADS)], axis=1)
    return jnp.concatenate([gate_a, gate_b, z, xbc, q_lat, small, c_kv], axis=1).astype(BF16)


def _pack_w_q(w):
    w = w.reshape(Q_RANK, MLA_HEADS, QK_NOPE + QK_ROPE)
    w = jnp.pad(w, ((0, 0), (0, 0), (0, HEAD_PAD - QK_NOPE - QK_ROPE)))
    return w.reshape(Q_RANK, MLA_HEADS * HEAD_PAD).astype(BF16)


def _pack_w_kv(w):
    w = w.reshape(KV_RANK, MLA_HEADS, QK_NOPE + V_DIM)
    wk = jnp.pad(w[:, :, :QK_NOPE], ((0, 0), (0, 0), (0, HEAD_PAD - QK_NOPE)))
    wv = w[:, :, QK_NOPE:]
    return (wk.reshape(KV_RANK, MLA_HEADS * HEAD_PAD).astype(BF16),
            wv.reshape(KV_RANK, MLA_HEADS * V_DIM).astype(BF16))


def _rope_tables(positions):
    half = QK_ROPE // 2
    inv_freq = jnp.power(ROPE_THETA, -jnp.arange(half, dtype=F32) * (2.0 / QK_ROPE))
    ang = positions.astype(F32).reshape(N_TOK, 1) * inv_freq
    cos, sin = jnp.cos(ang), jnp.sin(ang)
    z = lambda n: jnp.zeros((N_TOK, n), F32)
    tail = HEAD_PAD - QK_NOPE - QK_ROPE
    tc = jnp.concatenate([jnp.ones((N_TOK, QK_NOPE), F32), cos, cos, z(tail)], axis=1)
    ts1 = jnp.concatenate([z(QK_NOPE), -sin, z(half), z(tail)], axis=1)
    ts2 = jnp.concatenate([z(QK_NOPE), z(half), sin, z(tail)], axis=1)
    return tc, ts1, ts2


def _head_lane_row(v):
    return jnp.zeros((1, LANES), F32).at[0, DT_LANE:DT_LANE + SSM_HEADS].set(v.astype(F32))


def _expand_matrix():
    r = jnp.arange(LANES)[:, None]
    c = jnp.arange(SSM_INNER)[None, :]
    return ((r - DT_LANE) == (c // SSM_HEADDIM)).astype(F32)


def kernel(x, p, positions, w_in, q_norm, w_q_up, kv_norm, w_kv_up, conv_w, conv_b, dt_bias, a_log, d_skip,
           ssm_norm, w_attn_br, w_ssm_br, w_o, ln1_g, ln1_b, w_router, router_bias, w_exp_gate, w_exp_up,
           w_exp_down, w_sh_gate, w_sh_up, w_sh_down, w_ple, w_ple_gate, ln2_g, ln2_b):
    tc, ts1, ts2 = _rope_tables(positions)
    e_mat = _expand_matrix()
    hf = x.reshape(N_TOK, D_MODEL)
    hb = hf.astype(BF16)
    row = lambda v: v.reshape(1, -1).astype(F32)
    for i in range(DEPTH):
        proj = _in_proj(hb, _pack_w_in(w_in[i]))
        q = _q_up(proj, row(q_norm[i]), _pack_w_q(w_q_up[i]), tc, ts1, ts2)
        wk, wv = _pack_w_kv(w_kv_up[i])
        k, v = _kv_up(proj, row(kv_norm[i]), wk, wv, tc, ts1, ts2)
        attn = _attention(q, k, v).reshape(N_TOK, MLA_HEADS * V_DIM)
        ssm_y = _ssd(proj, conv_w[i], row(conv_b[i]), _head_lane_row(dt_bias[i]), _head_lane_row(a_log[i]),
                     row(jnp.repeat(d_skip[i], SSM_HEADDIM)), row(ssm_norm[i]), e_mat)
        hf, hb = _merge(attn, ssm_y, proj, hf, w_attn_br[i].astype(BF16), w_ssm_br[i].astype(BF16),
                        w_o[i].astype(BF16), row(ln1_g[i]), row(ln1_b[i]))
        base, idx, wts, cnt = _post(hf, hb, p[i].reshape(N_TOK, PLE_DIM), w_router[i], row(router_bias[i]),
                                    w_sh_gate[i].astype(BF16), w_sh_up[i].astype(BF16),
                                    w_sh_down[i].astype(BF16), w_ple[i].astype(BF16),
                                    w_ple_gate[i].astype(BF16))
        base_slots, counts, pad_start, padded, n_used = _slot_layout(cnt)
        pos = _slot_positions(idx, base_slots)
        xs = _dispatch(counts, pad_start, padded, n_used, pos, hf.reshape(N_TOK, D_CHUNKS, LANES))
        y3 = _moe(pad_start // SLOT_BLOCK, padded // SLOT_BLOCK, n_used, xs,
                  w_exp_gate, w_exp_up, w_exp_down, i)
        hf, hb = _combine(pos, y3, wts, base, row(ln2_g[i]), row(ln2_b[i]))
    return hf.reshape(BATCH, SEQ, D_MODEL)
```

```python
import functools
import math

import jax
import jax.numpy as jnp
from jax import lax
from jax.experimental import pallas as pl
from jax.experimental.pallas import tpu as pltpu

F32 = jnp.float32
BF16 = jnp.bfloat16
I32 = jnp.int32
HIGHEST = lax.Precision.HIGHEST

D_MODEL = 1024
BATCH = 4
SEQ = 8192
DEPTH = 2
N_TOK = BATCH * SEQ
MLA_HEADS = 8
QK_NOPE = 64
QK_ROPE = 32
V_DIM = 64
Q_RANK = 384
KV_RANK = 256
ROPE_THETA = 10000.0
SSM_INNER = 1024
SSM_HEADDIM = 64
SSM_HEADS = 16
SSM_GROUPS = 2
SSM_STATE = 128
SSM_CONV = 4
SSM_CHUNK = 128
SSM_CONV_DIM = 1536
N_EXPERTS = 256
TOP_K = 8
N_GROUPS = 8
TOPK_GROUPS = 4
PER_GROUP = N_EXPERTS // N_GROUPS
EXPERT_FF = 256
ROUTED_SCALE = 2.5
PLE_DIM = 256
ALPHA = (2 * DEPTH) ** 0.25
LN_EPS = 1e-5
RMS_EPS = 1e-6

LANES = 128
SUBLANES = 8
VMEM_LIMIT = 48 * 1024 * 1024

COL_GATE_A = 0
COL_GATE_B = 1024
COL_Z = 2048
COL_XBC = 3072
COL_QLAT = 4608
COL_SMALL = 4992
COL_CKV = 5120
PACK_COLS = 5376
KPE_LANE = 64
DT_LANE = 96

HEAD_PAD = 128
N_PAIRS = MLA_HEADS // 2
V_SLAB = 2 * LANES

ROUTE_T = 256
N_RTILES = N_TOK // ROUTE_T
SLOT_BLOCK = 128
N_ASSIGN = N_TOK * TOP_K
N_BLOCKS = N_ASSIGN // SLOT_BLOCK + N_EXPERTS
CAP = N_BLOCKS * SLOT_BLOCK
D_CHUNKS = D_MODEL // LANES
MOE_RING = 4

NEG = float(jnp.finfo(jnp.float32).min)


def _cparams(sem):
    return pltpu.CompilerParams(dimension_semantics=sem, vmem_limit_bytes=VMEM_LIMIT)


def _sigmoid(x):
    return 1.0 / (1.0 + jnp.exp(-x))


def _silu(x):
    return x * _sigmoid(x)


def _layer_norm(x, g, b):
    mu = jnp.mean(x, axis=-1, keepdims=True)
    xc = x - mu
    var = jnp.mean(xc * xc, axis=-1, keepdims=True)
    return xc * lax.rsqrt(var + LN_EPS) * g + b


def _rms_norm(x, g):
    return x * lax.rsqrt(jnp.mean(x * x, axis=-1, keepdims=True) + RMS_EPS) * g


def _dot(a, b):
    return jnp.dot(a, b, preferred_element_type=F32)


def _swap_sublane_major(x):
    groups = x.shape[0]
    for d in (4, 2, 1):
        y = x.reshape(groups, SUBLANES // (2 * d), 2, d, SUBLANES, LANES)
        lo, hi = y[:, :, 0], y[:, :, 1]
        shp = lo.shape
        keep = (lax.broadcasted_iota(I32, shp, 3) & d) == 0
        rot = lambda v, s: pltpu.roll(v.reshape(-1, SUBLANES, LANES), s, 1).reshape(shp)
        new_lo = jnp.where(keep, lo, rot(hi, d))
        new_hi = jnp.where(keep, rot(lo, SUBLANES - d), hi)
        x = jnp.stack([new_lo, new_hi], axis=2).reshape(groups, SUBLANES, SUBLANES, LANES)
    return x


def _rows_to_matrix(x3):
    rows = x3.shape[0]
    t = _swap_sublane_major(x3.reshape(rows // SUBLANES, SUBLANES, D_CHUNKS, LANES))
    return jnp.concatenate([t[:, cc].reshape(rows, LANES) for cc in range(D_CHUNKS)], axis=1)


def _matrix_to_rows(y):
    rows = y.shape[0]
    t = jnp.stack([y[:, cc * LANES:(cc + 1) * LANES].reshape(rows // SUBLANES, SUBLANES, LANES)
                   for cc in range(D_CHUNKS)], axis=1)
    return _swap_sublane_major(t).reshape(rows, D_CHUNKS, LANES)


def _mm_kernel(x_ref, w_ref, o_ref):
    o_ref[...] = _dot(x_ref[...], w_ref[...]).astype(o_ref.dtype)


def _in_proj(hb, w_pack):
    tm, tn = 1024, 1792
    return pl.pallas_call(
        _mm_kernel,
        grid=(PACK_COLS // tn, N_TOK // tm),
        in_specs=[pl.BlockSpec((tm, D_MODEL), lambda j, i: (i, 0)),
                  pl.BlockSpec((D_MODEL, tn), lambda j, i: (0, j))],
        out_specs=pl.BlockSpec((tm, tn), lambda j, i: (i, j)),
        out_shape=jax.ShapeDtypeStruct((N_TOK, PACK_COLS), F32),
        compiler_params=_cparams(("parallel", "parallel")),
        name="in_proj",
    )(hb, w_pack)


def _rope128(x, c, s1, s2):
    return x * c + pltpu.roll(x, LANES - 16, 1) * s1 + pltpu.roll(x, 16, 1) * s2


def _q_up_kernel(ql_ref, g_ref, w_ref, c_ref, s1_ref, s2_ref, q_ref):
    y = _rms_norm(ql_ref[...], g_ref[...]).astype(BF16)
    q = _dot(y, w_ref[...])
    c, s1, s2 = c_ref[...], s1_ref[...], s2_ref[...]
    scale = (QK_NOPE + QK_ROPE) ** -0.5 * math.log2(math.e)
    for h in range(MLA_HEADS):
        qh = _rope128(q[:, h * HEAD_PAD:(h + 1) * HEAD_PAD], c, s1, s2)
        q_ref[h] = (qh * scale).astype(BF16)


def _q_up(proj, g, wq, tc, ts1, ts2):
    tm = 512
    spt = SEQ // tm
    tab = pl.BlockSpec((tm, LANES), lambda i: (i, 0))
    return pl.pallas_call(
        _q_up_kernel,
        grid=(N_TOK // tm,),
        in_specs=[pl.BlockSpec((tm, Q_RANK), lambda i: (i, COL_QLAT // Q_RANK)),
                  pl.BlockSpec((1, Q_RANK), lambda i: (0, 0)),
                  pl.BlockSpec((Q_RANK, MLA_HEADS * HEAD_PAD), lambda i: (0, 0)),
                  tab, tab, tab],
        out_specs=pl.BlockSpec((None, MLA_HEADS, tm, HEAD_PAD), lambda i: (i // spt, 0, i % spt, 0)),
        out_shape=jax.ShapeDtypeStruct((BATCH, MLA_HEADS, SEQ, HEAD_PAD), BF16),
        compiler_params=_cparams(("parallel",)),
        name="q_up",
    )(proj, g, wq, tc, ts1, ts2)


def _kv_up_kernel(ckv_ref, small_ref, g_ref, wk_ref, wv_ref, c_ref, s1_ref, s2_ref, k_ref, v_ref):
    y = _rms_norm(ckv_ref[...], g_ref[...]).astype(BF16)
    k_all = _dot(y, wk_ref[...])
    v_all = _dot(y, wv_ref[...])
    lane = lax.broadcasted_iota(I32, small_ref.shape, 1)
    kpe_raw = jnp.where((lane >= KPE_LANE) & (lane < KPE_LANE + QK_ROPE), small_ref[...], 0.0)
    kpe = _rope128(kpe_raw, c_ref[...], s1_ref[...], s2_ref[...])
    for h in range(MLA_HEADS):
        k_ref[h] = (k_all[:, h * HEAD_PAD:(h + 1) * HEAD_PAD] + kpe).astype(BF16)
    ones = jnp.ones((v_all.shape[0], LANES), F32)
    for j in range(N_PAIRS):
        v_ref[j] = jnp.concatenate([v_all[:, j * LANES:(j + 1) * LANES], ones], axis=1).astype(BF16)


def _kv_up(proj, g, wk, wv, tc, ts1, ts2):
    tm = 512
    spt = SEQ // tm
    tab = pl.BlockSpec((tm, LANES), lambda i: (i, 0))
    return pl.pallas_call(
        _kv_up_kernel,
        grid=(N_TOK // tm,),
        in_specs=[pl.BlockSpec((tm, KV_RANK), lambda i: (i, COL_CKV // KV_RANK)),
                  pl.BlockSpec((tm, LANES), lambda i: (i, COL_SMALL // LANES)),
                  pl.BlockSpec((1, KV_RANK), lambda i: (0, 0)),
                  pl.BlockSpec((KV_RANK, MLA_HEADS * HEAD_PAD), lambda i: (0, 0)),
                  pl.BlockSpec((KV_RANK, MLA_HEADS * V_DIM), lambda i: (0, 0)),
                  tab, tab, tab],
        out_specs=[pl.BlockSpec((None, MLA_HEADS, tm, HEAD_PAD), lambda i: (i // spt, 0, i % spt, 0)),
                   pl.BlockSpec((None, N_PAIRS, tm, V_SLAB), lambda i: (i // spt, 0, i % spt, 0))],
        out_shape=[jax.ShapeDtypeStruct((BATCH, MLA_HEADS, SEQ, HEAD_PAD), BF16),
                   jax.ShapeDtypeStruct((BATCH, N_PAIRS, SEQ, V_SLAB), BF16)],
        compiler_params=_cparams(("parallel",)),
        name="kv_up",
    )(proj, proj, g, wk, wv, tc, ts1, ts2)


ATT_T = 512


def _attn_kernel(q_ref, k_ref, v_ref, o_ref, m_ref, acc_ref):
    qi = pl.program_id(2)
    t = ATT_T
    m_ref[...] = jnp.full(m_ref.shape, -jnp.inf, F32)
    acc_ref[...] = jnp.zeros(acc_ref.shape, F32)

    def kv_tile(ks, width, masked):
        v = v_ref[pl.ds(ks, width), :]
        for hh in range(2):
            s = lax.dot_general(q_ref[hh], k_ref[hh, pl.ds(ks, width), :], (((1,), (1,)), ((), ())),
                                preferred_element_type=F32)
            if masked:
                row = lax.broadcasted_iota(I32, s.shape, 0)
                col = lax.broadcasted_iota(I32, s.shape, 1)
                s = jnp.where(ks + col <= qi * t + row, s, NEG)
            m_prev = m_ref[hh]
            m_new = jnp.maximum(m_prev, jnp.max(s, axis=-1, keepdims=True))
            alpha = jnp.exp2(m_prev - m_new)
            p = jnp.exp2(s - jnp.concatenate([m_new] * (width // LANES), axis=1))
            acc_ref[hh] = jnp.concatenate([alpha, alpha], axis=1) * acc_ref[hh] + _dot(p.astype(BF16), v)
            m_ref[hh] = m_new

    def body(kp, carry):
        kv_tile(pl.multiple_of(kp * 2 * t, 2 * t), 2 * t, False)
        return carry

    lax.fori_loop(0, qi // 2, body, 0)

    @pl.when(qi % 2 == 0)
    def _():
        kv_tile(pl.multiple_of(qi * t, t), t, True)

    @pl.when(qi % 2 == 1)
    def _():
        kv_tile(pl.multiple_of((qi - 1) * t, t), 2 * t, True)

    lane = lax.broadcasted_iota(I32, o_ref.shape, 1)
    a0 = acc_ref[0]
    a1 = acc_ref[1]
    o0 = a0[:, :LANES] / a0[:, LANES:]
    o1 = a1[:, :LANES] / a1[:, LANES:]
    o_ref[...] = jnp.where(lane < V_DIM, o0, o1).astype(o_ref.dtype)


def _attention(q, k, v):
    t = ATT_T
    return pl.pallas_call(
        _attn_kernel,
        grid=(BATCH, N_PAIRS, SEQ // t),
        in_specs=[pl.BlockSpec((None, 2, t, HEAD_PAD), lambda b, j, qi: (b, j, qi, 0)),
                  pl.BlockSpec((None, 2, SEQ, HEAD_PAD), lambda b, j, qi: (b, j, 0, 0)),
                  pl.BlockSpec((None, None, SEQ, V_SLAB), lambda b, j, qi: (b, j, 0, 0))],
        out_specs=pl.BlockSpec((None, t, LANES), lambda b, j, qi: (b, qi, j)),
        out_shape=jax.ShapeDtypeStruct((BATCH, SEQ, MLA_HEADS * V_DIM), BF16),
        scratch_shapes=[pltpu.VMEM((2, t, LANES), F32), pltpu.VMEM((2, t, V_SLAB), F32)],
        compiler_params=_cparams(("parallel", "parallel", "parallel")),
        name="mla_attention",
    )(q, k, v)


HALO = SUBLANES
HEADS_PER_GROUP = SSM_HEADS // SSM_GROUPS
GROUP_W = SSM_INNER // SSM_GROUPS


def _ssd_kernel(xbc_ref, halo_ref, z_ref, small_ref, cw_ref, cb_ref, dtb_ref, alog_ref, dexp_ref,
                nw_ref, e_ref, y_ref, st_ref, cat_ref):
    c = pl.program_id(1)
    L = SSM_CHUNK

    @pl.when(c == 0)
    def _():
        st_ref[...] = jnp.zeros(st_ref.shape, F32)

    cat_ref[0:HALO, :] = jnp.where(c == 0, 0.0, halo_ref[...])
    cat_ref[HALO:HALO + L, :] = xbc_ref[...]
    acc = jnp.broadcast_to(cb_ref[...], (L, SSM_CONV_DIM))
    for kk in range(SSM_CONV):
        off = HALO - (SSM_CONV - 1) + kk
        acc = acc + cw_ref[kk:kk + 1, :] * cat_ref[off:off + L, :]
    xc = _silu(acc)
    xs = xc[:, :SSM_INNER]
    bm = xc[:, SSM_INNER:SSM_INNER + SSM_GROUPS * SSM_STATE]
    cm = xc[:, SSM_INNER + SSM_GROUPS * SSM_STATE:]

    lane = lax.broadcasted_iota(I32, (L, LANES), 1)
    dt_lanes = (lane >= DT_LANE) & (lane < DT_LANE + SSM_HEADS)
    dt_in = small_ref[...] + dtb_ref[...]
    dt = jnp.maximum(dt_in, 0.0) + jnp.log1p(jnp.exp(-jnp.abs(dt_in)))
    dt = jnp.where(dt_lanes, dt, 0.0)
    a = -jnp.exp(alog_ref[...])
    a_dt = dt * a
    row = lax.broadcasted_iota(I32, (L, L), 0)
    col = lax.broadcasted_iota(I32, (L, L), 1)
    tri = row >= col
    a_cum = jnp.dot(tri.astype(F32), a_dt, precision=HIGHEST, preferred_element_type=F32)
    a_last = a_cum[L - 1:L, :]
    e = e_ref[...]

    def expand(t):
        return jnp.dot(t, e, precision=HIGHEST, preferred_element_type=F32)

    x_dt = xs * expand(dt)
    eac_x = expand(jnp.exp(a_cum))
    ds_x = expand(jnp.exp(a_last - a_cum))
    cd_x = eac_x[L - 1:L, :]
    xd = (x_dt * ds_x).astype(BF16)
    x_bf = x_dt.astype(BF16)
    a_cum_t = a_cum.T
    lane_h = lax.broadcasted_iota(I32, (L, LANES), 1)

    y_parts = []
    for g in range(SSM_GROUPS):
        bg = bm[:, g * SSM_STATE:(g + 1) * SSM_STATE].astype(BF16)
        cg = cm[:, g * SSM_STATE:(g + 1) * SSM_STATE].astype(BF16)
        cb = lax.dot_general(cg, bg, (((1,), (1,)), ((), ())), preferred_element_type=F32)
        st_prev = st_ref[g]
        y_off = _dot(cg, st_prev.astype(BF16)) * eac_x[:, g * GROUP_W:(g + 1) * GROUP_W]
        for jp in range(HEADS_PER_GROUP // 2):
            pair = g * (HEADS_PER_GROUP // 2) + jp
            xp = x_bf[:, pair * LANES:(pair + 1) * LANES]
            outs = []
            for hh in range(2):
                hl = DT_LANE + 2 * pair + hh
                seg = a_cum[:, hl:hl + 1] - a_cum_t[hl:hl + 1, :]
                decay = jnp.where(tri, jnp.exp(seg), 0.0)
                outs.append(_dot((cb * decay).astype(BF16), xp))
            y_parts.append(jnp.where(lane_h < SSM_HEADDIM, outs[0], outs[1])
                           + y_off[:, jp * LANES:(jp + 1) * LANES])
        upd = lax.dot_general(bg, xd[:, g * GROUP_W:(g + 1) * GROUP_W], (((0,), (0,)), ((), ())),
                              preferred_element_type=F32)
        st_ref[g] = st_prev * cd_x[:, g * GROUP_W:(g + 1) * GROUP_W] + upd

    y = jnp.concatenate(y_parts, axis=1) + xs * dexp_ref[...]
    y = y * _silu(z_ref[...])
    normed = []
    for g in range(SSM_GROUPS):
        yg = y[:, g * GROUP_W:(g + 1) * GROUP_W]
        normed.append(yg * lax.rsqrt(jnp.mean(yg * yg, axis=-1, keepdims=True) + RMS_EPS))
    y_ref[...] = (jnp.concatenate(normed, axis=1) * nw_ref[...]).astype(y_ref.dtype)


def _ssd(proj, cw, cb, dtb, alog, dexp, nw, e_mat):
    L = SSM_CHUNK
    nc = SEQ // L
    xbc_blk = COL_XBC // SSM_CONV_DIM

    def row1(w):
        return pl.BlockSpec((1, w), lambda b, c: (0, 0))

    return pl.pallas_call(
        _ssd_kernel,
        grid=(BATCH, nc),
        in_specs=[pl.BlockSpec((L, SSM_CONV_DIM), lambda b, c: (b * nc + c, xbc_blk)),
                  pl.BlockSpec((HALO, SSM_CONV_DIM),
                               lambda b, c: (jnp.maximum((b * nc + c) * (L // HALO) - 1, 0), xbc_blk)),
                  pl.BlockSpec((L, SSM_INNER), lambda b, c: (b * nc + c, COL_Z // SSM_INNER)),
                  pl.BlockSpec((L, LANES), lambda b, c: (b * nc + c, COL_SMALL // LANES)),
                  pl.BlockSpec((SSM_CONV, SSM_CONV_DIM), lambda b, c: (0, 0)),
                  row1(SSM_CONV_DIM), row1(LANES), row1(LANES), row1(SSM_INNER), row1(SSM_INNER),
                  pl.BlockSpec((LANES, SSM_INNER), lambda b, c: (0, 0))],
        out_specs=pl.BlockSpec((L, SSM_INNER), lambda b, c: (b * nc + c, 0)),
        out_shape=jax.ShapeDtypeStruct((N_TOK, SSM_INNER), BF16),
        scratch_shapes=[pltpu.VMEM((SSM_GROUPS, SSM_STATE, GROUP_W), F32),
                        pltpu.VMEM((HALO + L, SSM_CONV_DIM), F32)],
        compiler_params=_cparams(("parallel", "arbitrary")),
        name="mamba2_ssd",
    )(proj, proj, proj, proj, cw, cb, dtb, alog, dexp, nw, e_mat)


def _merge_kernel(attn_ref, ssm_ref, ga_ref, gb_ref, h_ref, wa_ref, ws_ref, wo_ref, g_ref, b_ref,
                  hf_ref, hb_ref):
    ya = _dot(attn_ref[...], wa_ref[...])
    yb = _dot(ssm_ref[...], ws_ref[...])
    mix = _sigmoid(ga_ref[...]) * ya + _sigmoid(gb_ref[...]) * yb
    mixed = _dot(mix.astype(BF16), wo_ref[...])
    h1 = _layer_norm(ALPHA * h_ref[...] + mixed, g_ref[...], b_ref[...])
    hf_ref[...] = h1
    hb_ref[...] = h1.astype(BF16)


def _merge(attn, ssm_y, proj, hf, wa, ws, wo, g, b):
    tm = 256
    full = lambda r, c: pl.BlockSpec((r, c), lambda i: (0, 0))
    tile = lambda c, j=0: pl.BlockSpec((tm, c), lambda i: (i, j))
    return pl.pallas_call(
        _merge_kernel,
        grid=(N_TOK // tm,),
        in_specs=[tile(MLA_HEADS * V_DIM), tile(SSM_INNER), tile(D_MODEL, COL_GATE_A // D_MODEL),
                  tile(D_MODEL, COL_GATE_B // D_MODEL), tile(D_MODEL),
                  full(MLA_HEADS * V_DIM, D_MODEL), full(SSM_INNER, D_MODEL), full(D_MODEL, D_MODEL),
                  full(1, D_MODEL), full(1, D_MODEL)],
        out_specs=[tile(D_MODEL), tile(D_MODEL)],
        out_shape=[jax.ShapeDtypeStruct((N_TOK, D_MODEL), F32),
                   jax.ShapeDtypeStruct((N_TOK, D_MODEL), BF16)],
        compiler_params=_cparams(("parallel",)),
        name="merge_ln1",
    )(attn, ssm_y, proj, proj, hf, wa, ws, wo, g, b)


def _first_argmax(vals, lane):
    m = jnp.max(vals, axis=-1, keepdims=True)
    idx = jnp.min(jnp.where(vals == m, lane, float(N_EXPERTS)), axis=-1, keepdims=True)
    return m, idx


def _post_kernel(hf_ref, hb_ref, p_ref, wr_ref, rb_ref, wsg_ref, wsu_ref, wsd_ref, wp_ref, wpg_ref,
                 base_ref, idx_ref, wt_ref, cnt_ref):
    hf = hf_ref[...]
    hb = hb_ref[...]
    tm = hf.shape[0]
    logits = jnp.dot(hf, wr_ref[...], precision=HIGHEST, preferred_element_type=F32)
    scores = _sigmoid(logits)
    sel = scores + rb_ref[...]
    lane_i = lax.broadcasted_iota(I32, (tm, N_EXPERTS), 1)
    grp = lane_i // PER_GROUP
    lane = lane_i.astype(F32)

    grp_scores = []
    for g in range(N_GROUPS):
        vals = jnp.where(grp == g, sel, -jnp.inf)
        m1, i1 = _first_argmax(vals, lane)
        m2 = jnp.max(jnp.where(lane == i1, -jnp.inf, vals), axis=-1, keepdims=True)
        grp_scores.append(m1 + m2)
    keep = jnp.zeros((tm, N_EXPERTS), jnp.bool_)
    for g in range(N_GROUPS):
        rank = jnp.zeros((tm, 1), I32)
        for o in range(N_GROUPS):
            if o == g:
                continue
            ahead = (grp_scores[o] > grp_scores[g]) if o > g else (grp_scores[o] >= grp_scores[g])
            rank = rank + ahead.astype(I32)
        keep = keep | ((grp == g) & (rank < TOPK_GROUPS))
    masked = jnp.where(keep, sel, -jnp.inf)

    lane_k = lax.broadcasted_iota(I32, (tm, TOP_K), 1)
    idx_out = jnp.zeros((tm, TOP_K), I32)
    w_out = jnp.zeros((tm, TOP_K), F32)
    chosen = jnp.zeros((tm, N_EXPERTS), F32)
    for kk in range(TOP_K):
        _, ik = _first_argmax(masked, lane)
        hit = lane == ik
        wk = jnp.sum(jnp.where(hit, scores, 0.0), axis=-1, keepdims=True)
        masked = jnp.where(hit, -jnp.inf, masked)
        chosen = jnp.where(hit, 1.0, chosen)
        idx_out = jnp.where(lane_k == kk, ik.astype(I32), idx_out)
        w_out = jnp.where(lane_k == kk, wk, w_out)
    w_out = w_out / jnp.sum(w_out, axis=-1, keepdims=True) * ROUTED_SCALE
    idx_ref[...] = idx_out
    wt_ref[...] = w_out
    cnt_ref[...] = jnp.sum(chosen, axis=0, keepdims=True)

    shared = _dot((_silu(_dot(hb, wsg_ref[...])) * _dot(hb, wsu_ref[...])).astype(BF16), wsd_ref[...])
    ple = _dot(p_ref[...].astype(BF16), wp_ref[...]) * _sigmoid(_dot(hb, wpg_ref[...]))
    base_ref[...] = ALPHA * hf + shared + ple


def _post(hf, hb, p, wr, rb, wsg, wsu, wsd, wp, wpg):
    tm = ROUTE_T
    full = lambda r, c: pl.BlockSpec((r, c), lambda i: (0, 0))
    tile = lambda c: pl.BlockSpec((tm, c), lambda i: (i, 0))
    return pl.pallas_call(
        _post_kernel,
        grid=(N_RTILES,),
        in_specs=[tile(D_MODEL), tile(D_MODEL), tile(PLE_DIM),
                  full(D_MODEL, N_EXPERTS), full(1, N_EXPERTS),
                  full(D_MODEL, EXPERT_FF), full(D_MODEL, EXPERT_FF), full(EXPERT_FF, D_MODEL),
                  full(PLE_DIM, D_MODEL), full(D_MODEL, D_MODEL)],
        out_specs=[tile(D_MODEL), tile(TOP_K), tile(TOP_K),
                   pl.BlockSpec((None, 1, N_EXPERTS), lambda i: (i, 0, 0))],
        out_shape=[jax.ShapeDtypeStruct((N_TOK, D_MODEL), F32),
                   jax.ShapeDtypeStruct((N_TOK, TOP_K), I32),
                   jax.ShapeDtypeStruct((N_TOK, TOP_K), F32),
                   jax.ShapeDtypeStruct((N_RTILES, 1, N_EXPERTS), F32)],
        compiler_params=_cparams(("parallel",)),
        name="router_shared_ple",
    )(hf, hb, p, wr, rb, wsg, wsu, wsd, wp, wpg)


def _pos_kernel(idx_ref, base_ref, pos_ref):
    idx = idx_ref[...]
    tm = idx.shape[0]
    lane = lax.broadcasted_iota(I32, (tm, N_EXPERTS), 1)
    hits = [lane == idx[:, kk:kk + 1] for kk in range(TOP_K)]
    chosen = hits[0]
    for kk in range(1, TOP_K):
        chosen = chosen | hits[kk]
    r = lax.broadcasted_iota(I32, (tm, tm), 0)
    c = lax.broadcasted_iota(I32, (tm, tm), 1)
    earlier = jnp.where(c < r, 1.0, 0.0).astype(BF16)
    rank = _dot(earlier, jnp.where(chosen, 1.0, 0.0).astype(BF16))
    slot = rank + base_ref[...]
    lane_k = lax.broadcasted_iota(I32, (tm, TOP_K), 1)
    out = jnp.zeros((tm, TOP_K), I32)
    for kk in range(TOP_K):
        pk = jnp.sum(jnp.where(hits[kk], slot, 0.0), axis=-1, keepdims=True)
        out = jnp.where(lane_k == kk, pk.astype(I32), out)
    pos_ref[...] = out


def _slot_positions(idx, base):
    tm = ROUTE_T
    return pl.pallas_call(
        _pos_kernel,
        grid=(N_RTILES,),
        in_specs=[pl.BlockSpec((tm, TOP_K), lambda i: (i, 0)),
                  pl.BlockSpec((None, 1, N_EXPERTS), lambda i: (i, 0, 0))],
        out_specs=pl.BlockSpec((tm, TOP_K), lambda i: (i, 0)),
        out_shape=jax.ShapeDtypeStruct((N_TOK, TOP_K), I32),
        compiler_params=_cparams(("parallel",)),
        name="slot_positions",
    )(idx, base)


def _slot_layout(cnt):
    counts_te = cnt.reshape(N_RTILES, N_EXPERTS).astype(I32)
    counts = jnp.sum(counts_te, axis=0)
    padded = (counts + SLOT_BLOCK - 1) // SLOT_BLOCK * SLOT_BLOCK
    pad_end = jnp.cumsum(padded)
    pad_start = pad_end - padded
    tile_off = jnp.cumsum(counts_te, axis=0) - counts_te
    base = (pad_start[None, :] + tile_off).astype(F32).reshape(N_RTILES, 1, N_EXPERTS)
    n_used = (pad_end[-1] // SLOT_BLOCK).astype(I32).reshape(1)
    return base, counts, pad_start, padded, n_used


DISP_T = 256


def _dispatch_kernel(cnt_ref, ps_ref, pd_ref, nu_ref, pos_hbm, h3_ref, x_hbm,
                     pos_smem, zbuf, pos_sem, row_sem, pad_sem):
    i = pl.program_id(0)
    n = pl.num_programs(0)

    def pos_copy(blk, slot):
        return pltpu.make_async_copy(pos_hbm.at[blk], pos_smem.at[slot], pos_sem.at[slot])

    def pad_row(e, r):
        return pltpu.make_async_copy(zbuf.at[0], x_hbm.at[ps_ref[e] + r], pad_sem)

    def pad_block(b):
        return pltpu.make_async_copy(zbuf, x_hbm.at[pl.ds(b * SLOT_BLOCK, SLOT_BLOCK)], pad_sem)

    def for_each_pad(fn_row, fn_block):
        def per_expert(e, carry):
            def per_row(r, c2):
                fn_row(e, r)
                return c2
            return lax.fori_loop(cnt_ref[e], pd_ref[e], per_row, carry)
        lax.fori_loop(0, N_EXPERTS, per_expert, 0)

        def per_block(b, carry):
            fn_block(b)
            return carry
        lax.fori_loop(nu_ref[0], N_BLOCKS, per_block, 0)

    @pl.when(i == 0)
    def _():
        pos_copy(0, 0).start()
        pos_copy(1, 1).start()
        zbuf[...] = jnp.zeros(zbuf.shape, F32)
        for_each_pad(lambda e, r: pad_row(e, r).start(), lambda b: pad_block(b).start())
        for_each_pad(lambda e, r: pad_row(e, r).wait(), lambda b: pad_block(b).wait())

    slot = i % 2
    pos_copy(i, slot).wait()

    def body(t, carry):
        for kk in range(TOP_K):
            dst = pos_smem[slot, t * TOP_K + kk]
            pltpu.make_async_copy(h3_ref.at[t], x_hbm.at[dst], row_sem).start(priority=kk % 2)
        return carry
    lax.fori_loop(0, DISP_T, body, 0)

    @pl.when(i + 2 < n)
    def _():
        pos_copy(i + 2, slot).start()

    for kk in range(TOP_K):
        pltpu.make_async_copy(h3_ref, x_hbm.at[pl.ds(0, DISP_T)], row_sem).wait()


def _dispatch(counts, pad_start, padded, n_used, pos, h3):
    n_tiles = N_TOK // DISP_T
    grid_spec = pltpu.PrefetchScalarGridSpec(
        num_scalar_prefetch=4,
        grid=(n_tiles,),
        in_specs=[pl.BlockSpec(memory_space=pl.ANY),
                  pl.BlockSpec((DISP_T, D_CHUNKS, LANES), lambda i, *_: (i, 0, 0))],
        out_specs=pl.BlockSpec(memory_space=pl.ANY),
        scratch_shapes=[pltpu.SMEM((2, DISP_T * TOP_K), I32),
                        pltpu.VMEM((SLOT_BLOCK, D_CHUNKS, LANES), F32),
                        pltpu.SemaphoreType.DMA((2,)),
                        pltpu.SemaphoreType.DMA,
                        pltpu.SemaphoreType.DMA])
    return pl.pallas_call(
        _dispatch_kernel,
        grid_spec=grid_spec,
        out_shape=jax.ShapeDtypeStruct((CAP, D_CHUNKS, LANES), F32),
        compiler_params=_cparams(("arbitrary",)),
        name="dispatch_scatter",
    )(counts, pad_start, padded, n_used, pos.reshape(n_tiles, DISP_T * TOP_K), h3)


def _moe_kernel(fb_ref, nb_ref, nu_ref, x_hbm, wg_ref, wu_ref, wd_ref, y_hbm,
                xbuf, ybuf, wg_bf, wu_bf, wd_bf, in_sem, out_sem):
    e = pl.program_id(0)
    first = fb_ref[e]
    nb = nb_ref[e]
    n_used = nu_ref[0]

    def rows(g):
        return pl.ds(pl.multiple_of(g * SLOT_BLOCK, SLOT_BLOCK), SLOT_BLOCK)

    def x_copy(g):
        slot = g % MOE_RING
        return pltpu.make_async_copy(x_hbm.at[rows(g)], xbuf.at[slot], in_sem.at[slot])

    def y_copy(g):
        slot = g % MOE_RING
        return pltpu.make_async_copy(ybuf.at[slot], y_hbm.at[rows(g)], out_sem.at[slot])

    @pl.when(e == 0)
    def _():
        for g in range(MOE_RING - 1):
            x_copy(g).start()

    @pl.when(nb > 0)
    def _():
        wg_bf[...] = wg_ref[...].astype(BF16)
        wu_bf[...] = wu_ref[...].astype(BF16)
        wd_bf[...] = wd_ref[...].astype(BF16)

        def body(g, carry):
            slot = g % MOE_RING
            x_copy(g).wait()

            @pl.when(g + MOE_RING - 1 < n_used)
            def _():
                x_copy(g + MOE_RING - 1).start()

            x = _rows_to_matrix(xbuf[slot]).astype(BF16)
            hid = (_silu(_dot(x, wg_bf[...])) * _dot(x, wu_bf[...])).astype(BF16)
            y3 = _matrix_to_rows(_dot(hid, wd_bf[...]))

            @pl.when(g >= MOE_RING)
            def _():
                y_copy(g - MOE_RING).wait()

            ybuf[slot] = y3
            y_copy(g).start()
            return carry

        lax.fori_loop(first, first + nb, body, 0)

    @pl.when(e == N_EXPERTS - 1)
    def _():
        for j in range(MOE_RING):
            y_copy(n_used - 1 - j).wait()
        ybuf[0] = jnp.zeros(ybuf.shape[1:], F32)

        def tail_copy(b):
            return pltpu.make_async_copy(
                ybuf.at[0], y_hbm.at[pl.ds(pl.multiple_of(b * SLOT_BLOCK, SLOT_BLOCK), SLOT_BLOCK)],
                out_sem.at[0])

        def start(b, carry):
            tail_copy(b).start()
            return carry

        def wait(b, carry):
            tail_copy(b).wait()
            return carry

        lax.fori_loop(nu_ref[0], N_BLOCKS, start, 0)
        lax.fori_loop(nu_ref[0], N_BLOCKS, wait, 0)


def _moe(first_block, n_blocks, n_used, xs, wg, wu, wd, layer):
    wspec = lambda r, c: pl.BlockSpec((None, None, r, c), lambda e, *_: (layer, e, 0, 0))
    grid_spec = pltpu.PrefetchScalarGridSpec(
        num_scalar_prefetch=3,
        grid=(N_EXPERTS,),
        in_specs=[pl.BlockSpec(memory_space=pl.ANY),
                  wspec(D_MODEL, EXPERT_FF), wspec(D_MODEL, EXPERT_FF), wspec(EXPERT_FF, D_MODEL)],
        out_specs=pl.BlockSpec(memory_space=pl.ANY),
        scratch_shapes=[pltpu.VMEM((MOE_RING, SLOT_BLOCK, D_CHUNKS, LANES), F32),
                        pltpu.VMEM((MOE_RING, SLOT_BLOCK, D_CHUNKS, LANES), F32),
                        pltpu.VMEM((D_MODEL, EXPERT_FF), BF16),
                        pltpu.VMEM((D_MODEL, EXPERT_FF), BF16),
                        pltpu.VMEM((EXPERT_FF, D_MODEL), BF16),
                        pltpu.SemaphoreType.DMA((MOE_RING,)),
                        pltpu.SemaphoreType.DMA((MOE_RING,))])
    return pl.pallas_call(
        _moe_kernel,
        grid_spec=grid_spec,
        out_shape=jax.ShapeDtypeStruct((CAP, D_CHUNKS, LANES), F32),
        compiler_params=_cparams(("arbitrary",)),
        name="routed_experts",
    )(first_block, n_blocks, n_used, xs, wg, wu, wd)


COMB_T = 128


def _combine_kernel(pos_hbm, y_hbm, w_ref, base_ref, g_ref, b_ref, hf_ref, hb_ref,
                    pos_smem, buf, pos_sem, row_sem):
    i = pl.program_id(0)
    n = pl.num_programs(0)

    def pos_copy(blk, slot):
        return pltpu.make_async_copy(pos_hbm.at[blk], pos_smem.at[slot], pos_sem.at[slot])

    def issue_rows(slot):
        def body(t, carry):
            for kk in range(TOP_K):
                src = pos_smem[slot, t * TOP_K + kk]
                pltpu.make_async_copy(y_hbm.at[src], buf.at[slot, t, kk],
                                      row_sem.at[slot]).start(priority=kk % 2)
            return carry
        lax.fori_loop(0, COMB_T, body, 0)

    def wait_rows(slot):
        for kk in range(TOP_K):
            pltpu.make_async_copy(y_hbm.at[pl.ds(0, COMB_T)], buf.at[slot, :, kk],
                                  row_sem.at[slot]).wait()

    @pl.when(i == 0)
    def _():
        pos_copy(0, 0).start()
        pos_copy(1, 1).start()
        pos_copy(0, 0).wait()
        issue_rows(0)

    nxt = i + 1
    nslot = nxt % 2

    @pl.when(nxt < n)
    def _():
        pos_copy(nxt, nslot).wait()
        issue_rows(nslot)

    @pl.when(i + 2 < n)
    def _():
        pos_copy(i + 2, i % 2).start()

    slot = i % 2
    wait_rows(slot)
    rows = buf.at[slot]
    routed = None
    for kk in range(TOP_K):
        term = w_ref[:, kk:kk + 1] * _rows_to_matrix(rows[:, kk])
        routed = term if routed is None else routed + term
    h2 = _layer_norm(base_ref[...] + routed, g_ref[...], b_ref[...])
    hf_ref[...] = h2
    hb_ref[...] = h2.astype(BF16)


def _combine(pos, y3, wts, base, g, b):
    tm = COMB_T
    n_tiles = N_TOK // tm
    full = lambda r, c: pl.BlockSpec((r, c), lambda i: (0, 0))
    tile = lambda c: pl.BlockSpec((tm, c), lambda i: (i, 0))
    return pl.pallas_call(
        _combine_kernel,
        grid=(n_tiles,),
        in_specs=[pl.BlockSpec(memory_space=pl.ANY), pl.BlockSpec(memory_space=pl.ANY),
                  tile(TOP_K), tile(D_MODEL), full(1, D_MODEL), full(1, D_MODEL)],
        out_specs=[tile(D_MODEL), tile(D_MODEL)],
        out_shape=[jax.ShapeDtypeStruct((N_TOK, D_MODEL), F32),
                   jax.ShapeDtypeStruct((N_TOK, D_MODEL), BF16)],
        scratch_shapes=[pltpu.SMEM((2, tm * TOP_K), I32),
                        pltpu.VMEM((2, tm, TOP_K, D_CHUNKS, LANES), F32),
                        pltpu.SemaphoreType.DMA((2,)),
                        pltpu.SemaphoreType.DMA((2,))],
        compiler_params=_cparams(("arbitrary",)),
        name="combine_ln2",
    )(pos.reshape(n_tiles, tm * TOP_K), y3, wts, base, g, b)


def _pack_w_in(w):
    q_lat = w[:, 0:384]
    c_kv = w[:, 384:640]
    k_pe = w[:, 640:672]
    z = w[:, 672:1696]
    xbc = w[:, 1696:3232]
    dt = w[:, 3232:3248]
    gate_a = w[:, 3248:4272]
    gate_b = w[:, 4272:5296]
    zeros = lambda n: jnp.zeros((D_MODEL, n), w.dtype)
    small = jnp.concatenate([zeros(KPE_LANE), k_pe, dt, zeros(LANES - DT_LANE - SSM_HEADS)], axis=1)
    return jnp.concatenate([gate_a, gate_b, z, xbc, q_lat, small, c_kv], axis=1).astype(BF16)


def _pack_w_q(w):
    w = w.reshape(Q_RANK, MLA_HEADS, QK_NOPE + QK_ROPE)
    w = jnp.pad(w, ((0, 0), (0, 0), (0, HEAD_PAD - QK_NOPE - QK_ROPE)))
    return w.reshape(Q_RANK, MLA_HEADS * HEAD_PAD).astype(BF16)


def _pack_w_kv(w):
    w = w.reshape(KV_RANK, MLA_HEADS, QK_NOPE + V_DIM)
    wk = jnp.pad(w[:, :, :QK_NOPE], ((0, 0), (0, 0), (0, HEAD_PAD - QK_NOPE)))
    wv = w[:, :, QK_NOPE:]
    return (wk.reshape(KV_RANK, MLA_HEADS * HEAD_PAD).astype(BF16),
            wv.reshape(KV_RANK, MLA_HEADS * V_DIM).astype(BF16))


def _rope_tables(positions):
    half = QK_ROPE // 2
    inv_freq = jnp.power(ROPE_THETA, -jnp.arange(half, dtype=F32) * (2.0 / QK_ROPE))
    ang = positions.astype(F32).reshape(N_TOK, 1) * inv_freq
    cos, sin = jnp.cos(ang), jnp.sin(ang)
    z = lambda n: jnp.zeros((N_TOK, n), F32)
    tail = HEAD_PAD - QK_NOPE - QK_ROPE
    tc = jnp.concatenate([jnp.ones((N_TOK, QK_NOPE), F32), cos, cos, z(tail)], axis=1)
    ts1 = jnp.concatenate([z(QK_NOPE), -sin, z(half), z(tail)], axis=1)
    ts2 = jnp.concatenate([z(QK_NOPE), z(half), sin, z(tail)], axis=1)
    return tc, ts1, ts2


def _head_lane_row(v):
    return jnp.zeros((1, LANES), F32).at[0, DT_LANE:DT_LANE + SSM_HEADS].set(v.astype(F32))


def _expand_matrix():
    r = jnp.arange(LANES)[:, None]
    c = jnp.arange(SSM_INNER)[None, :]
    return ((r - DT_LANE) == (c // SSM_HEADDIM)).astype(F32)


def kernel(x, p, positions, w_in, q_norm, w_q_up, kv_norm, w_kv_up, conv_w, conv_b, dt_bias, a_log, d_skip,
           ssm_norm, w_attn_br, w_ssm_br, w_o, ln1_g, ln1_b, w_router, router_bias, w_exp_gate, w_exp_up,
           w_exp_down, w_sh_gate, w_sh_up, w_sh_down, w_ple, w_ple_gate, ln2_g, ln2_b):
    tc, ts1, ts2 = _rope_tables(positions)
    e_mat = _expand_matrix()
    hf = x.reshape(N_TOK, D_MODEL)
    hb = hf.astype(BF16)
    row = lambda v: v.reshape(1, -1).astype(F32)
    for i in range(DEPTH):
        proj = _in_proj(hb, _pack_w_in(w_in[i]))
        q = _q_up(proj, row(q_norm[i]), _pack_w_q(w_q_up[i]), tc, ts1, ts2)
        wk, wv = _pack_w_kv(w_kv_up[i])
        k, v = _kv_up(proj, row(kv_norm[i]), wk, wv, tc, ts1, ts2)
        attn = _attention(q, k, v).reshape(N_TOK, MLA_HEADS * V_DIM)
        ssm_y = _ssd(proj, conv_w[i], row(conv_b[i]), _head_lane_row(dt_bias[i]), _head_lane_row(a_log[i]),
                     row(jnp.repeat(d_skip[i], SSM_HEADDIM)), row(ssm_norm[i]), e_mat)
        hf, hb = _merge(attn, ssm_y, proj, hf, w_attn_br[i].astype(BF16), w_ssm_br[i].astype(BF16),
                        w_o[i].astype(BF16), row(ln1_g[i]), row(ln1_b[i]))
        base, idx, wts, cnt = _post(hf, hb, p[i].reshape(N_TOK, PLE_DIM), w_router[i], row(router_bias[i]),
                                    w_sh_gate[i].astype(BF16), w_sh_up[i].astype(BF16),
                                    w_sh_down[i].astype(BF16), w_ple[i].astype(BF16),
                                    w_ple_gate[i].astype(BF16))
        base_slots, counts, pad_start, padded, n_used = _slot_layout(cnt)
        pos = _slot_positions(idx, base_slots)
        xs = _dispatch(counts, pad_start, padded, n_used, pos, hf.reshape(N_TOK, D_CHUNKS, LANES))
        y3 = _moe(pad_start // SLOT_BLOCK, padded // SLOT_BLOCK, n_used, xs,
                  w_exp_gate, w_exp_up, w_exp_down, i)
        hf, hb = _combine(pos, y3, wts, base, row(ln2_g[i]), row(ln2_b[i]))
    return hf.reshape(BATCH, SEQ, D_MODEL)
```

```python
import functools
import math

import jax
import jax.numpy as jnp
from jax import lax
from jax.experimental import pallas as pl
from jax.experimental.pallas import tpu as pltpu

F32 = jnp.float32
BF16 = jnp.bfloat16
I32 = jnp.int32
HIGHEST = lax.Precision.HIGHEST

D_MODEL = 1024
BATCH = 4
SEQ = 8192
DEPTH = 2
N_TOK = BATCH * SEQ
MLA_HEADS = 8
QK_NOPE = 64
QK_ROPE = 32
V_DIM = 64
Q_RANK = 384
KV_RANK = 256
ROPE_THETA = 10000.0
SSM_INNER = 1024
SSM_HEADDIM = 64
SSM_HEADS = 16
SSM_GROUPS = 2
SSM_STATE = 128
SSM_CONV = 4
SSM_CHUNK = 128
SSM_CONV_DIM = 1536
N_EXPERTS = 256
TOP_K = 8
N_GROUPS = 8
TOPK_GROUPS = 4
PER_GROUP = N_EXPERTS // N_GROUPS
EXPERT_FF = 256
ROUTED_SCALE = 2.5
PLE_DIM = 256
ALPHA = (2 * DEPTH) ** 0.25
LN_EPS = 1e-5
RMS_EPS = 1e-6

LANES = 128
SUBLANES = 8
VMEM_LIMIT = 48 * 1024 * 1024

COL_GATE_A = 0
COL_GATE_B = 1024
COL_Z = 2048
COL_XBC = 3072
COL_QLAT = 4608
COL_SMALL = 4992
COL_CKV = 5120
PACK_COLS = 5376
KPE_LANE = 64
DT_LANE = 96

HEAD_PAD = 128
N_PAIRS = MLA_HEADS // 2
V_SLAB = 2 * LANES

ROUTE_T = 256
N_RTILES = N_TOK // ROUTE_T
SLOT_BLOCK = 128
N_ASSIGN = N_TOK * TOP_K
N_BLOCKS = N_ASSIGN // SLOT_BLOCK + N_EXPERTS
CAP = N_BLOCKS * SLOT_BLOCK
D_CHUNKS = D_MODEL // LANES
MOE_RING = 4

NEG = float(jnp.finfo(jnp.float32).min)


def _cparams(sem):
    return pltpu.CompilerParams(dimension_semantics=sem, vmem_limit_bytes=VMEM_LIMIT)


def _sigmoid(x):
    return 1.0 / (1.0 + jnp.exp(-x))


def _silu(x):
    return x * _sigmoid(x)


def _layer_norm(x, g, b):
    mu = jnp.mean(x, axis=-1, keepdims=True)
    xc = x - mu
    var = jnp.mean(xc * xc, axis=-1, keepdims=True)
    return xc * lax.rsqrt(var + LN_EPS) * g + b


def _rms_norm(x, g):
    return x * lax.rsqrt(jnp.mean(x * x, axis=-1, keepdims=True) + RMS_EPS) * g


def _dot(a, b):
    return jnp.dot(a, b, preferred_element_type=F32)


def _swap_sublane_major(x):
    groups = x.shape[0]
    for d in (4, 2, 1):
        y = x.reshape(groups, SUBLANES // (2 * d), 2, d, SUBLANES, LANES)
        lo, hi = y[:, :, 0], y[:, :, 1]
        shp = lo.shape
        keep = (lax.broadcasted_iota(I32, shp, 3) & d) == 0
        rot = lambda v, s: pltpu.roll(v.reshape(-1, SUBLANES, LANES), s, 1).reshape(shp)
        new_lo = jnp.where(keep, lo, rot(hi, d))
        new_hi = jnp.where(keep, rot(lo, SUBLANES - d), hi)
        x = jnp.stack([new_lo, new_hi], axis=2).reshape(groups, SUBLANES, SUBLANES, LANES)
    return x


def _rows_to_matrix(x3):
    rows = x3.shape[0]
    t = _swap_sublane_major(x3.reshape(rows // SUBLANES, SUBLANES, D_CHUNKS, LANES))
    return jnp.concatenate([t[:, cc].reshape(rows, LANES) for cc in range(D_CHUNKS)], axis=1)


def _matrix_to_rows(y):
    rows = y.shape[0]
    t = jnp.stack([y[:, cc * LANES:(cc + 1) * LANES].reshape(rows // SUBLANES, SUBLANES, LANES)
                   for cc in range(D_CHUNKS)], axis=1)
    return _swap_sublane_major(t).reshape(rows, D_CHUNKS, LANES)


def _mm_kernel(x_ref, w_ref, o_ref):
    o_ref[...] = _dot(x_ref[...], w_ref[...]).astype(o_ref.dtype)


def _in_proj(hb, w_pack):
    tm, tn = 1024, 1792
    return pl.pallas_call(
        _mm_kernel,
        grid=(PACK_COLS // tn, N_TOK // tm),
        in_specs=[pl.BlockSpec((tm, D_MODEL), lambda j, i: (i, 0)),
                  pl.BlockSpec((D_MODEL, tn), lambda j, i: (0, j))],
        out_specs=pl.BlockSpec((tm, tn), lambda j, i: (i, j)),
        out_shape=jax.ShapeDtypeStruct((N_TOK, PACK_COLS), F32),
        compiler_params=_cparams(("parallel", "parallel")),
        name="in_proj",
    )(hb, w_pack)


def _rope128(x, c, s1, s2):
    return x * c + pltpu.roll(x, LANES - 16, 1) * s1 + pltpu.roll(x, 16, 1) * s2


def _q_up_kernel(ql_ref, g_ref, w_ref, c_ref, s1_ref, s2_ref, q_ref):
    y = _rms_norm(ql_ref[...], g_ref[...]).astype(BF16)
    q = _dot(y, w_ref[...])
    c, s1, s2 = c_ref[...], s1_ref[...], s2_ref[...]
    scale = (QK_NOPE + QK_ROPE) ** -0.5 * math.log2(math.e)
    for h in range(MLA_HEADS):
        qh = _rope128(q[:, h * HEAD_PAD:(h + 1) * HEAD_PAD], c, s1, s2)
        q_ref[h] = (qh * scale).astype(BF16)


def _q_up(proj, g, wq, tc, ts1, ts2):
    tm = 512
    spt = SEQ // tm
    tab = pl.BlockSpec((tm, LANES), lambda i: (i, 0))
    return pl.pallas_call(
        _q_up_kernel,
        grid=(N_TOK // tm,),
        in_specs=[pl.BlockSpec((tm, Q_RANK), lambda i: (i, COL_QLAT // Q_RANK)),
                  pl.BlockSpec((1, Q_RANK), lambda i: (0, 0)),
                  pl.BlockSpec((Q_RANK, MLA_HEADS * HEAD_PAD), lambda i: (0, 0)),
                  tab, tab, tab],
        out_specs=pl.BlockSpec((None, MLA_HEADS, tm, HEAD_PAD), lambda i: (i // spt, 0, i % spt, 0)),
        out_shape=jax.ShapeDtypeStruct((BATCH, MLA_HEADS, SEQ, HEAD_PAD), BF16),
        compiler_params=_cparams(("parallel",)),
        name="q_up",
    )(proj, g, wq, tc, ts1, ts2)


def _kv_up_kernel(ckv_ref, small_ref, g_ref, wk_ref, wv_ref, c_ref, s1_ref, s2_ref, k_ref, v_ref):
    y = _rms_norm(ckv_ref[...], g_ref[...]).astype(BF16)
    k_all = _dot(y, wk_ref[...])
    v_all = _dot(y, wv_ref[...])
    lane = lax.broadcasted_iota(I32, small_ref.shape, 1)
    kpe_raw = jnp.where((lane >= KPE_LANE) & (lane < KPE_LANE + QK_ROPE), small_ref[...], 0.0)
    kpe = _rope128(kpe_raw, c_ref[...], s1_ref[...], s2_ref[...])
    for h in range(MLA_HEADS):
        k_ref[h] = (k_all[:, h * HEAD_PAD:(h + 1) * HEAD_PAD] + kpe).astype(BF16)
    ones = jnp.ones((v_all.shape[0], LANES), F32)
    for j in range(N_PAIRS):
        v_ref[j] = jnp.concatenate([v_all[:, j * LANES:(j + 1) * LANES], ones], axis=1).astype(BF16)


def _kv_up(proj, g, wk, wv, tc, ts1, ts2):
    tm = 512
    spt = SEQ // tm
    tab = pl.BlockSpec((tm, LANES), lambda i: (i, 0))
    return pl.pallas_call(
        _kv_up_kernel,
        grid=(N_TOK // tm,),
        in_specs=[pl.BlockSpec((tm, KV_RANK), lambda i: (i, COL_CKV // KV_RANK)),
                  pl.BlockSpec((tm, LANES), lambda i: (i, COL_SMALL // LANES)),
                  pl.BlockSpec((1, KV_RANK), lambda i: (0, 0)),
                  pl.BlockSpec((KV_RANK, MLA_HEADS * HEAD_PAD), lambda i: (0, 0)),
                  pl.BlockSpec((KV_RANK, MLA_HEADS * V_DIM), lambda i: (0, 0)),
                  tab, tab, tab],
        out_specs=[pl.BlockSpec((None, MLA_HEADS, tm, HEAD_PAD), lambda i: (i // spt, 0, i % spt, 0)),
                   pl.BlockSpec((None, N_PAIRS, tm, V_SLAB), lambda i: (i // spt, 0, i % spt, 0))],
        out_shape=[jax.ShapeDtypeStruct((BATCH, MLA_HEADS, SEQ, HEAD_PAD), BF16),
                   jax.ShapeDtypeStruct((BATCH, N_PAIRS, SEQ, V_SLAB), BF16)],
        compiler_params=_cparams(("parallel",)),
        name="kv_up",
    )(proj, proj, g, wk, wv, tc, ts1, ts2)


ATT_T = 512


def _attn_kernel(q_ref, k_ref, v_ref, o_ref, m_ref, acc_ref):
    qi = pl.program_id(2)
    t = ATT_T
    m_ref[...] = jnp.full(m_ref.shape, -jnp.inf, F32)
    acc_ref[...] = jnp.zeros(acc_ref.shape, F32)

    def kv_tile(ks, width, masked):
        v = v_ref[pl.ds(ks, width), :]
        for hh in range(2):
            s = lax.dot_general(q_ref[hh], k_ref[hh, pl.ds(ks, width), :], (((1,), (1,)), ((), ())),
                                preferred_element_type=F32)
            if masked:
                row = lax.broadcasted_iota(I32, s.shape, 0)
                col = lax.broadcasted_iota(I32, s.shape, 1)
                s = jnp.where(ks + col <= qi * t + row, s, NEG)
            m_prev = m_ref[hh]
            m_new = jnp.maximum(m_prev, jnp.max(s, axis=-1, keepdims=True))
            alpha = jnp.exp2(m_prev - m_new)
            p = jnp.exp2(s - jnp.concatenate([m_new] * (width // LANES), axis=1))
            acc_ref[hh] = jnp.concatenate([alpha, alpha], axis=1) * acc_ref[hh] + _dot(p.astype(BF16), v)
            m_ref[hh] = m_new

    def body(kp, carry):
        kv_tile(pl.multiple_of(kp * 2 * t, 2 * t), 2 * t, False)
        return carry

    lax.fori_loop(0, qi // 2, body, 0)

    @pl.when(qi % 2 == 0)
    def _():
        kv_tile(pl.multiple_of(qi * t, t), t, True)

    @pl.when(qi % 2 == 1)
    def _():
        kv_tile(pl.multiple_of((qi - 1) * t, t), 2 * t, True)

    lane = lax.broadcasted_iota(I32, o_ref.shape, 1)
    a0 = acc_ref[0]
    a1 = acc_ref[1]
    o0 = a0[:, :LANES] / a0[:, LANES:]
    o1 = a1[:, :LANES] / a1[:, LANES:]
    o_ref[...] = jnp.where(lane < V_DIM, o0, o1).astype(o_ref.dtype)


def _attention(q, k, v):
    t = ATT_T
    return pl.pallas_call(
        _attn_kernel,
        grid=(BATCH, N_PAIRS, SEQ // t),
        in_specs=[pl.BlockSpec((None, 2, t, HEAD_PAD), lambda b, j, qi: (b, j, qi, 0)),
                  pl.BlockSpec((None, 2, SEQ, HEAD_PAD), lambda b, j, qi: (b, j, 0, 0)),
                  pl.BlockSpec((None, None, SEQ, V_SLAB), lambda b, j, qi: (b, j, 0, 0))],
        out_specs=pl.BlockSpec((None, t, LANES), lambda b, j, qi: (b, qi, j)),
        out_shape=jax.ShapeDtypeStruct((BATCH, SEQ, MLA_HEADS * V_DIM), BF16),
        scratch_shapes=[pltpu.VMEM((2, t, LANES), F32), pltpu.VMEM((2, t, V_SLAB), F32)],
        compiler_params=_cparams(("parallel", "parallel", "parallel")),
        name="mla_attention",
    )(q, k, v)


HALO = SUBLANES
HEADS_PER_GROUP = SSM_HEADS // SSM_GROUPS
GROUP_W = SSM_INNER // SSM_GROUPS


def _ssd_kernel(xbc_ref, halo_ref, z_ref, small_ref, cw_ref, cb_ref, dtb_ref, alog_ref, dexp_ref,
                nw_ref, e_ref, y_ref, st_ref, cat_ref):
    c = pl.program_id(1)
    L = SSM_CHUNK

    @pl.when(c == 0)
    def _():
        st_ref[...] = jnp.zeros(st_ref.shape, F32)

    cat_ref[0:HALO, :] = jnp.where(c == 0, 0.0, halo_ref[...])
    cat_ref[HALO:HALO + L, :] = xbc_ref[...]
    acc = jnp.broadcast_to(cb_ref[...], (L, SSM_CONV_DIM))
    for kk in range(SSM_CONV):
        off = HALO - (SSM_CONV - 1) + kk
        acc = acc + cw_ref[kk:kk + 1, :] * cat_ref[off:off + L, :]
    xc = _silu(acc)
    xs = xc[:, :SSM_INNER]
    bm = xc[:, SSM_INNER:SSM_INNER + SSM_GROUPS * SSM_STATE]
    cm = xc[:, SSM_INNER + SSM_GROUPS * SSM_STATE:]

    lane = lax.broadcasted_iota(I32, (L, LANES), 1)
    dt_lanes = (lane >= DT_LANE) & (lane < DT_LANE + SSM_HEADS)
    dt_in = small_ref[...] + dtb_ref[...]
    dt = jnp.maximum(dt_in, 0.0) + jnp.log1p(jnp.exp(-jnp.abs(dt_in)))
    dt = jnp.where(dt_lanes, dt, 0.0)
    a = -jnp.exp(alog_ref[...])
    a_dt = dt * a
    row = lax.broadcasted_iota(I32, (L, L), 0)
    col = lax.broadcasted_iota(I32, (L, L), 1)
    tri = row >= col
    a_cum = jnp.dot(tri.astype(F32), a_dt, precision=HIGHEST, preferred_element_type=F32)
    a_last = a_cum[L - 1:L, :]
    t = jnp.concatenate([dt, jnp.exp(a_cum), jnp.exp(a_last - a_cum)], axis=0)
    t_hi = t.astype(BF16)
    r1 = t - t_hi.astype(F32)
    t_mid = r1.astype(BF16)
    t_lo = (r1 - t_mid.astype(F32)).astype(BF16)
    expanded = _dot(jnp.concatenate([t_hi, t_mid, t_lo], axis=1), e_ref[...])
    x_dt = xs * expanded[0:L]
    eac_x = expanded[L:2 * L]
    ds_x = expanded[2 * L:3 * L]
    cd_x = eac_x[L - 1:L, :]
    xd = (x_dt * ds_x).astype(BF16)
    x_bf = x_dt.astype(BF16)
    a_cum_t = a_cum.T
    lane_h = lax.broadcasted_iota(I32, (L, LANES), 1)

    y_parts = []
    for g in range(SSM_GROUPS):
        bg = bm[:, g * SSM_STATE:(g + 1) * SSM_STATE].astype(BF16)
        cg = cm[:, g * SSM_STATE:(g + 1) * SSM_STATE].astype(BF16)
        cb = lax.dot_general(cg, bg, (((1,), (1,)), ((), ())), preferred_element_type=F32)
        st_prev = st_ref[g]
        y_off = _dot(cg, st_prev.astype(BF16)) * eac_x[:, g * GROUP_W:(g + 1) * GROUP_W]
        for jp in range(HEADS_PER_GROUP // 2):
            pair = g * (HEADS_PER_GROUP // 2) + jp
            xp = x_bf[:, pair * LANES:(pair + 1) * LANES]
            outs = []
            for hh in range(2):
                hl = DT_LANE + 2 * pair + hh
                seg = a_cum[:, hl:hl + 1] - a_cum_t[hl:hl + 1, :]
                decay = jnp.where(tri, jnp.exp(seg), 0.0)
                outs.append(_dot((cb * decay).astype(BF16), xp))
            y_parts.append(jnp.where(lane_h < SSM_HEADDIM, outs[0], outs[1])
                           + y_off[:, jp * LANES:(jp + 1) * LANES])
        upd = lax.dot_general(bg, xd[:, g * GROUP_W:(g + 1) * GROUP_W], (((0,), (0,)), ((), ())),
                              preferred_element_type=F32)
        st_ref[g] = st_prev * cd_x[:, g * GROUP_W:(g + 1) * GROUP_W] + upd

    y = jnp.concatenate(y_parts, axis=1) + xs * dexp_ref[...]
    y = y * _silu(z_ref[...])
    normed = []
    for g in range(SSM_GROUPS):
        yg = y[:, g * GROUP_W:(g + 1) * GROUP_W]
        normed.append(yg * lax.rsqrt(jnp.mean(yg * yg, axis=-1, keepdims=True) + RMS_EPS))
    y_ref[...] = (jnp.concatenate(normed, axis=1) * nw_ref[...]).astype(y_ref.dtype)


def _ssd(proj, cw, cb, dtb, alog, dexp, nw, e_mat):
    L = SSM_CHUNK
    nc = SEQ // L
    xbc_blk = COL_XBC // SSM_CONV_DIM

    def row1(w):
        return pl.BlockSpec((1, w), lambda b, c: (0, 0))

    return pl.pallas_call(
        _ssd_kernel,
        grid=(BATCH, nc),
        in_specs=[pl.BlockSpec((L, SSM_CONV_DIM), lambda b, c: (b * nc + c, xbc_blk)),
                  pl.BlockSpec((HALO, SSM_CONV_DIM),
                               lambda b, c: (jnp.maximum((b * nc + c) * (L // HALO) - 1, 0), xbc_blk)),
                  pl.BlockSpec((L, SSM_INNER), lambda b, c: (b * nc + c, COL_Z // SSM_INNER)),
                  pl.BlockSpec((L, LANES), lambda b, c: (b * nc + c, COL_SMALL // LANES)),
                  pl.BlockSpec((SSM_CONV, SSM_CONV_DIM), lambda b, c: (0, 0)),
                  row1(SSM_CONV_DIM), row1(LANES), row1(LANES), row1(SSM_INNER), row1(SSM_INNER),
                  pl.BlockSpec((3 * LANES, SSM_INNER), lambda b, c: (0, 0))],
        out_specs=pl.BlockSpec((L, SSM_INNER), lambda b, c: (b * nc + c, 0)),
        out_shape=jax.ShapeDtypeStruct((N_TOK, SSM_INNER), BF16),
        scratch_shapes=[pltpu.VMEM((SSM_GROUPS, SSM_STATE, GROUP_W), F32),
                        pltpu.VMEM((HALO + L, SSM_CONV_DIM), F32)],
        compiler_params=_cparams(("parallel", "arbitrary")),
        name="mamba2_ssd",
    )(proj, proj, proj, proj, cw, cb, dtb, alog, dexp, nw, e_mat)


def _merge_kernel(attn_ref, ssm_ref, ga_ref, gb_ref, h_ref, wa_ref, ws_ref, wo_ref, g_ref, b_ref,
                  hf_ref, hb_ref):
    ya = _dot(attn_ref[...], wa_ref[...])
    yb = _dot(ssm_ref[...], ws_ref[...])
    mix = _sigmoid(ga_ref[...]) * ya + _sigmoid(gb_ref[...]) * yb
    mixed = _dot(mix.astype(BF16), wo_ref[...])
    h1 = _layer_norm(ALPHA * h_ref[...] + mixed, g_ref[...], b_ref[...])
    hf_ref[...] = h1
    hb_ref[...] = h1.astype(BF16)


def _merge(attn, ssm_y, proj, hf, wa, ws, wo, g, b):
    tm = 256
    full = lambda r, c: pl.BlockSpec((r, c), lambda i: (0, 0))
    tile = lambda c, j=0: pl.BlockSpec((tm, c), lambda i: (i, j))
    return pl.pallas_call(
        _merge_kernel,
        grid=(N_TOK // tm,),
        in_specs=[tile(MLA_HEADS * V_DIM), tile(SSM_INNER), tile(D_MODEL, COL_GATE_A // D_MODEL),
                  tile(D_MODEL, COL_GATE_B // D_MODEL), tile(D_MODEL),
                  full(MLA_HEADS * V_DIM, D_MODEL), full(SSM_INNER, D_MODEL), full(D_MODEL, D_MODEL),
                  full(1, D_MODEL), full(1, D_MODEL)],
        out_specs=[tile(D_MODEL), tile(D_MODEL)],
        out_shape=[jax.ShapeDtypeStruct((N_TOK, D_MODEL), F32),
                   jax.ShapeDtypeStruct((N_TOK, D_MODEL), BF16)],
        compiler_params=_cparams(("parallel",)),
        name="merge_ln1",
    )(attn, ssm_y, proj, proj, hf, wa, ws, wo, g, b)


def _first_argmax(vals, lane):
    m = jnp.max(vals, axis=-1, keepdims=True)
    idx = jnp.min(jnp.where(vals == m, lane, float(N_EXPERTS)), axis=-1, keepdims=True)
    return m, idx


def _post_kernel(hf_ref, hb_ref, p_ref, wr_ref, rb_ref, wsg_ref, wsu_ref, wsd_ref, wp_ref, wpg_ref,
                 base_ref, idx_ref, wt_ref, cnt_ref):
    hf = hf_ref[...]
    hb = hb_ref[...]
    tm = hf.shape[0]
    logits = jnp.dot(hf, wr_ref[...], precision=HIGHEST, preferred_element_type=F32)
    scores = _sigmoid(logits)
    sel = scores + rb_ref[...]
    lane_i = lax.broadcasted_iota(I32, (tm, N_EXPERTS), 1)
    grp = lane_i // PER_GROUP
    lane = lane_i.astype(F32)

    grp_scores = []
    for g in range(N_GROUPS):
        vals = jnp.where(grp == g, sel, -jnp.inf)
        m1, i1 = _first_argmax(vals, lane)
        m2 = jnp.max(jnp.where(lane == i1, -jnp.inf, vals), axis=-1, keepdims=True)
        grp_scores.append(m1 + m2)
    keep = jnp.zeros((tm, N_EXPERTS), jnp.bool_)
    for g in range(N_GROUPS):
        rank = jnp.zeros((tm, 1), I32)
        for o in range(N_GROUPS):
            if o == g:
                continue
            ahead = (grp_scores[o] > grp_scores[g]) if o > g else (grp_scores[o] >= grp_scores[g])
            rank = rank + ahead.astype(I32)
        keep = keep | ((grp == g) & (rank < TOPK_GROUPS))
    masked = jnp.where(keep, sel, -jnp.inf)

    lane_k = lax.broadcasted_iota(I32, (tm, TOP_K), 1)
    idx_out = jnp.zeros((tm, TOP_K), I32)
    w_out = jnp.zeros((tm, TOP_K), F32)
    chosen = jnp.zeros((tm, N_EXPERTS), F32)
    for kk in range(TOP_K):
        _, ik = _first_argmax(masked, lane)
        hit = lane == ik
        wk = jnp.sum(jnp.where(hit, scores, 0.0), axis=-1, keepdims=True)
        masked = jnp.where(hit, -jnp.inf, masked)
        chosen = jnp.where(hit, 1.0, chosen)
        idx_out = jnp.where(lane_k == kk, ik.astype(I32), idx_out)
        w_out = jnp.where(lane_k == kk, wk, w_out)
    w_out = w_out / jnp.sum(w_out, axis=-1, keepdims=True) * ROUTED_SCALE
    idx_ref[...] = idx_out
    wt_ref[...] = w_out
    cnt_ref[...] = jnp.sum(chosen, axis=0, keepdims=True)

    shared = _dot((_silu(_dot(hb, wsg_ref[...])) * _dot(hb, wsu_ref[...])).astype(BF16), wsd_ref[...])
    ple = _dot(p_ref[...].astype(BF16), wp_ref[...]) * _sigmoid(_dot(hb, wpg_ref[...]))
    base_ref[...] = ALPHA * hf + shared + ple


def _post(hf, hb, p, wr, rb, wsg, wsu, wsd, wp, wpg):
    tm = ROUTE_T
    full = lambda r, c: pl.BlockSpec((r, c), lambda i: (0, 0))
    tile = lambda c: pl.BlockSpec((tm, c), lambda i: (i, 0))
    return pl.pallas_call(
        _post_kernel,
        grid=(N_RTILES,),
        in_specs=[tile(D_MODEL), tile(D_MODEL), tile(PLE_DIM),
                  full(D_MODEL, N_EXPERTS), full(1, N_EXPERTS),
                  full(D_MODEL, EXPERT_FF), full(D_MODEL, EXPERT_FF), full(EXPERT_FF, D_MODEL),
                  full(PLE_DIM, D_MODEL), full(D_MODEL, D_MODEL)],
        out_specs=[tile(D_MODEL), tile(TOP_K), tile(TOP_K),
                   pl.BlockSpec((None, 1, N_EXPERTS), lambda i: (i, 0, 0))],
        out_shape=[jax.ShapeDtypeStruct((N_TOK, D_MODEL), F32),
                   jax.ShapeDtypeStruct((N_TOK, TOP_K), I32),
                   jax.ShapeDtypeStruct((N_TOK, TOP_K), F32),
                   jax.ShapeDtypeStruct((N_RTILES, 1, N_EXPERTS), F32)],
        compiler_params=_cparams(("parallel",)),
        name="router_shared_ple",
    )(hf, hb, p, wr, rb, wsg, wsu, wsd, wp, wpg)


def _pos_kernel(idx_ref, base_ref, pos_ref):
    idx = idx_ref[...]
    tm = idx.shape[0]
    lane = lax.broadcasted_iota(I32, (tm, N_EXPERTS), 1)
    hits = [lane == idx[:, kk:kk + 1] for kk in range(TOP_K)]
    chosen = hits[0]
    for kk in range(1, TOP_K):
        chosen = chosen | hits[kk]
    r = lax.broadcasted_iota(I32, (tm, tm), 0)
    c = lax.broadcasted_iota(I32, (tm, tm), 1)
    earlier = jnp.where(c < r, 1.0, 0.0).astype(BF16)
    rank = _dot(earlier, jnp.where(chosen, 1.0, 0.0).astype(BF16))
    slot = rank + base_ref[...]
    lane_k = lax.broadcasted_iota(I32, (tm, TOP_K), 1)
    out = jnp.zeros((tm, TOP_K), I32)
    for kk in range(TOP_K):
        pk = jnp.sum(jnp.where(hits[kk], slot, 0.0), axis=-1, keepdims=True)
        out = jnp.where(lane_k == kk, pk.astype(I32), out)
    pos_ref[...] = out


def _slot_positions(idx, base):
    tm = ROUTE_T
    return pl.pallas_call(
        _pos_kernel,
        grid=(N_RTILES,),
        in_specs=[pl.BlockSpec((tm, TOP_K), lambda i: (i, 0)),
                  pl.BlockSpec((None, 1, N_EXPERTS), lambda i: (i, 0, 0))],
        out_specs=pl.BlockSpec((tm, TOP_K), lambda i: (i, 0)),
        out_shape=jax.ShapeDtypeStruct((N_TOK, TOP_K), I32),
        compiler_params=_cparams(("parallel",)),
        name="slot_positions",
    )(idx, base)


def _slot_layout(cnt):
    counts_te = cnt.reshape(N_RTILES, N_EXPERTS).astype(I32)
    counts = jnp.sum(counts_te, axis=0)
    padded = (counts + SLOT_BLOCK - 1) // SLOT_BLOCK * SLOT_BLOCK
    pad_end = jnp.cumsum(padded)
    pad_start = pad_end - padded
    tile_off = jnp.cumsum(counts_te, axis=0) - counts_te
    base = (pad_start[None, :] + tile_off).astype(F32).reshape(N_RTILES, 1, N_EXPERTS)
    n_used = (pad_end[-1] // SLOT_BLOCK).astype(I32).reshape(1)
    return base, counts, pad_start, padded, n_used


DISP_T = 256


def _dispatch_kernel(cnt_ref, ps_ref, pd_ref, nu_ref, pos_hbm, h3_ref, x_hbm,
                     pos_smem, zbuf, pos_sem, row_sem, pad_sem):
    i = pl.program_id(0)
    n = pl.num_programs(0)

    def pos_copy(blk, slot):
        return pltpu.make_async_copy(pos_hbm.at[blk], pos_smem.at[slot], pos_sem.at[slot])

    def pad_row(e, r):
        return pltpu.make_async_copy(zbuf.at[0], x_hbm.at[ps_ref[e] + r], pad_sem)

    def pad_block(b):
        return pltpu.make_async_copy(zbuf, x_hbm.at[pl.ds(b * SLOT_BLOCK, SLOT_BLOCK)], pad_sem)

    def for_each_pad(fn_row, fn_block):
        def per_expert(e, carry):
            def per_row(r, c2):
                fn_row(e, r)
                return c2
            return lax.fori_loop(cnt_ref[e], pd_ref[e], per_row, carry)
        lax.fori_loop(0, N_EXPERTS, per_expert, 0)

        def per_block(b, carry):
            fn_block(b)
            return carry
        lax.fori_loop(nu_ref[0], N_BLOCKS, per_block, 0)

    @pl.when(i == 0)
    def _():
        pos_copy(0, 0).start()
        pos_copy(1, 1).start()
        zbuf[...] = jnp.zeros(zbuf.shape, F32)
        for_each_pad(lambda e, r: pad_row(e, r).start(), lambda b: pad_block(b).start())
        for_each_pad(lambda e, r: pad_row(e, r).wait(), lambda b: pad_block(b).wait())

    slot = i % 2
    pos_copy(i, slot).wait()

    def body(t, carry):
        for kk in range(TOP_K):
            dst = pos_smem[slot, t * TOP_K + kk]
            pltpu.make_async_copy(h3_ref.at[t], x_hbm.at[dst], row_sem).start(priority=kk % 2)
        return carry
    lax.fori_loop(0, DISP_T, body, 0)

    @pl.when(i + 2 < n)
    def _():
        pos_copy(i + 2, slot).start()

    for kk in range(TOP_K):
        pltpu.make_async_copy(h3_ref, x_hbm.at[pl.ds(0, DISP_T)], row_sem).wait()


def _dispatch(counts, pad_start, padded, n_used, pos, h3):
    n_tiles = N_TOK // DISP_T
    grid_spec = pltpu.PrefetchScalarGridSpec(
        num_scalar_prefetch=4,
        grid=(n_tiles,),
        in_specs=[pl.BlockSpec(memory_space=pl.ANY),
                  pl.BlockSpec((DISP_T, D_CHUNKS, LANES), lambda i, *_: (i, 0, 0))],
        out_specs=pl.BlockSpec(memory_space=pl.ANY),
        scratch_shapes=[pltpu.SMEM((2, DISP_T * TOP_K), I32),
                        pltpu.VMEM((SLOT_BLOCK, D_CHUNKS, LANES), F32),
                        pltpu.SemaphoreType.DMA((2,)),
                        pltpu.SemaphoreType.DMA,
                        pltpu.SemaphoreType.DMA])
    return pl.pallas_call(
        _dispatch_kernel,
        grid_spec=grid_spec,
        out_shape=jax.ShapeDtypeStruct((CAP, D_CHUNKS, LANES), F32),
        compiler_params=_cparams(("arbitrary",)),
        name="dispatch_scatter",
    )(counts, pad_start, padded, n_used, pos.reshape(n_tiles, DISP_T * TOP_K), h3)


def _moe_kernel(fb_ref, nb_ref, nu_ref, x_hbm, wg_ref, wu_ref, wd_ref, y_hbm,
                xbuf, ybuf, wg_bf, wu_bf, wd_bf, in_sem, out_sem):
    e = pl.program_id(0)
    first = fb_ref[e]
    nb = nb_ref[e]
    n_used = nu_ref[0]

    def rows(g):
        return pl.ds(pl.multiple_of(g * SLOT_BLOCK, SLOT_BLOCK), SLOT_BLOCK)

    def x_copy(g):
        slot = g % MOE_RING
        return pltpu.make_async_copy(x_hbm.at[rows(g)], xbuf.at[slot], in_sem.at[slot])

    def y_copy(g):
        slot = g % MOE_RING
        return pltpu.make_async_copy(ybuf.at[slot], y_hbm.at[rows(g)], out_sem.at[slot])

    @pl.when(e == 0)
    def _():
        for g in range(MOE_RING - 1):
            x_copy(g).start()

    @pl.when(nb > 0)
    def _():
        wg_bf[...] = wg_ref[...].astype(BF16)
        wu_bf[...] = wu_ref[...].astype(BF16)
        wd_bf[...] = wd_ref[...].astype(BF16)

        def body(g, carry):
            slot = g % MOE_RING
            x_copy(g).wait()

            @pl.when(g + MOE_RING - 1 < n_used)
            def _():
                x_copy(g + MOE_RING - 1).start()

            x = _rows_to_matrix(xbuf[slot]).astype(BF16)
            hid = (_silu(_dot(x, wg_bf[...])) * _dot(x, wu_bf[...])).astype(BF16)
            y3 = _matrix_to_rows(_dot(hid, wd_bf[...]))

            @pl.when(g >= MOE_RING)
            def _():
                y_copy(g - MOE_RING).wait()

            ybuf[slot] = y3
            y_copy(g).start()
            return carry

        lax.fori_loop(first, first + nb, body, 0)

    @pl.when(e == N_EXPERTS - 1)
    def _():
        for j in range(MOE_RING):
            y_copy(n_used - 1 - j).wait()
        ybuf[0] = jnp.zeros(ybuf.shape[1:], F32)

        def tail_copy(b):
            return pltpu.make_async_copy(
                ybuf.at[0], y_hbm.at[pl.ds(pl.multiple_of(b * SLOT_BLOCK, SLOT_BLOCK), SLOT_BLOCK)],
                out_sem.at[0])

        def start(b, carry):
            tail_copy(b).start()
            return carry

        def wait(b, carry):
            tail_copy(b).wait()
            return carry

        lax.fori_loop(nu_ref[0], N_BLOCKS, start, 0)
        lax.fori_loop(nu_ref[0], N_BLOCKS, wait, 0)


def _moe(first_block, n_blocks, n_used, xs, wg, wu, wd, layer):
    wspec = lambda r, c: pl.BlockSpec((None, None, r, c), lambda e, *_: (layer, e, 0, 0))
    grid_spec = pltpu.PrefetchScalarGridSpec(
        num_scalar_prefetch=3,
        grid=(N_EXPERTS,),
        in_specs=[pl.BlockSpec(memory_space=pl.ANY),
                  wspec(D_MODEL, EXPERT_FF), wspec(D_MODEL, EXPERT_FF), wspec(EXPERT_FF, D_MODEL)],
        out_specs=pl.BlockSpec(memory_space=pl.ANY),
        scratch_shapes=[pltpu.VMEM((MOE_RING, SLOT_BLOCK, D_CHUNKS, LANES), F32),
                        pltpu.VMEM((MOE_RING, SLOT_BLOCK, D_CHUNKS, LANES), F32),
                        pltpu.VMEM((D_MODEL, EXPERT_FF), BF16),
                        pltpu.VMEM((D_MODEL, EXPERT_FF), BF16),
                        pltpu.VMEM((EXPERT_FF, D_MODEL), BF16),
                        pltpu.SemaphoreType.DMA((MOE_RING,)),
                        pltpu.SemaphoreType.DMA((MOE_RING,))])
    return pl.pallas_call(
        _moe_kernel,
        grid_spec=grid_spec,
        out_shape=jax.ShapeDtypeStruct((CAP, D_CHUNKS, LANES), F32),
        compiler_params=_cparams(("arbitrary",)),
        name="routed_experts",
    )(first_block, n_blocks, n_used, xs, wg, wu, wd)


COMB_T = 128


def _combine_kernel(pos_hbm, y_hbm, w_ref, base_ref, g_ref, b_ref, hf_ref, hb_ref,
                    pos_smem, buf, routed_ref, pos_sem, row_sem):
    i = pl.program_id(0)
    n = pl.num_programs(0)
    slot = i % 2
    nslot = (i + 1) % 2
    n_groups = COMB_T // SUBLANES

    def pos_copy(blk, s):
        return pltpu.make_async_copy(pos_hbm.at[blk], pos_smem.at[s], pos_sem.at[s])

    def issue_group(s, g):
        for j in range(SUBLANES):
            t = g * SUBLANES + j
            for kk in range(TOP_K):
                src = pos_smem[s, t * TOP_K + kk]
                pltpu.make_async_copy(y_hbm.at[src], buf.at[s, t, kk],
                                      row_sem.at[s]).start(priority=kk % 2)

    def reduce_group(s, g):
        rows8 = pl.ds(pl.multiple_of(g * SUBLANES, SUBLANES), SUBLANES)
        acc = None
        for kk in range(TOP_K):
            term = w_ref[rows8, kk:kk + 1] * _rows_to_matrix(buf[s, rows8, kk])
            acc = term if acc is None else acc + term
        routed_ref[rows8, :] = acc

    def group_loop(fn):
        def body(g, carry):
            fn(g)
            return carry
        lax.fori_loop(0, n_groups, body, 0)

    @pl.when(i == 0)
    def _():
        pos_copy(0, 0).start()
        pos_copy(1, 1).start()
        pos_copy(0, 0).wait()
        group_loop(lambda g: issue_group(0, g))

    for kk in range(TOP_K):
        pltpu.make_async_copy(y_hbm.at[pl.ds(0, COMB_T)], buf.at[slot, :, kk], row_sem.at[slot]).wait()

    @pl.when(i + 1 < n)
    def _():
        pos_copy(i + 1, nslot).wait()

        def both(g):
            issue_group(nslot, g)
            reduce_group(slot, g)
        group_loop(both)

    @pl.when(i + 1 == n)
    def _():
        group_loop(lambda g: reduce_group(slot, g))

    @pl.when(i + 2 < n)
    def _():
        pos_copy(i + 2, slot).start()

    h2 = _layer_norm(base_ref[...] + routed_ref[...], g_ref[...], b_ref[...])
    hf_ref[...] = h2
    hb_ref[...] = h2.astype(BF16)


def _combine(pos, y3, wts, base, g, b):
    tm = COMB_T
    n_tiles = N_TOK // tm
    full = lambda r, c: pl.BlockSpec((r, c), lambda i: (0, 0))
    tile = lambda c: pl.BlockSpec((tm, c), lambda i: (i, 0))
    return pl.pallas_call(
        _combine_kernel,
        grid=(n_tiles,),
        in_specs=[pl.BlockSpec(memory_space=pl.ANY), pl.BlockSpec(memory_space=pl.ANY),
                  tile(TOP_K), tile(D_MODEL), full(1, D_MODEL), full(1, D_MODEL)],
        out_specs=[tile(D_MODEL), tile(D_MODEL)],
        out_shape=[jax.ShapeDtypeStruct((N_TOK, D_MODEL), F32),
                   jax.ShapeDtypeStruct((N_TOK, D_MODEL), BF16)],
        scratch_shapes=[pltpu.SMEM((2, tm * TOP_K), I32),
                        pltpu.VMEM((2, tm, TOP_K, D_CHUNKS, LANES), F32),
                        pltpu.VMEM((tm, D_MODEL), F32),
                        pltpu.SemaphoreType.DMA((2,)),
                        pltpu.SemaphoreType.DMA((2,))],
        compiler_params=_cparams(("arbitrary",)),
        name="combine_ln2",
    )(pos.reshape(n_tiles, tm * TOP_K), y3, wts, base, g, b)


def _pack_w_in(w):
    q_lat = w[:, 0:384]
    c_kv = w[:, 384:640]
    k_pe = w[:, 640:672]
    z = w[:, 672:1696]
    xbc = w[:, 1696:3232]
    dt = w[:, 3232:3248]
    gate_a = w[:, 3248:4272]
    gate_b = w[:, 4272:5296]
    zeros = lambda n: jnp.zeros((D_MODEL, n), w.dtype)
    small = jnp.concatenate([zeros(KPE_LANE), k_pe, dt, zeros(LANES - DT_LANE - SSM_HEADS)], axis=1)
    return jnp.concatenate([gate_a, gate_b, z, xbc, q_lat, small, c_kv], axis=1).astype(BF16)


def _pack_w_q(w):
    w = w.reshape(Q_RANK, MLA_HEADS, QK_NOPE + QK_ROPE)
    w = jnp.pad(w, ((0, 0), (0, 0), (0, HEAD_PAD - QK_NOPE - QK_ROPE)))
    return w.reshape(Q_RANK, MLA_HEADS * HEAD_PAD).astype(BF16)


def _pack_w_kv(w):
    w = w.reshape(KV_RANK, MLA_HEADS, QK_NOPE + V_DIM)
    wk = jnp.pad(w[:, :, :QK_NOPE], ((0, 0), (0, 0), (0, HEAD_PAD - QK_NOPE)))
    wv = w[:, :, QK_NOPE:]
    return (wk.reshape(KV_RANK, MLA_HEADS * HEAD_PAD).astype(BF16),
            wv.reshape(KV_RANK, MLA_HEADS * V_DIM).astype(BF16))


def _rope_tables(positions):
    half = QK_ROPE // 2
    inv_freq = jnp.power(ROPE_THETA, -jnp.arange(half, dtype=F32) * (2.0 / QK_ROPE))
    ang = positions.astype(F32).reshape(N_TOK, 1) * inv_freq
    cos, sin = jnp.cos(ang), jnp.sin(ang)
    z = lambda n: jnp.zeros((N_TOK, n), F32)
    tail = HEAD_PAD - QK_NOPE - QK_ROPE
    tc = jnp.concatenate([jnp.ones((N_TOK, QK_NOPE), F32), cos, cos, z(tail)], axis=1)
    ts1 = jnp.concatenate([z(QK_NOPE), -sin, z(half), z(tail)], axis=1)
    ts2 = jnp.concatenate([z(QK_NOPE), z(half), sin, z(tail)], axis=1)
    return tc, ts1, ts2


def _head_lane_row(v):
    return jnp.zeros((1, LANES), F32).at[0, DT_LANE:DT_LANE + SSM_HEADS].set(v.astype(F32))


def _expand_matrix():
    r = jnp.arange(LANES)[:, None]
    c = jnp.arange(SSM_INNER)[None, :]
    e = ((r - DT_LANE) == (c // SSM_HEADDIM)).astype(BF16)
    return jnp.concatenate([e, e, e], axis=0)


def kernel(x, p, positions, w_in, q_norm, w_q_up, kv_norm, w_kv_up, conv_w, conv_b, dt_bias, a_log, d_skip,
           ssm_norm, w_attn_br, w_ssm_br, w_o, ln1_g, ln1_b, w_router, router_bias, w_exp_gate, w_exp_up,
           w_exp_down, w_sh_gate, w_sh_up, w_sh_down, w_ple, w_ple_gate, ln2_g, ln2_b):
    tc, ts1, ts2 = _rope_tables(positions)
    e_mat = _expand_matrix()
    hf = x.reshape(N_TOK, D_MODEL)
    hb = hf.astype(BF16)
    row = lambda v: v.reshape(1, -1).astype(F32)
    for i in range(DEPTH):
        proj = _in_proj(hb, _pack_w_in(w_in[i]))
        q = _q_up(proj, row(q_norm[i]), _pack_w_q(w_q_up[i]), tc, ts1, ts2)
        wk, wv = _pack_w_kv(w_kv_up[i])
        k, v = _kv_up(proj, row(kv_norm[i]), wk, wv, tc, ts1, ts2)
        attn = _attention(q, k, v).reshape(N_TOK, MLA_HEADS * V_DIM)
        ssm_y = _ssd(proj, conv_w[i], row(conv_b[i]), _head_lane_row(dt_bias[i]), _head_lane_row(a_log[i]),
                     row(jnp.repeat(d_skip[i], SSM_HEADDIM)), row(ssm_norm[i]), e_mat)
        hf, hb = _merge(attn, ssm_y, proj, hf, w_attn_br[i].astype(BF16), w_ssm_br[i].astype(BF16),
                        w_o[i].astype(BF16), row(ln1_g[i]), row(ln1_b[i]))
        base, idx, wts, cnt = _post(hf, hb, p[i].reshape(N_TOK, PLE_DIM), w_router[i], row(router_bias[i]),
                                    w_sh_gate[i].astype(BF16), w_sh_up[i].astype(BF16),
                                    w_sh_down[i].astype(BF16), w_ple[i].astype(BF16),
                                    w_ple_gate[i].astype(BF16))
        base_slots, counts, pad_start, padded, n_used = _slot_layout(cnt)
        pos = _slot_positions(idx, base_slots)
        xs = _dispatch(counts, pad_start, padded, n_used, pos, hf.reshape(N_TOK, D_CHUNKS, LANES))
        y3 = _moe(pad_start // SLOT_BLOCK, padded // SLOT_BLOCK, n_used, xs,
                  w_exp_gate, w_exp_up, w_exp_down, i)
        hf, hb = _combine(pos, y3, wts, base, row(ln2_g[i]), row(ln2_b[i]))
    return hf.reshape(BATCH, SEQ, D_MODEL)
```

```python
import functools
import math

import jax
import jax.numpy as jnp
from jax import lax
from jax.experimental import pallas as pl
from jax.experimental.pallas import tpu as pltpu

F32 = jnp.float32
BF16 = jnp.bfloat16
I32 = jnp.int32
HIGHEST = lax.Precision.HIGHEST

D_MODEL = 1024
BATCH = 4
SEQ = 8192
DEPTH = 2
N_TOK = BATCH * SEQ
MLA_HEADS = 8
QK_NOPE = 64
QK_ROPE = 32
V_DIM = 64
Q_RANK = 384
KV_RANK = 256
ROPE_THETA = 10000.0
SSM_INNER = 1024
SSM_HEADDIM = 64
SSM_HEADS = 16
SSM_GROUPS = 2
SSM_STATE = 128
SSM_CONV = 4
SSM_CHUNK = 128
SSM_CONV_DIM = 1536
N_EXPERTS = 256
TOP_K = 8
N_GROUPS = 8
TOPK_GROUPS = 4
PER_GROUP = N_EXPERTS // N_GROUPS
EXPERT_FF = 256
ROUTED_SCALE = 2.5
PLE_DIM = 256
ALPHA = (2 * DEPTH) ** 0.25
LN_EPS = 1e-5
RMS_EPS = 1e-6

LANES = 128
SUBLANES = 8
VMEM_LIMIT = 48 * 1024 * 1024

COL_GATE_A = 0
COL_GATE_B = 1024
COL_Z = 2048
COL_XBC = 3072
COL_QLAT = 4608
COL_SMALL = 4992
COL_CKV = 5120
PACK_COLS = 5376
KPE_LANE = 64
DT_LANE = 96

HEAD_PAD = 128
N_PAIRS = MLA_HEADS // 2
V_SLAB = 2 * LANES

ROUTE_T = 256
N_RTILES = N_TOK // ROUTE_T
SLOT_BLOCK = 128
N_ASSIGN = N_TOK * TOP_K
N_BLOCKS = N_ASSIGN // SLOT_BLOCK + N_EXPERTS
CAP = N_BLOCKS * SLOT_BLOCK
D_CHUNKS = D_MODEL // LANES
MOE_RING = 6

NEG = float(jnp.finfo(jnp.float32).min)


def _cparams(sem):
    return pltpu.CompilerParams(dimension_semantics=sem, vmem_limit_bytes=VMEM_LIMIT)


def _sigmoid(x):
    return 1.0 / (1.0 + jnp.exp(-x))


def _silu(x):
    return x * _sigmoid(x)


def _layer_norm(x, g, b):
    mu = jnp.mean(x, axis=-1, keepdims=True)
    xc = x - mu
    var = jnp.mean(xc * xc, axis=-1, keepdims=True)
    return xc * lax.rsqrt(var + LN_EPS) * g + b


def _rms_norm(x, g):
    return x * lax.rsqrt(jnp.mean(x * x, axis=-1, keepdims=True) + RMS_EPS) * g


def _dot(a, b):
    return jnp.dot(a, b, preferred_element_type=F32)


def _swap_sublane_major(x):
    groups = x.shape[0]
    for d in (4, 2, 1):
        y = x.reshape(groups, SUBLANES // (2 * d), 2, d, SUBLANES, LANES)
        lo, hi = y[:, :, 0], y[:, :, 1]
        shp = lo.shape
        keep = (lax.broadcasted_iota(I32, shp, 3) & d) == 0
        rot = lambda v, s: pltpu.roll(v.reshape(-1, SUBLANES, LANES), s, 1).reshape(shp)
        new_lo = jnp.where(keep, lo, rot(hi, d))
        new_hi = jnp.where(keep, rot(lo, SUBLANES - d), hi)
        x = jnp.stack([new_lo, new_hi], axis=2).reshape(groups, SUBLANES, SUBLANES, LANES)
    return x


def _rows_to_matrix(x3):
    rows = x3.shape[0]
    t = _swap_sublane_major(x3.reshape(rows // SUBLANES, SUBLANES, D_CHUNKS, LANES))
    return jnp.concatenate([t[:, cc].reshape(rows, LANES) for cc in range(D_CHUNKS)], axis=1)


def _matrix_to_rows(y):
    rows = y.shape[0]
    t = jnp.stack([y[:, cc * LANES:(cc + 1) * LANES].reshape(rows // SUBLANES, SUBLANES, LANES)
                   for cc in range(D_CHUNKS)], axis=1)
    return _swap_sublane_major(t).reshape(rows, D_CHUNKS, LANES)


def _mm_kernel(x_ref, w_ref, o_ref):
    o_ref[...] = _dot(x_ref[...], w_ref[...]).astype(o_ref.dtype)


def _in_proj(hb, w_pack):
    tm, tn = 1024, 1792
    return pl.pallas_call(
        _mm_kernel,
        grid=(PACK_COLS // tn, N_TOK // tm),
        in_specs=[pl.BlockSpec((tm, D_MODEL), lambda j, i: (i, 0)),
                  pl.BlockSpec((D_MODEL, tn), lambda j, i: (0, j))],
        out_specs=pl.BlockSpec((tm, tn), lambda j, i: (i, j)),
        out_shape=jax.ShapeDtypeStruct((N_TOK, PACK_COLS), F32),
        compiler_params=_cparams(("parallel", "parallel")),
        name="in_proj",
    )(hb, w_pack)


def _rope128(x, c, s1, s2):
    return x * c + pltpu.roll(x, LANES - 16, 1) * s1 + pltpu.roll(x, 16, 1) * s2


def _q_up_kernel(ql_ref, g_ref, w_ref, c_ref, s1_ref, s2_ref, q_ref):
    y = _rms_norm(ql_ref[...], g_ref[...]).astype(BF16)
    q = _dot(y, w_ref[...])
    c, s1, s2 = c_ref[...], s1_ref[...], s2_ref[...]
    scale = (QK_NOPE + QK_ROPE) ** -0.5 * math.log2(math.e)
    for h in range(MLA_HEADS):
        qh = _rope128(q[:, h * HEAD_PAD:(h + 1) * HEAD_PAD], c, s1, s2)
        q_ref[h] = (qh * scale).astype(BF16)


def _q_up(proj, g, wq, tc, ts1, ts2):
    tm = 512
    spt = SEQ // tm
    tab = pl.BlockSpec((tm, LANES), lambda i: (i, 0))
    return pl.pallas_call(
        _q_up_kernel,
        grid=(N_TOK // tm,),
        in_specs=[pl.BlockSpec((tm, Q_RANK), lambda i: (i, COL_QLAT // Q_RANK)),
                  pl.BlockSpec((1, Q_RANK), lambda i: (0, 0)),
                  pl.BlockSpec((Q_RANK, MLA_HEADS * HEAD_PAD), lambda i: (0, 0)),
                  tab, tab, tab],
        out_specs=pl.BlockSpec((None, MLA_HEADS, tm, HEAD_PAD), lambda i: (i // spt, 0, i % spt, 0)),
        out_shape=jax.ShapeDtypeStruct((BATCH, MLA_HEADS, SEQ, HEAD_PAD), BF16),
        compiler_params=_cparams(("parallel",)),
        name="q_up",
    )(proj, g, wq, tc, ts1, ts2)


def _kv_up_kernel(ckv_ref, small_ref, g_ref, wk_ref, wv_ref, c_ref, s1_ref, s2_ref, k_ref, v_ref):
    y = _rms_norm(ckv_ref[...], g_ref[...]).astype(BF16)
    k_all = _dot(y, wk_ref[...])
    v_all = _dot(y, wv_ref[...])
    lane = lax.broadcasted_iota(I32, small_ref.shape, 1)
    kpe_raw = jnp.where((lane >= KPE_LANE) & (lane < KPE_LANE + QK_ROPE), small_ref[...], 0.0)
    kpe = _rope128(kpe_raw, c_ref[...], s1_ref[...], s2_ref[...])
    for h in range(MLA_HEADS):
        k_ref[h] = (k_all[:, h * HEAD_PAD:(h + 1) * HEAD_PAD] + kpe).astype(BF16)
    ones = jnp.ones((v_all.shape[0], LANES), F32)
    for j in range(N_PAIRS):
        v_ref[j] = jnp.concatenate([v_all[:, j * LANES:(j + 1) * LANES], ones], axis=1).astype(BF16)


def _kv_up(proj, g, wk, wv, tc, ts1, ts2):
    tm = 512
    spt = SEQ // tm
    tab = pl.BlockSpec((tm, LANES), lambda i: (i, 0))
    return pl.pallas_call(
        _kv_up_kernel,
        grid=(N_TOK // tm,),
        in_specs=[pl.BlockSpec((tm, KV_RANK), lambda i: (i, COL_CKV // KV_RANK)),
                  pl.BlockSpec((tm, LANES), lambda i: (i, COL_SMALL // LANES)),
                  pl.BlockSpec((1, KV_RANK), lambda i: (0, 0)),
                  pl.BlockSpec((KV_RANK, MLA_HEADS * HEAD_PAD), lambda i: (0, 0)),
                  pl.BlockSpec((KV_RANK, MLA_HEADS * V_DIM), lambda i: (0, 0)),
                  tab, tab, tab],
        out_specs=[pl.BlockSpec((None, MLA_HEADS, tm, HEAD_PAD), lambda i: (i // spt, 0, i % spt, 0)),
                   pl.BlockSpec((None, N_PAIRS, tm, V_SLAB), lambda i: (i // spt, 0, i % spt, 0))],
        out_shape=[jax.ShapeDtypeStruct((BATCH, MLA_HEADS, SEQ, HEAD_PAD), BF16),
                   jax.ShapeDtypeStruct((BATCH, N_PAIRS, SEQ, V_SLAB), BF16)],
        compiler_params=_cparams(("parallel",)),
        name="kv_up",
    )(proj, proj, g, wk, wv, tc, ts1, ts2)


ATT_T = 512


def _attn_kernel(q_ref, k_ref, v_ref, o_ref, m_ref, acc_ref):
    qi = pl.program_id(2)
    t = ATT_T
    m_ref[...] = jnp.full(m_ref.shape, -jnp.inf, F32)
    acc_ref[...] = jnp.zeros(acc_ref.shape, F32)

    def kv_tile(ks, width, masked):
        v = v_ref[pl.ds(ks, width), :]
        for hh in range(2):
            s = lax.dot_general(q_ref[hh], k_ref[hh, pl.ds(ks, width), :], (((1,), (1,)), ((), ())),
                                preferred_element_type=F32)
            if masked:
                row = lax.broadcasted_iota(I32, s.shape, 0)
                col = lax.broadcasted_iota(I32, s.shape, 1)
                s = jnp.where(ks + col <= qi * t + row, s, NEG)
            m_prev = m_ref[hh]
            m_new = jnp.maximum(m_prev, jnp.max(s, axis=-1, keepdims=True))
            alpha = jnp.exp2(m_prev - m_new)
            p = jnp.exp2(s - jnp.concatenate([m_new] * (width // LANES), axis=1))
            acc_ref[hh] = jnp.concatenate([alpha, alpha], axis=1) * acc_ref[hh] + _dot(p.astype(BF16), v)
            m_ref[hh] = m_new

    def body(kp, carry):
        kv_tile(pl.multiple_of(kp * 2 * t, 2 * t), 2 * t, False)
        return carry

    lax.fori_loop(0, qi // 2, body, 0)

    @pl.when(qi % 2 == 0)
    def _():
        kv_tile(pl.multiple_of(qi * t, t), t, True)

    @pl.when(qi % 2 == 1)
    def _():
        kv_tile(pl.multiple_of((qi - 1) * t, t), 2 * t, True)

    lane = lax.broadcasted_iota(I32, o_ref.shape, 1)
    a0 = acc_ref[0]
    a1 = acc_ref[1]
    o0 = a0[:, :LANES] / a0[:, LANES:]
    o1 = a1[:, :LANES] / a1[:, LANES:]
    o_ref[...] = jnp.where(lane < V_DIM, o0, o1).astype(o_ref.dtype)


def _attention(q, k, v):
    t = ATT_T
    return pl.pallas_call(
        _attn_kernel,
        grid=(BATCH, N_PAIRS, SEQ // t),
        in_specs=[pl.BlockSpec((None, 2, t, HEAD_PAD), lambda b, j, qi: (b, j, qi, 0)),
                  pl.BlockSpec((None, 2, SEQ, HEAD_PAD), lambda b, j, qi: (b, j, 0, 0)),
                  pl.BlockSpec((None, None, SEQ, V_SLAB), lambda b, j, qi: (b, j, 0, 0))],
        out_specs=pl.BlockSpec((None, t, LANES), lambda b, j, qi: (b, qi, j)),
        out_shape=jax.ShapeDtypeStruct((BATCH, SEQ, MLA_HEADS * V_DIM), BF16),
        scratch_shapes=[pltpu.VMEM((2, t, LANES), F32), pltpu.VMEM((2, t, V_SLAB), F32)],
        compiler_params=_cparams(("parallel", "parallel", "parallel")),
        name="mla_attention",
    )(q, k, v)


HALO = SUBLANES
HEADS_PER_GROUP = SSM_HEADS // SSM_GROUPS
GROUP_W = SSM_INNER // SSM_GROUPS


def _ssd_kernel(xbc_ref, halo_ref, z_ref, small_ref, cw_ref, cb_ref, dtb_ref, alog_ref, dexp_ref,
                nw_ref, e_ref, y_ref, st_ref, cat_ref):
    c = pl.program_id(1)
    L = SSM_CHUNK

    @pl.when(c == 0)
    def _():
        st_ref[...] = jnp.zeros(st_ref.shape, F32)

    cat_ref[0:HALO, :] = jnp.where(c == 0, 0.0, halo_ref[...])
    cat_ref[HALO:HALO + L, :] = xbc_ref[...]
    acc = jnp.broadcast_to(cb_ref[...], (L, SSM_CONV_DIM))
    for kk in range(SSM_CONV):
        off = HALO - (SSM_CONV - 1) + kk
        acc = acc + cw_ref[kk:kk + 1, :] * cat_ref[off:off + L, :]
    xc = _silu(acc)
    xs = xc[:, :SSM_INNER]
    bm = xc[:, SSM_INNER:SSM_INNER + SSM_GROUPS * SSM_STATE]
    cm = xc[:, SSM_INNER + SSM_GROUPS * SSM_STATE:]

    lane = lax.broadcasted_iota(I32, (L, LANES), 1)
    dt_lanes = (lane >= DT_LANE) & (lane < DT_LANE + SSM_HEADS)
    dt_in = small_ref[...] + dtb_ref[...]
    dt = jnp.maximum(dt_in, 0.0) + jnp.log1p(jnp.exp(-jnp.abs(dt_in)))
    dt = jnp.where(dt_lanes, dt, 0.0)
    a = -jnp.exp(alog_ref[...])
    a_dt = dt * a
    row = lax.broadcasted_iota(I32, (L, L), 0)
    col = lax.broadcasted_iota(I32, (L, L), 1)
    tri = row >= col
    a_cum = jnp.dot(tri.astype(F32), a_dt, precision=HIGHEST, preferred_element_type=F32)
    a_last = a_cum[L - 1:L, :]
    t = jnp.concatenate([dt, jnp.exp(a_cum), jnp.exp(a_last - a_cum)], axis=0)
    t_hi = t.astype(BF16)
    r1 = t - t_hi.astype(F32)
    t_mid = r1.astype(BF16)
    t_lo = (r1 - t_mid.astype(F32)).astype(BF16)
    expanded = _dot(jnp.concatenate([t_hi, t_mid, t_lo], axis=1), e_ref[...])
    x_dt = xs * expanded[0:L]
    eac_x = expanded[L:2 * L]
    ds_x = expanded[2 * L:3 * L]
    cd_x = eac_x[L - 1:L, :]
    xd = (x_dt * ds_x).astype(BF16)
    x_bf = x_dt.astype(BF16)
    a_cum_t = a_cum.T
    lane_h = lax.broadcasted_iota(I32, (L, LANES), 1)

    y_parts = []
    for g in range(SSM_GROUPS):
        bg = bm[:, g * SSM_STATE:(g + 1) * SSM_STATE].astype(BF16)
        cg = cm[:, g * SSM_STATE:(g + 1) * SSM_STATE].astype(BF16)
        cb = lax.dot_general(cg, bg, (((1,), (1,)), ((), ())), preferred_element_type=F32)
        st_prev = st_ref[g]
        y_off = _dot(cg, st_prev.astype(BF16)) * eac_x[:, g * GROUP_W:(g + 1) * GROUP_W]
        for jp in range(HEADS_PER_GROUP // 2):
            pair = g * (HEADS_PER_GROUP // 2) + jp
            xp = x_bf[:, pair * LANES:(pair + 1) * LANES]
            outs = []
            for hh in range(2):
                hl = DT_LANE + 2 * pair + hh
                seg = a_cum[:, hl:hl + 1] - a_cum_t[hl:hl + 1, :]
                decay = jnp.where(tri, jnp.exp(seg), 0.0)
                outs.append(_dot((cb * decay).astype(BF16), xp))
            y_parts.append(jnp.where(lane_h < SSM_HEADDIM, outs[0], outs[1])
                           + y_off[:, jp * LANES:(jp + 1) * LANES])
        upd = lax.dot_general(bg, xd[:, g * GROUP_W:(g + 1) * GROUP_W], (((0,), (0,)), ((), ())),
                              preferred_element_type=F32)
        st_ref[g] = st_prev * cd_x[:, g * GROUP_W:(g + 1) * GROUP_W] + upd

    y = jnp.concatenate(y_parts, axis=1) + xs * dexp_ref[...]
    y = y * _silu(z_ref[...])
    normed = []
    for g in range(SSM_GROUPS):
        yg = y[:, g * GROUP_W:(g + 1) * GROUP_W]
        normed.append(yg * lax.rsqrt(jnp.mean(yg * yg, axis=-1, keepdims=True) + RMS_EPS))
    y_ref[...] = (jnp.concatenate(normed, axis=1) * nw_ref[...]).astype(y_ref.dtype)


def _ssd(proj, cw, cb, dtb, alog, dexp, nw, e_mat):
    L = SSM_CHUNK
    nc = SEQ // L
    xbc_blk = COL_XBC // SSM_CONV_DIM

    def row1(w):
        return pl.BlockSpec((1, w), lambda b, c: (0, 0))

    return pl.pallas_call(
        _ssd_kernel,
        grid=(BATCH, nc),
        in_specs=[pl.BlockSpec((L, SSM_CONV_DIM), lambda b, c: (b * nc + c, xbc_blk)),
                  pl.BlockSpec((HALO, SSM_CONV_DIM),
                               lambda b, c: (jnp.maximum((b * nc + c) * (L // HALO) - 1, 0), xbc_blk)),
                  pl.BlockSpec((L, SSM_INNER), lambda b, c: (b * nc + c, COL_Z // SSM_INNER)),
                  pl.BlockSpec((L, LANES), lambda b, c: (b * nc + c, COL_SMALL // LANES)),
                  pl.BlockSpec((SSM_CONV, SSM_CONV_DIM), lambda b, c: (0, 0)),
                  row1(SSM_CONV_DIM), row1(LANES), row1(LANES), row1(SSM_INNER), row1(SSM_INNER),
                  pl.BlockSpec((3 * LANES, SSM_INNER), lambda b, c: (0, 0))],
        out_specs=pl.BlockSpec((L, SSM_INNER), lambda b, c: (b * nc + c, 0)),
        out_shape=jax.ShapeDtypeStruct((N_TOK, SSM_INNER), BF16),
        scratch_shapes=[pltpu.VMEM((SSM_GROUPS, SSM_STATE, GROUP_W), F32),
                        pltpu.VMEM((HALO + L, SSM_CONV_DIM), F32)],
        compiler_params=_cparams(("parallel", "arbitrary")),
        name="mamba2_ssd",
    )(proj, proj, proj, proj, cw, cb, dtb, alog, dexp, nw, e_mat)


def _merge_kernel(attn_ref, ssm_ref, ga_ref, gb_ref, h_ref, wa_ref, ws_ref, wo_ref, g_ref, b_ref,
                  hf_ref, hb_ref):
    ya = _dot(attn_ref[...], wa_ref[...])
    yb = _dot(ssm_ref[...], ws_ref[...])
    mix = _sigmoid(ga_ref[...]) * ya + _sigmoid(gb_ref[...]) * yb
    mixed = _dot(mix.astype(BF16), wo_ref[...])
    h1 = _layer_norm(ALPHA * h_ref[...] + mixed, g_ref[...], b_ref[...])
    hf_ref[...] = h1
    hb_ref[...] = h1.astype(BF16)


def _merge(attn, ssm_y, proj, hf, wa, ws, wo, g, b):
    tm = 512
    full = lambda r, c: pl.BlockSpec((r, c), lambda i: (0, 0))
    tile = lambda c, j=0: pl.BlockSpec((tm, c), lambda i: (i, j))
    return pl.pallas_call(
        _merge_kernel,
        grid=(N_TOK // tm,),
        in_specs=[tile(MLA_HEADS * V_DIM), tile(SSM_INNER), tile(D_MODEL, COL_GATE_A // D_MODEL),
                  tile(D_MODEL, COL_GATE_B // D_MODEL), tile(D_MODEL),
                  full(MLA_HEADS * V_DIM, D_MODEL), full(SSM_INNER, D_MODEL), full(D_MODEL, D_MODEL),
                  full(1, D_MODEL), full(1, D_MODEL)],
        out_specs=[tile(D_MODEL), tile(D_MODEL)],
        out_shape=[jax.ShapeDtypeStruct((N_TOK, D_MODEL), F32),
                   jax.ShapeDtypeStruct((N_TOK, D_MODEL), BF16)],
        compiler_params=_cparams(("parallel",)),
        name="merge_ln1",
    )(attn, ssm_y, proj, proj, hf, wa, ws, wo, g, b)


def _first_argmax(vals, lane):
    m = jnp.max(vals, axis=-1, keepdims=True)
    idx = jnp.min(jnp.where(vals == m, lane, float(N_EXPERTS)), axis=-1, keepdims=True)
    return m, idx


def _post_kernel(hf_ref, hb_ref, p_ref, wr_ref, rb_ref, wsg_ref, wsu_ref, wsd_ref, wp_ref, wpg_ref,
                 base_ref, idx_ref, wt_ref, cnt_ref):
    hf = hf_ref[...]
    hb = hb_ref[...]
    tm = hf.shape[0]
    logits = jnp.dot(hf, wr_ref[...], precision=HIGHEST, preferred_element_type=F32)
    scores = _sigmoid(logits)
    sel = scores + rb_ref[...]
    lane_i = lax.broadcasted_iota(I32, (tm, N_EXPERTS), 1)
    grp = lane_i // PER_GROUP
    lane = lane_i.astype(F32)

    grp_scores = []
    for g in range(N_GROUPS):
        vals = jnp.where(grp == g, sel, -jnp.inf)
        m1, i1 = _first_argmax(vals, lane)
        m2 = jnp.max(jnp.where(lane == i1, -jnp.inf, vals), axis=-1, keepdims=True)
        grp_scores.append(m1 + m2)
    keep = jnp.zeros((tm, N_EXPERTS), jnp.bool_)
    for g in range(N_GROUPS):
        rank = jnp.zeros((tm, 1), I32)
        for o in range(N_GROUPS):
            if o == g:
                continue
            ahead = (grp_scores[o] > grp_scores[g]) if o > g else (grp_scores[o] >= grp_scores[g])
            rank = rank + ahead.astype(I32)
        keep = keep | ((grp == g) & (rank < TOPK_GROUPS))
    masked = jnp.where(keep, sel, -jnp.inf)

    lane_k = lax.broadcasted_iota(I32, (tm, TOP_K), 1)
    idx_out = jnp.zeros((tm, TOP_K), I32)
    w_out = jnp.zeros((tm, TOP_K), F32)
    chosen = jnp.zeros((tm, N_EXPERTS), F32)
    for kk in range(TOP_K):
        _, ik = _first_argmax(masked, lane)
        hit = lane == ik
        wk = jnp.sum(jnp.where(hit, scores, 0.0), axis=-1, keepdims=True)
        masked = jnp.where(hit, -jnp.inf, masked)
        chosen = jnp.where(hit, 1.0, chosen)
        idx_out = jnp.where(lane_k == kk, ik.astype(I32), idx_out)
        w_out = jnp.where(lane_k == kk, wk, w_out)
    w_out = w_out / jnp.sum(w_out, axis=-1, keepdims=True) * ROUTED_SCALE
    idx_ref[...] = idx_out
    wt_ref[...] = w_out
    cnt_ref[...] = jnp.sum(chosen, axis=0, keepdims=True)

    shared = _dot((_silu(_dot(hb, wsg_ref[...])) * _dot(hb, wsu_ref[...])).astype(BF16), wsd_ref[...])
    ple = _dot(p_ref[...].astype(BF16), wp_ref[...]) * _sigmoid(_dot(hb, wpg_ref[...]))
    base_ref[...] = ALPHA * hf + shared + ple


def _post(hf, hb, p, wr, rb, wsg, wsu, wsd, wp, wpg):
    tm = ROUTE_T
    full = lambda r, c: pl.BlockSpec((r, c), lambda i: (0, 0))
    tile = lambda c: pl.BlockSpec((tm, c), lambda i: (i, 0))
    return pl.pallas_call(
        _post_kernel,
        grid=(N_RTILES,),
        in_specs=[tile(D_MODEL), tile(D_MODEL), tile(PLE_DIM),
                  full(D_MODEL, N_EXPERTS), full(1, N_EXPERTS),
                  full(D_MODEL, EXPERT_FF), full(D_MODEL, EXPERT_FF), full(EXPERT_FF, D_MODEL),
                  full(PLE_DIM, D_MODEL), full(D_MODEL, D_MODEL)],
        out_specs=[tile(D_MODEL), tile(TOP_K), tile(TOP_K),
                   pl.BlockSpec((None, 1, N_EXPERTS), lambda i: (i, 0, 0))],
        out_shape=[jax.ShapeDtypeStruct((N_TOK, D_MODEL), F32),
                   jax.ShapeDtypeStruct((N_TOK, TOP_K), I32),
                   jax.ShapeDtypeStruct((N_TOK, TOP_K), F32),
                   jax.ShapeDtypeStruct((N_RTILES, 1, N_EXPERTS), F32)],
        compiler_params=_cparams(("parallel",)),
        name="router_shared_ple",
    )(hf, hb, p, wr, rb, wsg, wsu, wsd, wp, wpg)


def _pos_kernel(idx_ref, base_ref, pos_ref):
    idx = idx_ref[...]
    tm = idx.shape[0]
    lane = lax.broadcasted_iota(I32, (tm, N_EXPERTS), 1)
    hits = [lane == idx[:, kk:kk + 1] for kk in range(TOP_K)]
    chosen = hits[0]
    for kk in range(1, TOP_K):
        chosen = chosen | hits[kk]
    r = lax.broadcasted_iota(I32, (tm, tm), 0)
    c = lax.broadcasted_iota(I32, (tm, tm), 1)
    earlier = jnp.where(c < r, 1.0, 0.0).astype(BF16)
    rank = _dot(earlier, jnp.where(chosen, 1.0, 0.0).astype(BF16))
    slot = rank + base_ref[...]
    lane_k = lax.broadcasted_iota(I32, (tm, TOP_K), 1)
    out = jnp.zeros((tm, TOP_K), I32)
    for kk in range(TOP_K):
        pk = jnp.sum(jnp.where(hits[kk], slot, 0.0), axis=-1, keepdims=True)
        out = jnp.where(lane_k == kk, pk.astype(I32), out)
    pos_ref[...] = out


def _slot_positions(idx, base):
    tm = ROUTE_T
    return pl.pallas_call(
        _pos_kernel,
        grid=(N_RTILES,),
        in_specs=[pl.BlockSpec((tm, TOP_K), lambda i: (i, 0)),
                  pl.BlockSpec((None, 1, N_EXPERTS), lambda i: (i, 0, 0))],
        out_specs=pl.BlockSpec((tm, TOP_K), lambda i: (i, 0)),
        out_shape=jax.ShapeDtypeStruct((N_TOK, TOP_K), I32),
        compiler_params=_cparams(("parallel",)),
        name="slot_positions",
    )(idx, base)


def _slot_layout(cnt):
    counts_te = cnt.reshape(N_RTILES, N_EXPERTS).astype(I32)
    counts = jnp.sum(counts_te, axis=0)
    padded = (counts + SLOT_BLOCK - 1) // SLOT_BLOCK * SLOT_BLOCK
    pad_end = jnp.cumsum(padded)
    pad_start = pad_end - padded
    tile_off = jnp.cumsum(counts_te, axis=0) - counts_te
    base = (pad_start[None, :] + tile_off).astype(F32).reshape(N_RTILES, 1, N_EXPERTS)
    n_used = (pad_end[-1] // SLOT_BLOCK).astype(I32).reshape(1)
    return base, counts, pad_start, padded, n_used


DISP_T = 256


def _dispatch_kernel(cnt_ref, ps_ref, pd_ref, nu_ref, pos_hbm, h3_ref, x_hbm,
                     pos_smem, zbuf, pos_sem, row_sem, pad_sem):
    i = pl.program_id(0)
    n = pl.num_programs(0)

    def pos_copy(blk, slot):
        return pltpu.make_async_copy(pos_hbm.at[blk], pos_smem.at[slot], pos_sem.at[slot])

    def pad_row(e, r):
        return pltpu.make_async_copy(zbuf.at[0], x_hbm.at[ps_ref[e] + r], pad_sem)

    def pad_block(b):
        return pltpu.make_async_copy(zbuf, x_hbm.at[pl.ds(b * SLOT_BLOCK, SLOT_BLOCK)], pad_sem)

    def for_each_pad(fn_row, fn_block):
        def per_expert(e, carry):
            def per_row(r, c2):
                fn_row(e, r)
                return c2
            return lax.fori_loop(cnt_ref[e], pd_ref[e], per_row, carry)
        lax.fori_loop(0, N_EXPERTS, per_expert, 0)

        def per_block(b, carry):
            fn_block(b)
            return carry
        lax.fori_loop(nu_ref[0], N_BLOCKS, per_block, 0)

    @pl.when(i == 0)
    def _():
        pos_copy(0, 0).start()
        pos_copy(1, 1).start()
        zbuf[...] = jnp.zeros(zbuf.shape, F32)
        for_each_pad(lambda e, r: pad_row(e, r).start(), lambda b: pad_block(b).start())
        for_each_pad(lambda e, r: pad_row(e, r).wait(), lambda b: pad_block(b).wait())

    slot = i % 2
    pos_copy(i, slot).wait()

    def body(t, carry):
        for kk in range(TOP_K):
            dst = pos_smem[slot, t * TOP_K + kk]
            pltpu.make_async_copy(h3_ref.at[t], x_hbm.at[dst], row_sem).start(priority=kk % 2)
        return carry
    lax.fori_loop(0, DISP_T, body, 0)

    @pl.when(i + 2 < n)
    def _():
        pos_copy(i + 2, slot).start()

    for kk in range(TOP_K):
        pltpu.make_async_copy(h3_ref, x_hbm.at[pl.ds(0, DISP_T)], row_sem).wait()


def _dispatch(counts, pad_start, padded, n_used, pos, h3):
    n_tiles = N_TOK // DISP_T
    grid_spec = pltpu.PrefetchScalarGridSpec(
        num_scalar_prefetch=4,
        grid=(n_tiles,),
        in_specs=[pl.BlockSpec(memory_space=pl.ANY),
                  pl.BlockSpec((DISP_T, D_CHUNKS, LANES), lambda i, *_: (i, 0, 0))],
        out_specs=pl.BlockSpec(memory_space=pl.ANY),
        scratch_shapes=[pltpu.SMEM((2, DISP_T * TOP_K), I32),
                        pltpu.VMEM((SLOT_BLOCK, D_CHUNKS, LANES), F32),
                        pltpu.SemaphoreType.DMA((2,)),
                        pltpu.SemaphoreType.DMA,
                        pltpu.SemaphoreType.DMA])
    return pl.pallas_call(
        _dispatch_kernel,
        grid_spec=grid_spec,
        out_shape=jax.ShapeDtypeStruct((CAP, D_CHUNKS, LANES), F32),
        compiler_params=_cparams(("arbitrary",)),
        name="dispatch_scatter",
    )(counts, pad_start, padded, n_used, pos.reshape(n_tiles, DISP_T * TOP_K), h3)


def _moe_kernel(fb_ref, nb_ref, nu_ref, x_hbm, wg_ref, wu_ref, wd_ref, y_hbm,
                xbuf, ybuf, wg_bf, wu_bf, wd_bf, in_sem, out_sem):
    e = pl.program_id(0)
    first = fb_ref[e]
    nb = nb_ref[e]
    n_used = nu_ref[0]

    def rows(g):
        return pl.ds(pl.multiple_of(g * SLOT_BLOCK, SLOT_BLOCK), SLOT_BLOCK)

    def x_copy(g):
        slot = g % MOE_RING
        return pltpu.make_async_copy(x_hbm.at[rows(g)], xbuf.at[slot], in_sem.at[slot])

    def y_copy(g):
        slot = g % MOE_RING
        return pltpu.make_async_copy(ybuf.at[slot], y_hbm.at[rows(g)], out_sem.at[slot])

    @pl.when(e == 0)
    def _():
        for g in range(MOE_RING - 2):
            x_copy(g).start()

    def run_blocks(g, count):
        for j in range(count):
            x_copy(g + j).wait()
        for j in range(count):
            nxt = g + j + MOE_RING - 2

            @pl.when(nxt < n_used)
            def _():
                x_copy(nxt).start()

        x3 = jnp.concatenate([xbuf[(g + j) % MOE_RING] for j in range(count)], axis=0)
        x = _rows_to_matrix(x3).astype(BF16)
        hid = (_silu(_dot(x, wg_bf[...])) * _dot(x, wu_bf[...])).astype(BF16)
        y3 = _matrix_to_rows(_dot(hid, wd_bf[...]))
        for j in range(count):
            @pl.when(g + j >= MOE_RING)
            def _():
                y_copy(g + j - MOE_RING).wait()

            ybuf[(g + j) % MOE_RING] = y3[j * SLOT_BLOCK:(j + 1) * SLOT_BLOCK]
            y_copy(g + j).start()

    @pl.when(nb > 0)
    def _():
        wg_bf[...] = wg_ref[...].astype(BF16)
        wu_bf[...] = wu_ref[...].astype(BF16)
        wd_bf[...] = wd_ref[...].astype(BF16)

        def pair(p, carry):
            run_blocks(first + 2 * p, 2)
            return carry

        lax.fori_loop(0, nb // 2, pair, 0)

        @pl.when(nb % 2 == 1)
        def _():
            run_blocks(first + nb - 1, 1)

    @pl.when(e == N_EXPERTS - 1)
    def _():
        for j in range(MOE_RING):
            y_copy(n_used - 1 - j).wait()
        ybuf[0] = jnp.zeros(ybuf.shape[1:], F32)

        def tail_copy(b):
            return pltpu.make_async_copy(
                ybuf.at[0], y_hbm.at[pl.ds(pl.multiple_of(b * SLOT_BLOCK, SLOT_BLOCK), SLOT_BLOCK)],
                out_sem.at[0])

        def start(b, carry):
            tail_copy(b).start()
            return carry

        def wait(b, carry):
            tail_copy(b).wait()
            return carry

        lax.fori_loop(nu_ref[0], N_BLOCKS, start, 0)
        lax.fori_loop(nu_ref[0], N_BLOCKS, wait, 0)


def _moe(first_block, n_blocks, n_used, xs, wg, wu, wd, layer):
    wspec = lambda r, c: pl.BlockSpec((None, None, r, c), lambda e, *_: (layer, e, 0, 0))
    grid_spec = pltpu.PrefetchScalarGridSpec(
        num_scalar_prefetch=3,
        grid=(N_EXPERTS,),
        in_specs=[pl.BlockSpec(memory_space=pl.ANY),
                  wspec(D_MODEL, EXPERT_FF), wspec(D_MODEL, EXPERT_FF), wspec(EXPERT_FF, D_MODEL)],
        out_specs=pl.BlockSpec(memory_space=pl.ANY),
        scratch_shapes=[pltpu.VMEM((MOE_RING, SLOT_BLOCK, D_CHUNKS, LANES), F32),
                        pltpu.VMEM((MOE_RING, SLOT_BLOCK, D_CHUNKS, LANES), F32),
                        pltpu.VMEM((D_MODEL, EXPERT_FF), BF16),
                        pltpu.VMEM((D_MODEL, EXPERT_FF), BF16),
                        pltpu.VMEM((EXPERT_FF, D_MODEL), BF16),
                        pltpu.SemaphoreType.DMA((MOE_RING,)),
                        pltpu.SemaphoreType.DMA((MOE_RING,))])
    return pl.pallas_call(
        _moe_kernel,
        grid_spec=grid_spec,
        out_shape=jax.ShapeDtypeStruct((CAP, D_CHUNKS, LANES), F32),
        compiler_params=_cparams(("arbitrary",)),
        name="routed_experts",
    )(first_block, n_blocks, n_used, xs, wg, wu, wd)


COMB_T = 128


def _combine_kernel(pos_hbm, y_hbm, w_ref, base_ref, g_ref, b_ref, hf_ref, hb_ref,
                    pos_smem, buf, routed_ref, pos_sem, row_sem):
    i = pl.program_id(0)
    n = pl.num_programs(0)
    slot = i % 2
    nslot = (i + 1) % 2
    n_groups = COMB_T // SUBLANES

    def pos_copy(blk, s):
        return pltpu.make_async_copy(pos_hbm.at[blk], pos_smem.at[s], pos_sem.at[s])

    def issue_group(s, g):
        for j in range(SUBLANES):
            t = g * SUBLANES + j
            for kk in range(TOP_K):
                src = pos_smem[s, t * TOP_K + kk]
                pltpu.make_async_copy(y_hbm.at[src], buf.at[s, t, kk],
                                      row_sem.at[s]).start(priority=kk % 2)

    def reduce_group(s, g):
        rows8 = pl.ds(pl.multiple_of(g * SUBLANES, SUBLANES), SUBLANES)
        acc = None
        for kk in range(TOP_K):
            term = w_ref[rows8, kk:kk + 1] * _rows_to_matrix(buf[s, rows8, kk])
            acc = term if acc is None else acc + term
        routed_ref[rows8, :] = acc

    def group_loop(fn):
        def body(g, carry):
            fn(g)
            return carry
        lax.fori_loop(0, n_groups, body, 0)

    @pl.when(i == 0)
    def _():
        pos_copy(0, 0).start()
        pos_copy(1, 1).start()
        pos_copy(0, 0).wait()
        group_loop(lambda g: issue_group(0, g))

    for kk in range(TOP_K):
        pltpu.make_async_copy(y_hbm.at[pl.ds(0, COMB_T)], buf.at[slot, :, kk], row_sem.at[slot]).wait()

    @pl.when(i + 1 < n)
    def _():
        pos_copy(i + 1, nslot).wait()

        def both(g):
            issue_group(nslot, g)
            reduce_group(slot, g)
        group_loop(both)

    @pl.when(i + 1 == n)
    def _():
        group_loop(lambda g: reduce_group(slot, g))

    @pl.when(i + 2 < n)
    def _():
        pos_copy(i + 2, slot).start()

    h2 = _layer_norm(base_ref[...] + routed_ref[...], g_ref[...], b_ref[...])
    hf_ref[...] = h2
    hb_ref[...] = h2.astype(BF16)


def _combine(pos, y3, wts, base, g, b):
    tm = COMB_T
    n_tiles = N_TOK // tm
    full = lambda r, c: pl.BlockSpec((r, c), lambda i: (0, 0))
    tile = lambda c: pl.BlockSpec((tm, c), lambda i: (i, 0))
    return pl.pallas_call(
        _combine_kernel,
        grid=(n_tiles,),
        in_specs=[pl.BlockSpec(memory_space=pl.ANY), pl.BlockSpec(memory_space=pl.ANY),
                  tile(TOP_K), tile(D_MODEL), full(1, D_MODEL), full(1, D_MODEL)],
        out_specs=[tile(D_MODEL), tile(D_MODEL)],
        out_shape=[jax.ShapeDtypeStruct((N_TOK, D_MODEL), F32),
                   jax.ShapeDtypeStruct((N_TOK, D_MODEL), BF16)],
        scratch_shapes=[pltpu.SMEM((2, tm * TOP_K), I32),
                        pltpu.VMEM((2, tm, TOP_K, D_CHUNKS, LANES), F32),
                        pltpu.VMEM((tm, D_MODEL), F32),
                        pltpu.SemaphoreType.DMA((2,)),
                        pltpu.SemaphoreType.DMA((2,))],
        compiler_params=_cparams(("arbitrary",)),
        name="combine_ln2",
    )(pos.reshape(n_tiles, tm * TOP_K), y3, wts, base, g, b)


def _pack_w_in(w):
    q_lat = w[:, 0:384]
    c_kv = w[:, 384:640]
    k_pe = w[:, 640:672]
    z = w[:, 672:1696]
    xbc = w[:, 1696:3232]
    dt = w[:, 3232:3248]
    gate_a = w[:, 3248:4272]
    gate_b = w[:, 4272:5296]
    zeros = lambda n: jnp.zeros((D_MODEL, n), w.dtype)
    small = jnp.concatenate([zeros(KPE_LANE), k_pe, dt, zeros(LANES - DT_LANE - SSM_HEADS)], axis=1)
    return jnp.concatenate([gate_a, gate_b, z, xbc, q_lat, small, c_kv], axis=1).astype(BF16)


def _pack_w_q(w):
    w = w.reshape(Q_RANK, MLA_HEADS, QK_NOPE + QK_ROPE)
    w = jnp.pad(w, ((0, 0), (0, 0), (0, HEAD_PAD - QK_NOPE - QK_ROPE)))
    return w.reshape(Q_RANK, MLA_HEADS * HEAD_PAD).astype(BF16)


def _pack_w_kv(w):
    w = w.reshape(KV_RANK, MLA_HEADS, QK_NOPE + V_DIM)
    wk = jnp.pad(w[:, :, :QK_NOPE], ((0, 0), (0, 0), (0, HEAD_PAD - QK_NOPE)))
    wv = w[:, :, QK_NOPE:]
    return (wk.reshape(KV_RANK, MLA_HEADS * HEAD_PAD).astype(BF16),
            wv.reshape(KV_RANK, MLA_HEADS * V_DIM).astype(BF16))


def _rope_tables(positions):
    half = QK_ROPE // 2
    inv_freq = jnp.power(ROPE_THETA, -jnp.arange(half, dtype=F32) * (2.0 / QK_ROPE))
    ang = positions.astype(F32).reshape(N_TOK, 1) * inv_freq
    cos, sin = jnp.cos(ang), jnp.sin(ang)
    z = lambda n: jnp.zeros((N_TOK, n), F32)
    tail = HEAD_PAD - QK_NOPE - QK_ROPE
    tc = jnp.concatenate([jnp.ones((N_TOK, QK_NOPE), F32), cos, cos, z(tail)], axis=1)
    ts1 = jnp.concatenate([z(QK_NOPE), -sin, z(half), z(tail)], axis=1)
    ts2 = jnp.concatenate([z(QK_NOPE), z(half), sin, z(tail)], axis=1)
    return tc, ts1, ts2


def _head_lane_row(v):
    return jnp.zeros((1, LANES), F32).at[0, DT_LANE:DT_LANE + SSM_HEADS].set(v.astype(F32))


def _expand_matrix():
    r = jnp.arange(LANES)[:, None]
    c = jnp.arange(SSM_INNER)[None, :]
    e = ((r - DT_LANE) == (c // SSM_HEADDIM)).astype(BF16)
    return jnp.concatenate([e, e, e], axis=0)


def kernel(x, p, positions, w_in, q_norm, w_q_up, kv_norm, w_kv_up, conv_w, conv_b, dt_bias, a_log, d_skip,
           ssm_norm, w_attn_br, w_ssm_br, w_o, ln1_g, ln1_b, w_router, router_bias, w_exp_gate, w_exp_up,
           w_exp_down, w_sh_gate, w_sh_up, w_sh_down, w_ple, w_ple_gate, ln2_g, ln2_b):
    tc, ts1, ts2 = _rope_tables(positions)
    e_mat = _expand_matrix()
    hf = x.reshape(N_TOK, D_MODEL)
    hb = hf.astype(BF16)
    row = lambda v: v.reshape(1, -1).astype(F32)
    for i in range(DEPTH):
        proj = _in_proj(hb, _pack_w_in(w_in[i]))
        q = _q_up(proj, row(q_norm[i]), _pack_w_q(w_q_up[i]), tc, ts1, ts2)
        wk, wv = _pack_w_kv(w_kv_up[i])
        k, v = _kv_up(proj, row(kv_norm[i]), wk, wv, tc, ts1, ts2)
        attn = _attention(q, k, v).reshape(N_TOK, MLA_HEADS * V_DIM)
        ssm_y = _ssd(proj, conv_w[i], row(conv_b[i]), _head_lane_row(dt_bias[i]), _head_lane_row(a_log[i]),
                     row(jnp.repeat(d_skip[i], SSM_HEADDIM)), row(ssm_norm[i]), e_mat)
        hf, hb = _merge(attn, ssm_y, proj, hf, w_attn_br[i].astype(BF16), w_ssm_br[i].astype(BF16),
                        w_o[i].astype(BF16), row(ln1_g[i]), row(ln1_b[i]))
        base, idx, wts, cnt = _post(hf, hb, p[i].reshape(N_TOK, PLE_DIM), w_router[i], row(router_bias[i]),
                                    w_sh_gate[i].astype(BF16), w_sh_up[i].astype(BF16),
                                    w_sh_down[i].astype(BF16), w_ple[i].astype(BF16),
                                    w_ple_gate[i].astype(BF16))
        base_slots, counts, pad_start, padded, n_used = _slot_layout(cnt)
        pos = _slot_positions(idx, base_slots)
        xs = _dispatch(counts, pad_start, padded, n_used, pos, hf.reshape(N_TOK, D_CHUNKS, LANES))
        y3 = _moe(pad_start // SLOT_BLOCK, padded // SLOT_BLOCK, n_used, xs,
                  w_exp_gate, w_exp_up, w_exp_down, i)
        hf, hb = _combine(pos, y3, wts, base, row(ln2_g[i]), row(ln2_b[i]))
    return hf.reshape(BATCH, SEQ, D_MODEL)
```

```python
import functools
import math

import jax
import jax.numpy as jnp
from jax import lax
from jax.experimental import pallas as pl
from jax.experimental.pallas import tpu as pltpu

F32 = jnp.float32
BF16 = jnp.bfloat16
I32 = jnp.int32
HIGHEST = lax.Precision.HIGHEST

D_MODEL = 1024
BATCH = 4
SEQ = 8192
DEPTH = 2
N_TOK = BATCH * SEQ
MLA_HEADS = 8
QK_NOPE = 64
QK_ROPE = 32
V_DIM = 64
Q_RANK = 384
KV_RANK = 256
ROPE_THETA = 10000.0
SSM_INNER = 1024
SSM_HEADDIM = 64
SSM_HEADS = 16
SSM_GROUPS = 2
SSM_STATE = 128
SSM_CONV = 4
SSM_CHUNK = 128
SSM_CONV_DIM = 1536
N_EXPERTS = 256
TOP_K = 8
N_GROUPS = 8
TOPK_GROUPS = 4
PER_GROUP = N_EXPERTS // N_GROUPS
EXPERT_FF = 256
ROUTED_SCALE = 2.5
PLE_DIM = 256
ALPHA = (2 * DEPTH) ** 0.25
LN_EPS = 1e-5
RMS_EPS = 1e-6

LANES = 128
SUBLANES = 8
VMEM_LIMIT = 48 * 1024 * 1024

COL_GATE_A = 0
COL_GATE_B = 1024
COL_Z = 2048
COL_XBC = 3072
COL_QLAT = 4608
COL_SMALL = 4992
COL_CKV = 5120
PACK_COLS = 5376
KPE_LANE = 64
DT_LANE = 96

HEAD_PAD = 128
N_PAIRS = MLA_HEADS // 2
V_SLAB = 2 * LANES

ROUTE_T = 256
N_RTILES = N_TOK // ROUTE_T
SLOT_BLOCK = 128
N_ASSIGN = N_TOK * TOP_K
N_BLOCKS = N_ASSIGN // SLOT_BLOCK + N_EXPERTS
CAP = N_BLOCKS * SLOT_BLOCK
D_CHUNKS = D_MODEL // LANES
MOE_RING = 6

NEG = float(jnp.finfo(jnp.float32).min)


def _cparams(sem):
    return pltpu.CompilerParams(dimension_semantics=sem, vmem_limit_bytes=VMEM_LIMIT)


def _sigmoid(x):
    return 1.0 / (1.0 + jnp.exp(-x))


def _silu(x):
    return x * _sigmoid(x)


def _layer_norm(x, g, b):
    mu = jnp.mean(x, axis=-1, keepdims=True)
    xc = x - mu
    var = jnp.mean(xc * xc, axis=-1, keepdims=True)
    return xc * lax.rsqrt(var + LN_EPS) * g + b


def _rms_norm(x, g):
    return x * lax.rsqrt(jnp.mean(x * x, axis=-1, keepdims=True) + RMS_EPS) * g


def _dot(a, b):
    return jnp.dot(a, b, preferred_element_type=F32)


def _swap_sublane_major(x):
    groups = x.shape[0]
    for d in (4, 2, 1):
        y = x.reshape(groups, SUBLANES // (2 * d), 2, d, SUBLANES, LANES)
        lo, hi = y[:, :, 0], y[:, :, 1]
        shp = lo.shape
        keep = (lax.broadcasted_iota(I32, shp, 3) & d) == 0
        rot = lambda v, s: pltpu.roll(v.reshape(-1, SUBLANES, LANES), s, 1).reshape(shp)
        new_lo = jnp.where(keep, lo, rot(hi, d))
        new_hi = jnp.where(keep, rot(lo, SUBLANES - d), hi)
        x = jnp.stack([new_lo, new_hi], axis=2).reshape(groups, SUBLANES, SUBLANES, LANES)
    return x


def _rows_to_matrix(x3):
    rows = x3.shape[0]
    t = _swap_sublane_major(x3.reshape(rows // SUBLANES, SUBLANES, D_CHUNKS, LANES))
    return jnp.concatenate([t[:, cc].reshape(rows, LANES) for cc in range(D_CHUNKS)], axis=1)


def _matrix_to_rows(y):
    rows = y.shape[0]
    t = jnp.stack([y[:, cc * LANES:(cc + 1) * LANES].reshape(rows // SUBLANES, SUBLANES, LANES)
                   for cc in range(D_CHUNKS)], axis=1)
    return _swap_sublane_major(t).reshape(rows, D_CHUNKS, LANES)


def _mm_kernel(x_ref, w_ref, o_ref):
    o_ref[...] = _dot(x_ref[...], w_ref[...]).astype(o_ref.dtype)


def _in_proj(hb, w_pack):
    tm, tn = 1024, 1792
    return pl.pallas_call(
        _mm_kernel,
        grid=(PACK_COLS // tn, N_TOK // tm),
        in_specs=[pl.BlockSpec((tm, D_MODEL), lambda j, i: (i, 0)),
                  pl.BlockSpec((D_MODEL, tn), lambda j, i: (0, j))],
        out_specs=pl.BlockSpec((tm, tn), lambda j, i: (i, j)),
        out_shape=jax.ShapeDtypeStruct((N_TOK, PACK_COLS), F32),
        compiler_params=_cparams(("parallel", "parallel")),
        name="in_proj",
    )(hb, w_pack)


def _rope128(x, c, s1, s2):
    return x * c + pltpu.roll(x, LANES - 16, 1) * s1 + pltpu.roll(x, 16, 1) * s2


def _q_up_kernel(ql_ref, g_ref, w_ref, c_ref, s1_ref, s2_ref, q_ref):
    y = _rms_norm(ql_ref[...], g_ref[...]).astype(BF16)
    q = _dot(y, w_ref[...])
    c, s1, s2 = c_ref[...], s1_ref[...], s2_ref[...]
    scale = (QK_NOPE + QK_ROPE) ** -0.5 * math.log2(math.e)
    for h in range(MLA_HEADS):
        qh = _rope128(q[:, h * HEAD_PAD:(h + 1) * HEAD_PAD], c, s1, s2)
        q_ref[h] = (qh * scale).astype(BF16)


def _q_up(proj, g, wq, tc, ts1, ts2):
    tm = 512
    spt = SEQ // tm
    tab = pl.BlockSpec((tm, LANES), lambda i: (i, 0))
    return pl.pallas_call(
        _q_up_kernel,
        grid=(N_TOK // tm,),
        in_specs=[pl.BlockSpec((tm, Q_RANK), lambda i: (i, COL_QLAT // Q_RANK)),
                  pl.BlockSpec((1, Q_RANK), lambda i: (0, 0)),
                  pl.BlockSpec((Q_RANK, MLA_HEADS * HEAD_PAD), lambda i: (0, 0)),
                  tab, tab, tab],
        out_specs=pl.BlockSpec((None, MLA_HEADS, tm, HEAD_PAD), lambda i: (i // spt, 0, i % spt, 0)),
        out_shape=jax.ShapeDtypeStruct((BATCH, MLA_HEADS, SEQ, HEAD_PAD), BF16),
        compiler_params=_cparams(("parallel",)),
        name="q_up",
    )(proj, g, wq, tc, ts1, ts2)


def _kv_up_kernel(ckv_ref, small_ref, g_ref, wk_ref, wv_ref, c_ref, s1_ref, s2_ref, k_ref, v_ref):
    y = _rms_norm(ckv_ref[...], g_ref[...]).astype(BF16)
    k_all = _dot(y, wk_ref[...])
    v_all = _dot(y, wv_ref[...])
    lane = lax.broadcasted_iota(I32, small_ref.shape, 1)
    kpe_raw = jnp.where((lane >= KPE_LANE) & (lane < KPE_LANE + QK_ROPE), small_ref[...], 0.0)
    kpe = _rope128(kpe_raw, c_ref[...], s1_ref[...], s2_ref[...])
    for h in range(MLA_HEADS):
        k_ref[h] = (k_all[:, h * HEAD_PAD:(h + 1) * HEAD_PAD] + kpe).astype(BF16)
    ones = jnp.ones((v_all.shape[0], LANES), F32)
    for j in range(N_PAIRS):
        v_ref[j] = jnp.concatenate([v_all[:, j * LANES:(j + 1) * LANES], ones], axis=1).astype(BF16)


def _kv_up(proj, g, wk, wv, tc, ts1, ts2):
    tm = 512
    spt = SEQ // tm
    tab = pl.BlockSpec((tm, LANES), lambda i: (i, 0))
    return pl.pallas_call(
        _kv_up_kernel,
        grid=(N_TOK // tm,),
        in_specs=[pl.BlockSpec((tm, KV_RANK), lambda i: (i, COL_CKV // KV_RANK)),
                  pl.BlockSpec((tm, LANES), lambda i: (i, COL_SMALL // LANES)),
                  pl.BlockSpec((1, KV_RANK), lambda i: (0, 0)),
                  pl.BlockSpec((KV_RANK, MLA_HEADS * HEAD_PAD), lambda i: (0, 0)),
                  pl.BlockSpec((KV_RANK, MLA_HEADS * V_DIM), lambda i: (0, 0)),
                  tab, tab, tab],
        out_specs=[pl.BlockSpec((None, MLA_HEADS, tm, HEAD_PAD), lambda i: (i // spt, 0, i % spt, 0)),
                   pl.BlockSpec((None, N_PAIRS, tm, V_SLAB), lambda i: (i // spt, 0, i % spt, 0))],
        out_shape=[jax.ShapeDtypeStruct((BATCH, MLA_HEADS, SEQ, HEAD_PAD), BF16),
                   jax.ShapeDtypeStruct((BATCH, N_PAIRS, SEQ, V_SLAB), BF16)],
        compiler_params=_cparams(("parallel",)),
        name="kv_up",
    )(proj, proj, g, wk, wv, tc, ts1, ts2)


ATT_T = 512


def _attn_kernel(q_ref, k_ref, v_ref, o_ref, m_ref, acc_ref):
    qi = pl.program_id(2)
    t = ATT_T
    m_ref[...] = jnp.full(m_ref.shape, -jnp.inf, F32)
    acc_ref[...] = jnp.zeros(acc_ref.shape, F32)

    def kv_tile(ks, width, masked):
        v = v_ref[pl.ds(ks, width), :]
        for hh in range(2):
            s = lax.dot_general(q_ref[hh], k_ref[hh, pl.ds(ks, width), :], (((1,), (1,)), ((), ())),
                                preferred_element_type=F32)
            if masked:
                row = lax.broadcasted_iota(I32, s.shape, 0)
                col = lax.broadcasted_iota(I32, s.shape, 1)
                s = jnp.where(ks + col <= qi * t + row, s, NEG)
            m_prev = m_ref[hh]
            m_new = jnp.maximum(m_prev, jnp.max(s, axis=-1, keepdims=True))
            alpha = jnp.exp2(m_prev - m_new)
            p = jnp.exp2(s - jnp.concatenate([m_new] * (width // LANES), axis=1))
            acc_ref[hh] = jnp.concatenate([alpha, alpha], axis=1) * acc_ref[hh] + _dot(p.astype(BF16), v)
            m_ref[hh] = m_new

    def body(kp, carry):
        kv_tile(pl.multiple_of(kp * 2 * t, 2 * t), 2 * t, False)
        return carry

    lax.fori_loop(0, qi // 2, body, 0)

    @pl.when(qi % 2 == 0)
    def _():
        kv_tile(pl.multiple_of(qi * t, t), t, True)

    @pl.when(qi % 2 == 1)
    def _():
        kv_tile(pl.multiple_of((qi - 1) * t, t), 2 * t, True)

    lane = lax.broadcasted_iota(I32, o_ref.shape, 1)
    a0 = acc_ref[0]
    a1 = acc_ref[1]
    o0 = a0[:, :LANES] / a0[:, LANES:]
    o1 = a1[:, :LANES] / a1[:, LANES:]
    o_ref[...] = jnp.where(lane < V_DIM, o0, o1).astype(o_ref.dtype)


def _attention(q, k, v):
    t = ATT_T
    return pl.pallas_call(
        _attn_kernel,
        grid=(BATCH, N_PAIRS, SEQ // t),
        in_specs=[pl.BlockSpec((None, 2, t, HEAD_PAD), lambda b, j, qi: (b, j, qi, 0)),
                  pl.BlockSpec((None, 2, SEQ, HEAD_PAD), lambda b, j, qi: (b, j, 0, 0)),
                  pl.BlockSpec((None, None, SEQ, V_SLAB), lambda b, j, qi: (b, j, 0, 0))],
        out_specs=pl.BlockSpec((None, t, LANES), lambda b, j, qi: (b, qi, j)),
        out_shape=jax.ShapeDtypeStruct((BATCH, SEQ, MLA_HEADS * V_DIM), BF16),
        scratch_shapes=[pltpu.VMEM((2, t, LANES), F32), pltpu.VMEM((2, t, V_SLAB), F32)],
        compiler_params=_cparams(("parallel", "parallel", "parallel")),
        name="mla_attention",
    )(q, k, v)


HALO = SUBLANES
HEADS_PER_GROUP = SSM_HEADS // SSM_GROUPS
GROUP_W = SSM_INNER // SSM_GROUPS


def _ssd_kernel(xbc_ref, halo_ref, z_ref, small_ref, cw_ref, cb_ref, dtb_ref, alog_ref, dexp_ref,
                nw_ref, e_ref, y_ref, st_ref, cat_ref):
    c = pl.program_id(1)
    L = SSM_CHUNK

    @pl.when(c == 0)
    def _():
        st_ref[...] = jnp.zeros(st_ref.shape, F32)

    cat_ref[0:HALO, :] = jnp.where(c == 0, 0.0, halo_ref[...])
    cat_ref[HALO:HALO + L, :] = xbc_ref[...]
    acc = jnp.broadcast_to(cb_ref[...], (L, SSM_CONV_DIM))
    for kk in range(SSM_CONV):
        off = HALO - (SSM_CONV - 1) + kk
        acc = acc + cw_ref[kk:kk + 1, :] * cat_ref[off:off + L, :]
    xc = _silu(acc)
    xs = xc[:, :SSM_INNER]
    bm = xc[:, SSM_INNER:SSM_INNER + SSM_GROUPS * SSM_STATE]
    cm = xc[:, SSM_INNER + SSM_GROUPS * SSM_STATE:]

    lane = lax.broadcasted_iota(I32, (L, LANES), 1)
    dt_lanes = (lane >= DT_LANE) & (lane < DT_LANE + SSM_HEADS)
    dt_in = small_ref[...] + dtb_ref[...]
    dt = jnp.maximum(dt_in, 0.0) + jnp.log1p(jnp.exp(-jnp.abs(dt_in)))
    dt = jnp.where(dt_lanes, dt, 0.0)
    a = -jnp.exp(alog_ref[...])
    a_dt = dt * a
    row = lax.broadcasted_iota(I32, (L, L), 0)
    col = lax.broadcasted_iota(I32, (L, L), 1)
    tri = row >= col
    a_cum = jnp.dot(tri.astype(F32), a_dt, precision=HIGHEST, preferred_element_type=F32)
    a_last = a_cum[L - 1:L, :]
    t = jnp.concatenate([dt, jnp.exp(a_cum), jnp.exp(a_last - a_cum)], axis=0)
    t_hi = t.astype(BF16)
    r1 = t - t_hi.astype(F32)
    t_mid = r1.astype(BF16)
    t_lo = (r1 - t_mid.astype(F32)).astype(BF16)
    expanded = _dot(jnp.concatenate([t_hi, t_mid, t_lo], axis=1), e_ref[...])
    x_dt = xs * expanded[0:L]
    eac_x = expanded[L:2 * L]
    ds_x = expanded[2 * L:3 * L]
    cd_x = eac_x[L - 1:L, :]
    xd = (x_dt * ds_x).astype(BF16)
    x_bf = x_dt.astype(BF16)
    a_cum_t = a_cum.T
    lane_h = lax.broadcasted_iota(I32, (L, LANES), 1)

    y_parts = []
    for g in range(SSM_GROUPS):
        bg = bm[:, g * SSM_STATE:(g + 1) * SSM_STATE].astype(BF16)
        cg = cm[:, g * SSM_STATE:(g + 1) * SSM_STATE].astype(BF16)
        cb = lax.dot_general(cg, bg, (((1,), (1,)), ((), ())), preferred_element_type=F32)
        st_prev = st_ref[g]
        y_off = _dot(cg, st_prev.astype(BF16)) * eac_x[:, g * GROUP_W:(g + 1) * GROUP_W]
        for jp in range(HEADS_PER_GROUP // 2):
            pair = g * (HEADS_PER_GROUP // 2) + jp
            xp = x_bf[:, pair * LANES:(pair + 1) * LANES]
            outs = []
            for hh in range(2):
                hl = DT_LANE + 2 * pair + hh
                seg = a_cum[:, hl:hl + 1] - a_cum_t[hl:hl + 1, :]
                decay = jnp.where(tri, jnp.exp(seg), 0.0)
                outs.append(_dot((cb * decay).astype(BF16), xp))
            y_parts.append(jnp.where(lane_h < SSM_HEADDIM, outs[0], outs[1])
                           + y_off[:, jp * LANES:(jp + 1) * LANES])
        upd = lax.dot_general(bg, xd[:, g * GROUP_W:(g + 1) * GROUP_W], (((0,), (0,)), ((), ())),
                              preferred_element_type=F32)
        st_ref[g] = st_prev * cd_x[:, g * GROUP_W:(g + 1) * GROUP_W] + upd

    y = jnp.concatenate(y_parts, axis=1) + xs * dexp_ref[...]
    y = y * _silu(z_ref[...])
    normed = []
    for g in range(SSM_GROUPS):
        yg = y[:, g * GROUP_W:(g + 1) * GROUP_W]
        normed.append(yg * lax.rsqrt(jnp.mean(yg * yg, axis=-1, keepdims=True) + RMS_EPS))
    y_ref[...] = (jnp.concatenate(normed, axis=1) * nw_ref[...]).astype(y_ref.dtype)


def _ssd(proj, cw, cb, dtb, alog, dexp, nw, e_mat):
    L = SSM_CHUNK
    nc = SEQ // L
    xbc_blk = COL_XBC // SSM_CONV_DIM

    def row1(w):
        return pl.BlockSpec((1, w), lambda b, c: (0, 0))

    return pl.pallas_call(
        _ssd_kernel,
        grid=(BATCH, nc),
        in_specs=[pl.BlockSpec((L, SSM_CONV_DIM), lambda b, c: (b * nc + c, xbc_blk)),
                  pl.BlockSpec((HALO, SSM_CONV_DIM),
                               lambda b, c: (jnp.maximum((b * nc + c) * (L // HALO) - 1, 0), xbc_blk)),
                  pl.BlockSpec((L, SSM_INNER), lambda b, c: (b * nc + c, COL_Z // SSM_INNER)),
                  pl.BlockSpec((L, LANES), lambda b, c: (b * nc + c, COL_SMALL // LANES)),
                  pl.BlockSpec((SSM_CONV, SSM_CONV_DIM), lambda b, c: (0, 0)),
                  row1(SSM_CONV_DIM), row1(LANES), row1(LANES), row1(SSM_INNER), row1(SSM_INNER),
                  pl.BlockSpec((3 * LANES, SSM_INNER), lambda b, c: (0, 0))],
        out_specs=pl.BlockSpec((L, SSM_INNER), lambda b, c: (b * nc + c, 0)),
        out_shape=jax.ShapeDtypeStruct((N_TOK, SSM_INNER), BF16),
        scratch_shapes=[pltpu.VMEM((SSM_GROUPS, SSM_STATE, GROUP_W), F32),
                        pltpu.VMEM((HALO + L, SSM_CONV_DIM), F32)],
        compiler_params=_cparams(("parallel", "arbitrary")),
        name="mamba2_ssd",
    )(proj, proj, proj, proj, cw, cb, dtb, alog, dexp, nw, e_mat)


def _merge_kernel(attn_ref, ssm_ref, ga_ref, gb_ref, h_ref, wa_ref, ws_ref, wo_ref, g_ref, b_ref,
                  hf_ref, hb_ref):
    ya = _dot(attn_ref[...], wa_ref[...])
    yb = _dot(ssm_ref[...], ws_ref[...])
    mix = _sigmoid(ga_ref[...]) * ya + _sigmoid(gb_ref[...]) * yb
    mixed = _dot(mix.astype(BF16), wo_ref[...])
    h1 = _layer_norm(ALPHA * h_ref[...] + mixed, g_ref[...], b_ref[...])
    hf_ref[...] = h1
    hb_ref[...] = h1.astype(BF16)


def _merge(attn, ssm_y, proj, hf, wa, ws, wo, g, b):
    tm = 512
    full = lambda r, c: pl.BlockSpec((r, c), lambda i: (0, 0))
    tile = lambda c, j=0: pl.BlockSpec((tm, c), lambda i: (i, j))
    return pl.pallas_call(
        _merge_kernel,
        grid=(N_TOK // tm,),
        in_specs=[tile(MLA_HEADS * V_DIM), tile(SSM_INNER), tile(D_MODEL, COL_GATE_A // D_MODEL),
                  tile(D_MODEL, COL_GATE_B // D_MODEL), tile(D_MODEL),
                  full(MLA_HEADS * V_DIM, D_MODEL), full(SSM_INNER, D_MODEL), full(D_MODEL, D_MODEL),
                  full(1, D_MODEL), full(1, D_MODEL)],
        out_specs=[tile(D_MODEL), tile(D_MODEL)],
        out_shape=[jax.ShapeDtypeStruct((N_TOK, D_MODEL), F32),
                   jax.ShapeDtypeStruct((N_TOK, D_MODEL), BF16)],
        compiler_params=_cparams(("parallel",)),
        name="merge_ln1",
    )(attn, ssm_y, proj, proj, hf, wa, ws, wo, g, b)


def _first_argmax(vals, lane):
    m = jnp.max(vals, axis=-1, keepdims=True)
    idx = jnp.min(jnp.where(vals == m, lane, float(N_EXPERTS)), axis=-1, keepdims=True)
    return m, idx


def _post_kernel(hf_ref, hb_ref, p_ref, wr_ref, rb_ref, wsg_ref, wsu_ref, wsd_ref, wp_ref, wpg_ref,
                 base_ref, idx_ref, wt_ref, cnt_ref):
    hf = hf_ref[...]
    hb = hb_ref[...]
    tm = hf.shape[0]
    h_mid = (hf - hb.astype(F32)).astype(BF16)
    logits = _dot(jnp.concatenate([hb, h_mid, hb], axis=1), wr_ref[...])
    scores = _sigmoid(logits)
    sel = scores + rb_ref[...]
    lane_i = lax.broadcasted_iota(I32, (tm, N_EXPERTS), 1)
    grp = lane_i // PER_GROUP
    lane = lane_i.astype(F32)

    grp_scores = []
    for g in range(N_GROUPS):
        vals = jnp.where(grp == g, sel, -jnp.inf)
        m1, i1 = _first_argmax(vals, lane)
        m2 = jnp.max(jnp.where(lane == i1, -jnp.inf, vals), axis=-1, keepdims=True)
        grp_scores.append(m1 + m2)
    ranks = [jnp.zeros((tm, 1), I32) for _ in range(N_GROUPS)]
    for g in range(N_GROUPS):
        for o in range(g + 1, N_GROUPS):
            later_wins = (grp_scores[o] > grp_scores[g]).astype(I32)
            ranks[g] = ranks[g] + later_wins
            ranks[o] = ranks[o] + (1 - later_wins)
    keep = jnp.zeros((tm, N_EXPERTS), jnp.bool_)
    for g in range(N_GROUPS):
        keep = keep | ((grp == g) & (ranks[g] < TOPK_GROUPS))
    masked = jnp.where(keep, sel, -jnp.inf)

    lane_k = lax.broadcasted_iota(I32, (tm, TOP_K), 1)
    idx_out = jnp.zeros((tm, TOP_K), I32)
    w_out = jnp.zeros((tm, TOP_K), F32)
    chosen = jnp.zeros((tm, N_EXPERTS), F32)
    for kk in range(TOP_K):
        _, ik = _first_argmax(masked, lane)
        hit = lane == ik
        wk = jnp.sum(jnp.where(hit, scores, 0.0), axis=-1, keepdims=True)
        masked = jnp.where(hit, -jnp.inf, masked)
        chosen = jnp.where(hit, 1.0, chosen)
        idx_out = jnp.where(lane_k == kk, ik.astype(I32), idx_out)
        w_out = jnp.where(lane_k == kk, wk, w_out)
    w_out = w_out / jnp.sum(w_out, axis=-1, keepdims=True) * ROUTED_SCALE
    idx_ref[...] = idx_out
    wt_ref[...] = w_out
    cnt_ref[...] = jnp.sum(chosen, axis=0, keepdims=True)

    shared = _dot((_silu(_dot(hb, wsg_ref[...])) * _dot(hb, wsu_ref[...])).astype(BF16), wsd_ref[...])
    ple = _dot(p_ref[...].astype(BF16), wp_ref[...]) * _sigmoid(_dot(hb, wpg_ref[...]))
    base_ref[...] = ALPHA * hf + shared + ple


def _post(hf, hb, p, wr, rb, wsg, wsu, wsd, wp, wpg):
    tm = ROUTE_T
    full = lambda r, c: pl.BlockSpec((r, c), lambda i: (0, 0))
    tile = lambda c: pl.BlockSpec((tm, c), lambda i: (i, 0))
    return pl.pallas_call(
        _post_kernel,
        grid=(N_RTILES,),
        in_specs=[tile(D_MODEL), tile(D_MODEL), tile(PLE_DIM),
                  full(3 * D_MODEL, N_EXPERTS), full(1, N_EXPERTS),
                  full(D_MODEL, EXPERT_FF), full(D_MODEL, EXPERT_FF), full(EXPERT_FF, D_MODEL),
                  full(PLE_DIM, D_MODEL), full(D_MODEL, D_MODEL)],
        out_specs=[tile(D_MODEL), tile(TOP_K), tile(TOP_K),
                   pl.BlockSpec((None, 1, N_EXPERTS), lambda i: (i, 0, 0))],
        out_shape=[jax.ShapeDtypeStruct((N_TOK, D_MODEL), F32),
                   jax.ShapeDtypeStruct((N_TOK, TOP_K), I32),
                   jax.ShapeDtypeStruct((N_TOK, TOP_K), F32),
                   jax.ShapeDtypeStruct((N_RTILES, 1, N_EXPERTS), F32)],
        compiler_params=_cparams(("parallel",)),
        name="router_shared_ple",
    )(hf, hb, p, wr, rb, wsg, wsu, wsd, wp, wpg)


def _pos_kernel(idx_ref, base_ref, pos_ref):
    idx = idx_ref[...]
    tm = idx.shape[0]
    lane = lax.broadcasted_iota(I32, (tm, N_EXPERTS), 1)
    hits = [lane == idx[:, kk:kk + 1] for kk in range(TOP_K)]
    chosen = hits[0]
    for kk in range(1, TOP_K):
        chosen = chosen | hits[kk]
    r = lax.broadcasted_iota(I32, (tm, tm), 0)
    c = lax.broadcasted_iota(I32, (tm, tm), 1)
    earlier = jnp.where(c < r, 1.0, 0.0).astype(BF16)
    rank = _dot(earlier, jnp.where(chosen, 1.0, 0.0).astype(BF16))
    slot = rank + base_ref[...]
    lane_k = lax.broadcasted_iota(I32, (tm, TOP_K), 1)
    out = jnp.zeros((tm, TOP_K), I32)
    for kk in range(TOP_K):
        pk = jnp.sum(jnp.where(hits[kk], slot, 0.0), axis=-1, keepdims=True)
        out = jnp.where(lane_k == kk, pk.astype(I32), out)
    pos_ref[...] = out


def _slot_positions(idx, base):
    tm = ROUTE_T
    return pl.pallas_call(
        _pos_kernel,
        grid=(N_RTILES,),
        in_specs=[pl.BlockSpec((tm, TOP_K), lambda i: (i, 0)),
                  pl.BlockSpec((None, 1, N_EXPERTS), lambda i: (i, 0, 0))],
        out_specs=pl.BlockSpec((tm, TOP_K), lambda i: (i, 0)),
        out_shape=jax.ShapeDtypeStruct((N_TOK, TOP_K), I32),
        compiler_params=_cparams(("parallel",)),
        name="slot_positions",
    )(idx, base)


def _slot_layout(cnt):
    counts_te = cnt.reshape(N_RTILES, N_EXPERTS).astype(I32)
    counts = jnp.sum(counts_te, axis=0)
    padded = (counts + SLOT_BLOCK - 1) // SLOT_BLOCK * SLOT_BLOCK
    pad_end = jnp.cumsum(padded)
    pad_start = pad_end - padded
    tile_off = jnp.cumsum(counts_te, axis=0) - counts_te
    base = (pad_start[None, :] + tile_off).astype(F32).reshape(N_RTILES, 1, N_EXPERTS)
    n_used = (pad_end[-1] // SLOT_BLOCK).astype(I32).reshape(1)
    return base, counts, pad_start, padded, n_used


DISP_T = 512


def _dispatch_kernel(cnt_ref, ps_ref, pd_ref, nu_ref, pos_hbm, h3_ref, x_hbm,
                     pos_smem, zbuf, pos_sem, row_sem, pad_sem):
    i = pl.program_id(0)
    n = pl.num_programs(0)

    def pos_copy(blk, slot):
        return pltpu.make_async_copy(pos_hbm.at[blk], pos_smem.at[slot], pos_sem.at[slot])

    def pad_row(e, r):
        return pltpu.make_async_copy(zbuf.at[0], x_hbm.at[ps_ref[e] + r], pad_sem)

    def pad_block(b):
        return pltpu.make_async_copy(zbuf, x_hbm.at[pl.ds(b * SLOT_BLOCK, SLOT_BLOCK)], pad_sem)

    def for_each_pad(fn_row, fn_block):
        def per_expert(e, carry):
            def per_row(r, c2):
                fn_row(e, r)
                return c2
            return lax.fori_loop(cnt_ref[e], pd_ref[e], per_row, carry)
        lax.fori_loop(0, N_EXPERTS, per_expert, 0)

        def per_block(b, carry):
            fn_block(b)
            return carry
        lax.fori_loop(nu_ref[0], N_BLOCKS, per_block, 0)

    @pl.when(i == 0)
    def _():
        pos_copy(0, 0).start()
        pos_copy(1, 1).start()
        zbuf[...] = jnp.zeros(zbuf.shape, F32)
        for_each_pad(lambda e, r: pad_row(e, r).start(), lambda b: pad_block(b).start())
        for_each_pad(lambda e, r: pad_row(e, r).wait(), lambda b: pad_block(b).wait())

    slot = i % 2
    pos_copy(i, slot).wait()

    def body(t, carry):
        for kk in range(TOP_K):
            dst = pos_smem[slot, t * TOP_K + kk]
            pltpu.make_async_copy(h3_ref.at[t], x_hbm.at[dst], row_sem).start(priority=kk % 2)
        return carry
    lax.fori_loop(0, DISP_T, body, 0)

    @pl.when(i + 2 < n)
    def _():
        pos_copy(i + 2, slot).start()

    for kk in range(TOP_K):
        pltpu.make_async_copy(h3_ref, x_hbm.at[pl.ds(0, DISP_T)], row_sem).wait()


def _dispatch(counts, pad_start, padded, n_used, pos, h3):
    n_tiles = N_TOK // DISP_T
    grid_spec = pltpu.PrefetchScalarGridSpec(
        num_scalar_prefetch=4,
        grid=(n_tiles,),
        in_specs=[pl.BlockSpec(memory_space=pl.ANY),
                  pl.BlockSpec((DISP_T, D_CHUNKS, LANES), lambda i, *_: (i, 0, 0))],
        out_specs=pl.BlockSpec(memory_space=pl.ANY),
        scratch_shapes=[pltpu.SMEM((2, DISP_T * TOP_K), I32),
                        pltpu.VMEM((SLOT_BLOCK, D_CHUNKS, LANES), F32),
                        pltpu.SemaphoreType.DMA((2,)),
                        pltpu.SemaphoreType.DMA,
                        pltpu.SemaphoreType.DMA])
    return pl.pallas_call(
        _dispatch_kernel,
        grid_spec=grid_spec,
        out_shape=jax.ShapeDtypeStruct((CAP, D_CHUNKS, LANES), F32),
        compiler_params=_cparams(("arbitrary",)),
        name="dispatch_scatter",
    )(counts, pad_start, padded, n_used, pos.reshape(n_tiles, DISP_T * TOP_K), h3)


def _moe_kernel(fb_ref, nb_ref, nu_ref, x_hbm, wg_ref, wu_ref, wd_ref, y_hbm,
                xbuf, ybuf, wg_bf, wu_bf, wd_bf, in_sem, out_sem):
    e = pl.program_id(0)
    first = fb_ref[e]
    nb = nb_ref[e]
    n_used = nu_ref[0]

    def rows(g):
        return pl.ds(pl.multiple_of(g * SLOT_BLOCK, SLOT_BLOCK), SLOT_BLOCK)

    def x_copy(g):
        slot = g % MOE_RING
        return pltpu.make_async_copy(x_hbm.at[rows(g)], xbuf.at[slot], in_sem.at[slot])

    def y_copy(g):
        slot = g % MOE_RING
        return pltpu.make_async_copy(ybuf.at[slot], y_hbm.at[rows(g)], out_sem.at[slot])

    @pl.when(e == 0)
    def _():
        for g in range(MOE_RING - 2):
            x_copy(g).start()

    def run_blocks(g, count):
        for j in range(count):
            x_copy(g + j).wait()
        for j in range(count):
            nxt = g + j + MOE_RING - 2

            @pl.when(nxt < n_used)
            def _():
                x_copy(nxt).start()

        x3 = jnp.concatenate([xbuf[(g + j) % MOE_RING] for j in range(count)], axis=0)
        x = _rows_to_matrix(x3).astype(BF16)
        hid = (_silu(_dot(x, wg_bf[...])) * _dot(x, wu_bf[...])).astype(BF16)
        y3 = _matrix_to_rows(_dot(hid, wd_bf[...]))
        for j in range(count):
            @pl.when(g + j >= MOE_RING)
            def _():
                y_copy(g + j - MOE_RING).wait()

            ybuf[(g + j) % MOE_RING] = y3[j * SLOT_BLOCK:(j + 1) * SLOT_BLOCK]
            y_copy(g + j).start()

    @pl.when(nb > 0)
    def _():
        wg_bf[...] = wg_ref[...].astype(BF16)
        wu_bf[...] = wu_ref[...].astype(BF16)
        wd_bf[...] = wd_ref[...].astype(BF16)

        def pair(p, carry):
            run_blocks(first + 2 * p, 2)
            return carry

        lax.fori_loop(0, nb // 2, pair, 0)

        @pl.when(nb % 2 == 1)
        def _():
            run_blocks(first + nb - 1, 1)

    @pl.when(e == N_EXPERTS - 1)
    def _():
        for j in range(MOE_RING):
            y_copy(n_used - 1 - j).wait()
        ybuf[0] = jnp.zeros(ybuf.shape[1:], F32)

        def tail_copy(b):
            return pltpu.make_async_copy(
                ybuf.at[0], y_hbm.at[pl.ds(pl.multiple_of(b * SLOT_BLOCK, SLOT_BLOCK), SLOT_BLOCK)],
                out_sem.at[0])

        def start(b, carry):
            tail_copy(b).start()
            return carry

        def wait(b, carry):
            tail_copy(b).wait()
            return carry

        lax.fori_loop(nu_ref[0], N_BLOCKS, start, 0)
        lax.fori_loop(nu_ref[0], N_BLOCKS, wait, 0)


def _moe(first_block, n_blocks, n_used, xs, wg, wu, wd, layer):
    wspec = lambda r, c: pl.BlockSpec((None, None, r, c), lambda e, *_: (layer, e, 0, 0))
    grid_spec = pltpu.PrefetchScalarGridSpec(
        num_scalar_prefetch=3,
        grid=(N_EXPERTS,),
        in_specs=[pl.BlockSpec(memory_space=pl.ANY),
                  wspec(D_MODEL, EXPERT_FF), wspec(D_MODEL, EXPERT_FF), wspec(EXPERT_FF, D_MODEL)],
        out_specs=pl.BlockSpec(memory_space=pl.ANY),
        scratch_shapes=[pltpu.VMEM((MOE_RING, SLOT_BLOCK, D_CHUNKS, LANES), F32),
                        pltpu.VMEM((MOE_RING, SLOT_BLOCK, D_CHUNKS, LANES), F32),
                        pltpu.VMEM((D_MODEL, EXPERT_FF), BF16),
                        pltpu.VMEM((D_MODEL, EXPERT_FF), BF16),
                        pltpu.VMEM((EXPERT_FF, D_MODEL), BF16),
                        pltpu.SemaphoreType.DMA((MOE_RING,)),
                        pltpu.SemaphoreType.DMA((MOE_RING,))])
    return pl.pallas_call(
        _moe_kernel,
        grid_spec=grid_spec,
        out_shape=jax.ShapeDtypeStruct((CAP, D_CHUNKS, LANES), F32),
        compiler_params=_cparams(("arbitrary",)),
        name="routed_experts",
    )(first_block, n_blocks, n_used, xs, wg, wu, wd)


COMB_T = 128


def _combine_kernel(pos_hbm, y_hbm, w_ref, base_ref, g_ref, b_ref, hf_ref, hb_ref,
                    pos_smem, buf, routed_ref, pos_sem, row_sem):
    i = pl.program_id(0)
    n = pl.num_programs(0)
    slot = i % 2
    nslot = (i + 1) % 2
    n_groups = COMB_T // SUBLANES

    def pos_copy(blk, s):
        return pltpu.make_async_copy(pos_hbm.at[blk], pos_smem.at[s], pos_sem.at[s])

    def issue_group(s, g):
        for j in range(SUBLANES):
            t = g * SUBLANES + j
            for kk in range(TOP_K):
                src = pos_smem[s, t * TOP_K + kk]
                pltpu.make_async_copy(y_hbm.at[src], buf.at[s, t, kk],
                                      row_sem.at[s]).start(priority=kk % 2)

    def reduce_group(s, g):
        rows8 = pl.ds(pl.multiple_of(g * SUBLANES, SUBLANES), SUBLANES)
        acc = None
        for kk in range(TOP_K):
            term = w_ref[rows8, kk:kk + 1] * _rows_to_matrix(buf[s, rows8, kk])
            acc = term if acc is None else acc + term
        routed_ref[rows8, :] = acc

    def group_loop(fn):
        def body(g, carry):
            fn(g)
            return carry
        lax.fori_loop(0, n_groups, body, 0)

    @pl.when(i == 0)
    def _():
        pos_copy(0, 0).start()
        pos_copy(1, 1).start()
        pos_copy(0, 0).wait()
        group_loop(lambda g: issue_group(0, g))

    for kk in range(TOP_K):
        pltpu.make_async_copy(y_hbm.at[pl.ds(0, COMB_T)], buf.at[slot, :, kk], row_sem.at[slot]).wait()

    @pl.when(i + 1 < n)
    def _():
        pos_copy(i + 1, nslot).wait()

        def both(g):
            issue_group(nslot, g)
            reduce_group(slot, g)
        group_loop(both)

    @pl.when(i + 1 == n)
    def _():
        group_loop(lambda g: reduce_group(slot, g))

    @pl.when(i + 2 < n)
    def _():
        pos_copy(i + 2, slot).start()

    h2 = _layer_norm(base_ref[...] + routed_ref[...], g_ref[...], b_ref[...])
    hf_ref[...] = h2
    hb_ref[...] = h2.astype(BF16)


def _combine(pos, y3, wts, base, g, b):
    tm = COMB_T
    n_tiles = N_TOK // tm
    full = lambda r, c: pl.BlockSpec((r, c), lambda i: (0, 0))
    tile = lambda c: pl.BlockSpec((tm, c), lambda i: (i, 0))
    return pl.pallas_call(
        _combine_kernel,
        grid=(n_tiles,),
        in_specs=[pl.BlockSpec(memory_space=pl.ANY), pl.BlockSpec(memory_space=pl.ANY),
                  tile(TOP_K), tile(D_MODEL), full(1, D_MODEL), full(1, D_MODEL)],
        out_specs=[tile(D_MODEL), tile(D_MODEL)],
        out_shape=[jax.ShapeDtypeStruct((N_TOK, D_MODEL), F32),
                   jax.ShapeDtypeStruct((N_TOK, D_MODEL), BF16)],
        scratch_shapes=[pltpu.SMEM((2, tm * TOP_K), I32),
                        pltpu.VMEM((2, tm, TOP_K, D_CHUNKS, LANES), F32),
                        pltpu.VMEM((tm, D_MODEL), F32),
                        pltpu.SemaphoreType.DMA((2,)),
                        pltpu.SemaphoreType.DMA((2,))],
        compiler_params=_cparams(("arbitrary",)),
        name="combine_ln2",
    )(pos.reshape(n_tiles, tm * TOP_K), y3, wts, base, g, b)


def _pack_w_in(w):
    q_lat = w[:, 0:384]
    c_kv = w[:, 384:640]
    k_pe = w[:, 640:672]
    z = w[:, 672:1696]
    xbc = w[:, 1696:3232]
    dt = w[:, 3232:3248]
    gate_a = w[:, 3248:4272]
    gate_b = w[:, 4272:5296]
    zeros = lambda n: jnp.zeros((D_MODEL, n), w.dtype)
    small = jnp.concatenate([zeros(KPE_LANE), k_pe, dt, zeros(LANES - DT_LANE - SSM_HEADS)], axis=1)
    return jnp.concatenate([gate_a, gate_b, z, xbc, q_lat, small, c_kv], axis=1).astype(BF16)


def _pack_w_q(w):
    w = w.reshape(Q_RANK, MLA_HEADS, QK_NOPE + QK_ROPE)
    w = jnp.pad(w, ((0, 0), (0, 0), (0, HEAD_PAD - QK_NOPE - QK_ROPE)))
    return w.reshape(Q_RANK, MLA_HEADS * HEAD_PAD).astype(BF16)


def _pack_w_kv(w):
    w = w.reshape(KV_RANK, MLA_HEADS, QK_NOPE + V_DIM)
    wk = jnp.pad(w[:, :, :QK_NOPE], ((0, 0), (0, 0), (0, HEAD_PAD - QK_NOPE)))
    wv = w[:, :, QK_NOPE:]
    return (wk.reshape(KV_RANK, MLA_HEADS * HEAD_PAD).astype(BF16),
            wv.reshape(KV_RANK, MLA_HEADS * V_DIM).astype(BF16))


def _rope_tables(positions):
    half = QK_ROPE // 2
    inv_freq = jnp.power(ROPE_THETA, -jnp.arange(half, dtype=F32) * (2.0 / QK_ROPE))
    ang = positions.astype(F32).reshape(N_TOK, 1) * inv_freq
    cos, sin = jnp.cos(ang), jnp.sin(ang)
    z = lambda n: jnp.zeros((N_TOK, n), F32)
    tail = HEAD_PAD - QK_NOPE - QK_ROPE
    tc = jnp.concatenate([jnp.ones((N_TOK, QK_NOPE), F32), cos, cos, z(tail)], axis=1)
    ts1 = jnp.concatenate([z(QK_NOPE), -sin, z(half), z(tail)], axis=1)
    ts2 = jnp.concatenate([z(QK_NOPE), z(half), sin, z(tail)], axis=1)
    return tc, ts1, ts2


def _router_pieces(w):
    w_hi = lax.reduce_precision(w, exponent_bits=8, mantissa_bits=7)
    w_mid = (w - w_hi).astype(BF16)
    w_hi = w_hi.astype(BF16)
    return jnp.concatenate([w_hi, w_hi, w_mid], axis=0)


def _head_lane_row(v):
    return jnp.zeros((1, LANES), F32).at[0, DT_LANE:DT_LANE + SSM_HEADS].set(v.astype(F32))


def _expand_matrix():
    r = jnp.arange(LANES)[:, None]
    c = jnp.arange(SSM_INNER)[None, :]
    e = ((r - DT_LANE) == (c // SSM_HEADDIM)).astype(BF16)
    return jnp.concatenate([e, e, e], axis=0)


def kernel(x, p, positions, w_in, q_norm, w_q_up, kv_norm, w_kv_up, conv_w, conv_b, dt_bias, a_log, d_skip,
           ssm_norm, w_attn_br, w_ssm_br, w_o, ln1_g, ln1_b, w_router, router_bias, w_exp_gate, w_exp_up,
           w_exp_down, w_sh_gate, w_sh_up, w_sh_down, w_ple, w_ple_gate, ln2_g, ln2_b):
    tc, ts1, ts2 = _rope_tables(positions)
    e_mat = _expand_matrix()
    hf = x.reshape(N_TOK, D_MODEL)
    hb = hf.astype(BF16)
    row = lambda v: v.reshape(1, -1).astype(F32)
    for i in range(DEPTH):
        proj = _in_proj(hb, _pack_w_in(w_in[i]))
        q = _q_up(proj, row(q_norm[i]), _pack_w_q(w_q_up[i]), tc, ts1, ts2)
        wk, wv = _pack_w_kv(w_kv_up[i])
        k, v = _kv_up(proj, row(kv_norm[i]), wk, wv, tc, ts1, ts2)
        attn = _attention(q, k, v).reshape(N_TOK, MLA_HEADS * V_DIM)
        ssm_y = _ssd(proj, conv_w[i], row(conv_b[i]), _head_lane_row(dt_bias[i]), _head_lane_row(a_log[i]),
                     row(jnp.repeat(d_skip[i], SSM_HEADDIM)), row(ssm_norm[i]), e_mat)
        hf, hb = _merge(attn, ssm_y, proj, hf, w_attn_br[i].astype(BF16), w_ssm_br[i].astype(BF16),
                        w_o[i].astype(BF16), row(ln1_g[i]), row(ln1_b[i]))
        base, idx, wts, cnt = _post(hf, hb, p[i].reshape(N_TOK, PLE_DIM), _router_pieces(w_router[i]),
                                    row(router_bias[i]),
                                    w_sh_gate[i].astype(BF16), w_sh_up[i].astype(BF16),
                                    w_sh_down[i].astype(BF16), w_ple[i].astype(BF16),
                                    w_ple_gate[i].astype(BF16))
        base_slots, counts, pad_start, padded, n_used = _slot_layout(cnt)
        pos = _slot_positions(idx, base_slots)
        xs = _dispatch(counts, pad_start, padded, n_used, pos, hf.reshape(N_TOK, D_CHUNKS, LANES))
        y3 = _moe(pad_start // SLOT_BLOCK, padded // SLOT_BLOCK, n_used, xs,
                  w_exp_gate, w_exp_up, w_exp_down, i)
        hf, hb = _combine(pos, y3, wts, base, row(ln2_g[i]), row(ln2_b[i]))
    return hf.reshape(BATCH, SEQ, D_MODEL)
```

```python
import math

import jax
import jax.numpy as jnp
from jax import lax
from jax.experimental import pallas as pl
from jax.experimental.pallas import tpu as pltpu

F32 = jnp.float32
BF16 = jnp.bfloat16
I32 = jnp.int32
HIGHEST = lax.Precision.HIGHEST

D_MODEL = 1024
BATCH = 4
SEQ = 8192
DEPTH = 2
N_TOK = BATCH * SEQ
MLA_HEADS = 8
QK_NOPE = 64
QK_ROPE = 32
V_DIM = 64
Q_RANK = 384
KV_RANK = 256
ROPE_THETA = 10000.0
SSM_INNER = 1024
SSM_HEADDIM = 64
SSM_HEADS = 16
SSM_GROUPS = 2
SSM_STATE = 128
SSM_CONV = 4
SSM_CHUNK = 128
SSM_CONV_DIM = 1536
N_EXPERTS = 256
TOP_K = 8
N_GROUPS = 8
TOPK_GROUPS = 4
PER_GROUP = N_EXPERTS // N_GROUPS
EXPERT_FF = 256
ROUTED_SCALE = 2.5
PLE_DIM = 256
ALPHA = (2 * DEPTH) ** 0.25
LN_EPS = 1e-5
RMS_EPS = 1e-6

LANES = 128
SUBLANES = 8
VMEM_LIMIT = 48 * 1024 * 1024

COL_GATE_A = 0
COL_GATE_B = 1024
COL_Z = 2048
COL_XBC = 3072
COL_QLAT = 4608
COL_SMALL = 4992
COL_CKV = 5120
PACK_COLS = 5376
KPE_LANE = 64
DT_LANE = 96

HEAD_PAD = 128
N_PAIRS = MLA_HEADS // 2
V_SLAB = 2 * LANES

ROUTE_T = 256
N_RTILES = N_TOK // ROUTE_T
SLOT_BLOCK = 128
N_ASSIGN = N_TOK * TOP_K
N_BLOCKS = N_ASSIGN // SLOT_BLOCK + N_EXPERTS
CAP = N_BLOCKS * SLOT_BLOCK
D_CHUNKS = D_MODEL // LANES
MOE_RING = 6

NEG = float(jnp.finfo(jnp.float32).min)


def _cparams(sem):
    return pltpu.CompilerParams(dimension_semantics=sem, vmem_limit_bytes=VMEM_LIMIT)


def _sigmoid(x):
    return 1.0 / (1.0 + jnp.exp(-x))


def _silu(x):
    return x * _sigmoid(x)


def _layer_norm(x, g, b):
    mu = jnp.mean(x, axis=-1, keepdims=True)
    xc = x - mu
    var = jnp.mean(xc * xc, axis=-1, keepdims=True)
    return xc * lax.rsqrt(var + LN_EPS) * g + b


def _rms_norm(x, g):
    return x * lax.rsqrt(jnp.mean(x * x, axis=-1, keepdims=True) + RMS_EPS) * g


def _dot(a, b):
    return jnp.dot(a, b, preferred_element_type=F32)


def _swap_sublane_major(x):
    groups = x.shape[0]
    for d in (4, 2, 1):
        y = x.reshape(groups, SUBLANES // (2 * d), 2, d, SUBLANES, LANES)
        lo, hi = y[:, :, 0], y[:, :, 1]
        shp = lo.shape
        keep = (lax.broadcasted_iota(I32, shp, 3) & d) == 0
        rot = lambda v, s: pltpu.roll(v.reshape(-1, SUBLANES, LANES), s, 1).reshape(shp)
        new_lo = jnp.where(keep, lo, rot(hi, d))
        new_hi = jnp.where(keep, rot(lo, SUBLANES - d), hi)
        x = jnp.stack([new_lo, new_hi], axis=2).reshape(groups, SUBLANES, SUBLANES, LANES)
    return x


def _rows_to_matrix(x3):
    rows = x3.shape[0]
    t = _swap_sublane_major(x3.reshape(rows // SUBLANES, SUBLANES, D_CHUNKS, LANES))
    return jnp.concatenate([t[:, cc].reshape(rows, LANES) for cc in range(D_CHUNKS)], axis=1)


def _matrix_to_rows(y):
    rows = y.shape[0]
    t = jnp.stack([y[:, cc * LANES:(cc + 1) * LANES].reshape(rows // SUBLANES, SUBLANES, LANES)
                   for cc in range(D_CHUNKS)], axis=1)
    return _swap_sublane_major(t).reshape(rows, D_CHUNKS, LANES)


def _mm_kernel(x_ref, w_ref, o_ref):
    o_ref[...] = _dot(x_ref[...], w_ref[...]).astype(o_ref.dtype)


def _in_proj(hb, w_pack):
    tm, tn = 1024, 1792
    return pl.pallas_call(
        _mm_kernel,
        grid=(PACK_COLS // tn, N_TOK // tm),
        in_specs=[pl.BlockSpec((tm, D_MODEL), lambda j, i: (i, 0)),
                  pl.BlockSpec((D_MODEL, tn), lambda j, i: (0, j))],
        out_specs=pl.BlockSpec((tm, tn), lambda j, i: (i, j)),
        out_shape=jax.ShapeDtypeStruct((N_TOK, PACK_COLS), F32),
        compiler_params=_cparams(("parallel", "parallel")),
        name="in_proj",
    )(hb, w_pack)


def _rope128(x, c, s1, s2):
    return x * c + pltpu.roll(x, LANES - 16, 1) * s1 + pltpu.roll(x, 16, 1) * s2


def _q_up_kernel(ql_ref, g_ref, w_ref, c_ref, s1_ref, s2_ref, q_ref):
    y = _rms_norm(ql_ref[...], g_ref[...]).astype(BF16)
    q = _dot(y, w_ref[...])
    c, s1, s2 = c_ref[...], s1_ref[...], s2_ref[...]
    scale = (QK_NOPE + QK_ROPE) ** -0.5 * math.log2(math.e)
    for h in range(MLA_HEADS):
        qh = _rope128(q[:, h * HEAD_PAD:(h + 1) * HEAD_PAD], c, s1, s2)
        q_ref[h] = (qh * scale).astype(BF16)


def _q_up(proj, g, wq, tc, ts1, ts2):
    tm = 512
    spt = SEQ // tm
    tab = pl.BlockSpec((tm, LANES), lambda i: (i, 0))
    return pl.pallas_call(
        _q_up_kernel,
        grid=(N_TOK // tm,),
        in_specs=[pl.BlockSpec((tm, Q_RANK), lambda i: (i, COL_QLAT // Q_RANK)),
                  pl.BlockSpec((1, Q_RANK), lambda i: (0, 0)),
                  pl.BlockSpec((Q_RANK, MLA_HEADS * HEAD_PAD), lambda i: (0, 0)),
                  tab, tab, tab],
        out_specs=pl.BlockSpec((None, MLA_HEADS, tm, HEAD_PAD), lambda i: (i // spt, 0, i % spt, 0)),
        out_shape=jax.ShapeDtypeStruct((BATCH, MLA_HEADS, SEQ, HEAD_PAD), BF16),
        compiler_params=_cparams(("parallel",)),
        name="q_up",
    )(proj, g, wq, tc, ts1, ts2)


def _kv_up_kernel(ckv_ref, small_ref, g_ref, wk_ref, wv_ref, c_ref, s1_ref, s2_ref, k_ref, v_ref):
    y = _rms_norm(ckv_ref[...], g_ref[...]).astype(BF16)
    k_all = _dot(y, wk_ref[...])
    v_all = _dot(y, wv_ref[...])
    lane = lax.broadcasted_iota(I32, small_ref.shape, 1)
    kpe_raw = jnp.where((lane >= KPE_LANE) & (lane < KPE_LANE + QK_ROPE), small_ref[...], 0.0)
    kpe = _rope128(kpe_raw, c_ref[...], s1_ref[...], s2_ref[...])
    for h in range(MLA_HEADS):
        k_ref[h] = (k_all[:, h * HEAD_PAD:(h + 1) * HEAD_PAD] + kpe).astype(BF16)
    ones = jnp.ones((v_all.shape[0], LANES), F32)
    for j in range(N_PAIRS):
        v_ref[j] = jnp.concatenate([v_all[:, j * LANES:(j + 1) * LANES], ones], axis=1).astype(BF16)


def _kv_up(proj, g, wk, wv, tc, ts1, ts2):
    tm = 512
    spt = SEQ // tm
    tab = pl.BlockSpec((tm, LANES), lambda i: (i, 0))
    return pl.pallas_call(
        _kv_up_kernel,
        grid=(N_TOK // tm,),
        in_specs=[pl.BlockSpec((tm, KV_RANK), lambda i: (i, COL_CKV // KV_RANK)),
                  pl.BlockSpec((tm, LANES), lambda i: (i, COL_SMALL // LANES)),
                  pl.BlockSpec((1, KV_RANK), lambda i: (0, 0)),
                  pl.BlockSpec((KV_RANK, MLA_HEADS * HEAD_PAD), lambda i: (0, 0)),
                  pl.BlockSpec((KV_RANK, MLA_HEADS * V_DIM), lambda i: (0, 0)),
                  tab, tab, tab],
        out_specs=[pl.BlockSpec((None, MLA_HEADS, tm, HEAD_PAD), lambda i: (i // spt, 0, i % spt, 0)),
                   pl.BlockSpec((None, N_PAIRS, tm, V_SLAB), lambda i: (i // spt, 0, i % spt, 0))],
        out_shape=[jax.ShapeDtypeStruct((BATCH, MLA_HEADS, SEQ, HEAD_PAD), BF16),
                   jax.ShapeDtypeStruct((BATCH, N_PAIRS, SEQ, V_SLAB), BF16)],
        compiler_params=_cparams(("parallel",)),
        name="kv_up",
    )(proj, proj, g, wk, wv, tc, ts1, ts2)


ATT_T = 512


def _attn_kernel(q_ref, k_ref, v_ref, o_ref, m_ref, acc_ref):
    qi = pl.program_id(2)
    t = ATT_T
    m_ref[...] = jnp.full(m_ref.shape, -jnp.inf, F32)
    acc_ref[...] = jnp.zeros(acc_ref.shape, F32)

    def kv_tile(ks, width, masked):
        v = v_ref[pl.ds(ks, width), :]
        for hh in range(2):
            s = lax.dot_general(q_ref[hh], k_ref[hh, pl.ds(ks, width), :], (((1,), (1,)), ((), ())),
                                preferred_element_type=F32)
            if masked:
                row = lax.broadcasted_iota(I32, s.shape, 0)
                col = lax.broadcasted_iota(I32, s.shape, 1)
                s = jnp.where(ks + col <= qi * t + row, s, NEG)
            m_prev = m_ref[hh]
            m_new = jnp.maximum(m_prev, jnp.max(s, axis=-1, keepdims=True))
            alpha = jnp.exp2(m_prev - m_new)
            p = jnp.exp2(s - jnp.concatenate([m_new] * (width // LANES), axis=1))
            acc_ref[hh] = jnp.concatenate([alpha, alpha], axis=1) * acc_ref[hh] + _dot(p.astype(BF16), v)
            m_ref[hh] = m_new

    def body(kp, carry):
        kv_tile(pl.multiple_of(kp * 2 * t, 2 * t), 2 * t, False)
        return carry

    lax.fori_loop(0, qi // 2, body, 0)

    @pl.when(qi % 2 == 0)
    def _():
        kv_tile(pl.multiple_of(qi * t, t), t, True)

    @pl.when(qi % 2 == 1)
    def _():
        kv_tile(pl.multiple_of((qi - 1) * t, t), 2 * t, True)

    lane = lax.broadcasted_iota(I32, o_ref.shape, 1)
    a0 = acc_ref[0]
    a1 = acc_ref[1]
    o0 = a0[:, :LANES] / a0[:, LANES:]
    o1 = a1[:, :LANES] / a1[:, LANES:]
    o_ref[...] = jnp.where(lane < V_DIM, o0, o1).astype(o_ref.dtype)


def _attention(q, k, v):
    t = ATT_T
    return pl.pallas_call(
        _attn_kernel,
        grid=(BATCH, N_PAIRS, SEQ // t),
        in_specs=[pl.BlockSpec((None, 2, t, HEAD_PAD), lambda b, j, qi: (b, j, qi, 0)),
                  pl.BlockSpec((None, 2, SEQ, HEAD_PAD), lambda b, j, qi: (b, j, 0, 0)),
                  pl.BlockSpec((None, None, SEQ, V_SLAB), lambda b, j, qi: (b, j, 0, 0))],
        out_specs=pl.BlockSpec((None, t, LANES), lambda b, j, qi: (b, qi, j)),
        out_shape=jax.ShapeDtypeStruct((BATCH, SEQ, MLA_HEADS * V_DIM), BF16),
        scratch_shapes=[pltpu.VMEM((2, t, LANES), F32), pltpu.VMEM((2, t, V_SLAB), F32)],
        compiler_params=_cparams(("parallel", "parallel", "parallel")),
        name="mla_attention",
    )(q, k, v)


HALO = SUBLANES
HEADS_PER_GROUP = SSM_HEADS // SSM_GROUPS
GROUP_W = SSM_INNER // SSM_GROUPS


def _ssd_kernel(xbc_ref, halo_ref, z_ref, small_ref, cw_ref, cb_ref, dtb_ref, alog_ref, dexp_ref,
                nw_ref, e_ref, y_ref, st_ref, cat_ref):
    c = pl.program_id(1)
    L = SSM_CHUNK

    @pl.when(c == 0)
    def _():
        st_ref[...] = jnp.zeros(st_ref.shape, F32)

    cat_ref[0:HALO, :] = jnp.where(c == 0, 0.0, halo_ref[...])
    cat_ref[HALO:HALO + L, :] = xbc_ref[...]
    acc = jnp.broadcast_to(cb_ref[...], (L, SSM_CONV_DIM))
    for kk in range(SSM_CONV):
        off = HALO - (SSM_CONV - 1) + kk
        acc = acc + cw_ref[kk:kk + 1, :] * cat_ref[off:off + L, :]
    xc = _silu(acc)
    xs = xc[:, :SSM_INNER]
    bm = xc[:, SSM_INNER:SSM_INNER + SSM_GROUPS * SSM_STATE]
    cm = xc[:, SSM_INNER + SSM_GROUPS * SSM_STATE:]

    lane = lax.broadcasted_iota(I32, (L, LANES), 1)
    dt_lanes = (lane >= DT_LANE) & (lane < DT_LANE + SSM_HEADS)
    dt_in = small_ref[...] + dtb_ref[...]
    dt = jnp.maximum(dt_in, 0.0) + jnp.log1p(jnp.exp(-jnp.abs(dt_in)))
    dt = jnp.where(dt_lanes, dt, 0.0)
    a = -jnp.exp(alog_ref[...])
    a_dt = dt * a
    row = lax.broadcasted_iota(I32, (L, L), 0)
    col = lax.broadcasted_iota(I32, (L, L), 1)
    tri = row >= col
    a_cum = jnp.dot(tri.astype(F32), a_dt, precision=HIGHEST, preferred_element_type=F32)
    a_last = a_cum[L - 1:L, :]
    t = jnp.concatenate([dt, jnp.exp(a_cum), jnp.exp(a_last - a_cum)], axis=0)
    t_hi = t.astype(BF16)
    r1 = t - t_hi.astype(F32)
    t_mid = r1.astype(BF16)
    t_lo = (r1 - t_mid.astype(F32)).astype(BF16)
    expanded = _dot(jnp.concatenate([t_hi, t_mid, t_lo], axis=1), e_ref[...])
    x_dt = xs * expanded[0:L]
    eac_x = expanded[L:2 * L]
    ds_x = expanded[2 * L:3 * L]
    cd_x = eac_x[L - 1:L, :]
    xd = (x_dt * ds_x).astype(BF16)
    x_bf = x_dt.astype(BF16)
    a_cum_t = a_cum.T
    lane_h = lax.broadcasted_iota(I32, (L, LANES), 1)

    y_parts = []
    for g in range(SSM_GROUPS):
        bg = bm[:, g * SSM_STATE:(g + 1) * SSM_STATE].astype(BF16)
        cg = cm[:, g * SSM_STATE:(g + 1) * SSM_STATE].astype(BF16)
        cb = lax.dot_general(cg, bg, (((1,), (1,)), ((), ())), preferred_element_type=F32)
        st_prev = st_ref[g]
        y_off = _dot(cg, st_prev.astype(BF16)) * eac_x[:, g * GROUP_W:(g + 1) * GROUP_W]
        for jp in range(HEADS_PER_GROUP // 2):
            pair = g * (HEADS_PER_GROUP // 2) + jp
            xp = x_bf[:, pair * LANES:(pair + 1) * LANES]
            outs = []
            for hh in range(2):
                hl = DT_LANE + 2 * pair + hh
                seg = a_cum[:, hl:hl + 1] - a_cum_t[hl:hl + 1, :]
                decay = jnp.where(tri, jnp.exp(seg), 0.0)
                outs.append(_dot((cb * decay).astype(BF16), xp))
            y_parts.append(jnp.where(lane_h < SSM_HEADDIM, outs[0], outs[1])
                           + y_off[:, jp * LANES:(jp + 1) * LANES])
        upd = lax.dot_general(bg, xd[:, g * GROUP_W:(g + 1) * GROUP_W], (((0,), (0,)), ((), ())),
                              preferred_element_type=F32)
        st_ref[g] = st_prev * cd_x[:, g * GROUP_W:(g + 1) * GROUP_W] + upd

    y = jnp.concatenate(y_parts, axis=1) + xs * dexp_ref[...]
    y = y * _silu(z_ref[...])
    normed = []
    for g in range(SSM_GROUPS):
        yg = y[:, g * GROUP_W:(g + 1) * GROUP_W]
        normed.append(yg * lax.rsqrt(jnp.mean(yg * yg, axis=-1, keepdims=True) + RMS_EPS))
    y_ref[...] = (jnp.concatenate(normed, axis=1) * nw_ref[...]).astype(y_ref.dtype)


def _ssd(proj, cw, cb, dtb, alog, dexp, nw, e_mat):
    L = SSM_CHUNK
    nc = SEQ // L
    xbc_blk = COL_XBC // SSM_CONV_DIM

    def row1(w):
        return pl.BlockSpec((1, w), lambda b, c: (0, 0))

    return pl.pallas_call(
        _ssd_kernel,
        grid=(BATCH, nc),
        in_specs=[pl.BlockSpec((L, SSM_CONV_DIM), lambda b, c: (b * nc + c, xbc_blk)),
                  pl.BlockSpec((HALO, SSM_CONV_DIM),
                               lambda b, c: (jnp.maximum((b * nc + c) * (L // HALO) - 1, 0), xbc_blk)),
                  pl.BlockSpec((L, SSM_INNER), lambda b, c: (b * nc + c, COL_Z // SSM_INNER)),
                  pl.BlockSpec((L, LANES), lambda b, c: (b * nc + c, COL_SMALL // LANES)),
                  pl.BlockSpec((SSM_CONV, SSM_CONV_DIM), lambda b, c: (0, 0)),
                  row1(SSM_CONV_DIM), row1(LANES), row1(LANES), row1(SSM_INNER), row1(SSM_INNER),
                  pl.BlockSpec((3 * LANES, SSM_INNER), lambda b, c: (0, 0))],
        out_specs=pl.BlockSpec((L, SSM_INNER), lambda b, c: (b * nc + c, 0)),
        out_shape=jax.ShapeDtypeStruct((N_TOK, SSM_INNER), BF16),
        scratch_shapes=[pltpu.VMEM((SSM_GROUPS, SSM_STATE, GROUP_W), F32),
                        pltpu.VMEM((HALO + L, SSM_CONV_DIM), F32)],
        compiler_params=_cparams(("parallel", "arbitrary")),
        name="mamba2_ssd",
    )(proj, proj, proj, proj, cw, cb, dtb, alog, dexp, nw, e_mat)


def _merge_kernel(attn_ref, ssm_ref, ga_ref, gb_ref, h_ref, wa_ref, ws_ref, wo_ref, g_ref, b_ref,
                  hf_ref, hb_ref):
    ya = _dot(attn_ref[...], wa_ref[...])
    yb = _dot(ssm_ref[...], ws_ref[...])
    mix = _sigmoid(ga_ref[...]) * ya + _sigmoid(gb_ref[...]) * yb
    mixed = _dot(mix.astype(BF16), wo_ref[...])
    h1 = _layer_norm(ALPHA * h_ref[...] + mixed, g_ref[...], b_ref[...])
    hf_ref[...] = h1
    hb_ref[...] = h1.astype(BF16)


def _merge(attn, ssm_y, proj, hf, wa, ws, wo, g, b):
    tm = 512
    full = lambda r, c: pl.BlockSpec((r, c), lambda i: (0, 0))
    tile = lambda c, j=0: pl.BlockSpec((tm, c), lambda i: (i, j))
    return pl.pallas_call(
        _merge_kernel,
        grid=(N_TOK // tm,),
        in_specs=[tile(MLA_HEADS * V_DIM), tile(SSM_INNER), tile(D_MODEL, COL_GATE_A // D_MODEL),
                  tile(D_MODEL, COL_GATE_B // D_MODEL), tile(D_MODEL),
                  full(MLA_HEADS * V_DIM, D_MODEL), full(SSM_INNER, D_MODEL), full(D_MODEL, D_MODEL),
                  full(1, D_MODEL), full(1, D_MODEL)],
        out_specs=[tile(D_MODEL), tile(D_MODEL)],
        out_shape=[jax.ShapeDtypeStruct((N_TOK, D_MODEL), F32),
                   jax.ShapeDtypeStruct((N_TOK, D_MODEL), BF16)],
        compiler_params=_cparams(("parallel",)),
        name="merge_ln1",
    )(attn, ssm_y, proj, proj, hf, wa, ws, wo, g, b)


def _first_argmax(vals, lane):
    m = jnp.max(vals, axis=-1, keepdims=True)
    idx = jnp.min(jnp.where(vals == m, lane, float(N_EXPERTS)), axis=-1, keepdims=True)
    return m, idx


def _post_kernel(hf_ref, hb_ref, p_ref, wr_ref, rb_ref, wsg_ref, wsu_ref, wsd_ref, wp_ref, wpg_ref,
                 base_ref, idx_ref, wt_ref, cnt_ref):
    hf = hf_ref[...]
    hb = hb_ref[...]
    tm = hf.shape[0]
    h_mid = (hf - hb.astype(F32)).astype(BF16)
    logits = _dot(jnp.concatenate([hb, h_mid, hb], axis=1), wr_ref[...])
    scores = _sigmoid(logits)
    sel = scores + rb_ref[...]
    lane_i = lax.broadcasted_iota(I32, (tm, N_EXPERTS), 1)
    grp = lane_i // PER_GROUP
    lane = lane_i.astype(F32)

    grp_scores = []
    for g in range(N_GROUPS):
        vals = jnp.where(grp == g, sel, -jnp.inf)
        m1, i1 = _first_argmax(vals, lane)
        m2 = jnp.max(jnp.where(lane == i1, -jnp.inf, vals), axis=-1, keepdims=True)
        grp_scores.append(m1 + m2)
    ranks = [jnp.zeros((tm, 1), I32) for _ in range(N_GROUPS)]
    for g in range(N_GROUPS):
        for o in range(g + 1, N_GROUPS):
            later_wins = (grp_scores[o] > grp_scores[g]).astype(I32)
            ranks[g] = ranks[g] + later_wins
            ranks[o] = ranks[o] + (1 - later_wins)
    keep = jnp.zeros((tm, N_EXPERTS), jnp.bool_)
    for g in range(N_GROUPS):
        keep = keep | ((grp == g) & (ranks[g] < TOPK_GROUPS))
    masked = jnp.where(keep, sel, -jnp.inf)

    lane_k = lax.broadcasted_iota(I32, (tm, TOP_K), 1)
    idx_out = jnp.zeros((tm, TOP_K), I32)
    w_out = jnp.zeros((tm, TOP_K), F32)
    chosen = jnp.zeros((tm, N_EXPERTS), F32)
    for kk in range(TOP_K):
        _, ik = _first_argmax(masked, lane)
        hit = lane == ik
        wk = jnp.sum(jnp.where(hit, scores, 0.0), axis=-1, keepdims=True)
        masked = jnp.where(hit, -jnp.inf, masked)
        chosen = jnp.where(hit, 1.0, chosen)
        idx_out = jnp.where(lane_k == kk, ik.astype(I32), idx_out)
        w_out = jnp.where(lane_k == kk, wk, w_out)
    w_out = w_out / jnp.sum(w_out, axis=-1, keepdims=True) * ROUTED_SCALE
    idx_ref[...] = idx_out
    wt_ref[...] = w_out
    cnt_ref[...] = jnp.sum(chosen, axis=0, keepdims=True)

    shared = _dot((_silu(_dot(hb, wsg_ref[...])) * _dot(hb, wsu_ref[...])).astype(BF16), wsd_ref[...])
    ple = _dot(p_ref[...].astype(BF16), wp_ref[...]) * _sigmoid(_dot(hb, wpg_ref[...]))
    base_ref[...] = ALPHA * hf + shared + ple


def _post(hf, hb, p, wr, rb, wsg, wsu, wsd, wp, wpg):
    tm = ROUTE_T
    full = lambda r, c: pl.BlockSpec((r, c), lambda i: (0, 0))
    tile = lambda c: pl.BlockSpec((tm, c), lambda i: (i, 0))
    return pl.pallas_call(
        _post_kernel,
        grid=(N_RTILES,),
        in_specs=[tile(D_MODEL), tile(D_MODEL), tile(PLE_DIM),
                  full(3 * D_MODEL, N_EXPERTS), full(1, N_EXPERTS),
                  full(D_MODEL, EXPERT_FF), full(D_MODEL, EXPERT_FF), full(EXPERT_FF, D_MODEL),
                  full(PLE_DIM, D_MODEL), full(D_MODEL, D_MODEL)],
        out_specs=[tile(D_MODEL), tile(TOP_K), tile(TOP_K),
                   pl.BlockSpec((None, 1, N_EXPERTS), lambda i: (i, 0, 0))],
        out_shape=[jax.ShapeDtypeStruct((N_TOK, D_MODEL), F32),
                   jax.ShapeDtypeStruct((N_TOK, TOP_K), I32),
                   jax.ShapeDtypeStruct((N_TOK, TOP_K), F32),
                   jax.ShapeDtypeStruct((N_RTILES, 1, N_EXPERTS), F32)],
        compiler_params=_cparams(("parallel",)),
        name="router_shared_ple",
    )(hf, hb, p, wr, rb, wsg, wsu, wsd, wp, wpg)


def _pos_kernel(idx_ref, base_ref, pos_ref):
    idx = idx_ref[...]
    tm = idx.shape[0]
    lane = lax.broadcasted_iota(I32, (tm, N_EXPERTS), 1)
    hits = [lane == idx[:, kk:kk + 1] for kk in range(TOP_K)]
    chosen = hits[0]
    for kk in range(1, TOP_K):
        chosen = chosen | hits[kk]
    r = lax.broadcasted_iota(I32, (tm, tm), 0)
    c = lax.broadcasted_iota(I32, (tm, tm), 1)
    earlier = jnp.where(c < r, 1.0, 0.0).astype(BF16)
    rank = _dot(earlier, jnp.where(chosen, 1.0, 0.0).astype(BF16))
    slot = rank + base_ref[...]
    lane_k = lax.broadcasted_iota(I32, (tm, TOP_K), 1)
    out = jnp.zeros((tm, TOP_K), I32)
    for kk in range(TOP_K):
        pk = jnp.sum(jnp.where(hits[kk], slot, 0.0), axis=-1, keepdims=True)
        out = jnp.where(lane_k == kk, pk.astype(I32), out)
    pos_ref[...] = out


def _slot_positions(idx, base):
    tm = ROUTE_T
    return pl.pallas_call(
        _pos_kernel,
        grid=(N_RTILES,),
        in_specs=[pl.BlockSpec((tm, TOP_K), lambda i: (i, 0)),
                  pl.BlockSpec((None, 1, N_EXPERTS), lambda i: (i, 0, 0))],
        out_specs=pl.BlockSpec((tm, TOP_K), lambda i: (i, 0)),
        out_shape=jax.ShapeDtypeStruct((N_TOK, TOP_K), I32),
        compiler_params=_cparams(("parallel",)),
        name="slot_positions",
    )(idx, base)


def _slot_layout(cnt):
    counts_te = cnt.reshape(N_RTILES, N_EXPERTS).astype(I32)
    counts = jnp.sum(counts_te, axis=0)
    padded = (counts + SLOT_BLOCK - 1) // SLOT_BLOCK * SLOT_BLOCK
    pad_end = jnp.cumsum(padded)
    pad_start = pad_end - padded
    tile_off = jnp.cumsum(counts_te, axis=0) - counts_te
    base = (pad_start[None, :] + tile_off).astype(F32).reshape(N_RTILES, 1, N_EXPERTS)
    n_used = (pad_end[-1] // SLOT_BLOCK).astype(I32).reshape(1)
    return base, counts, pad_start, padded, n_used


DISP_T = 512


def _dispatch_kernel(cnt_ref, ps_ref, pd_ref, nu_ref, pos_hbm, h3_ref, x_hbm,
                     pos_smem, zbuf, pos_sem, row_sem, pad_sem):
    i = pl.program_id(0)
    n = pl.num_programs(0)

    def pos_copy(blk, slot):
        return pltpu.make_async_copy(pos_hbm.at[blk], pos_smem.at[slot], pos_sem.at[slot])

    def pad_row(e, r):
        return pltpu.make_async_copy(zbuf.at[0], x_hbm.at[ps_ref[e] + r], pad_sem)

    def pad_block(b):
        return pltpu.make_async_copy(zbuf, x_hbm.at[pl.ds(b * SLOT_BLOCK, SLOT_BLOCK)], pad_sem)

    def for_each_pad(fn_row, fn_block):
        def per_expert(e, carry):
            def per_row(r, c2):
                fn_row(e, r)
                return c2
            return lax.fori_loop(cnt_ref[e], pd_ref[e], per_row, carry)
        lax.fori_loop(0, N_EXPERTS, per_expert, 0)

        def per_block(b, carry):
            fn_block(b)
            return carry
        lax.fori_loop(nu_ref[0], N_BLOCKS, per_block, 0)

    @pl.when(i == 0)
    def _():
        pos_copy(0, 0).start()
        pos_copy(1, 1).start()
        zbuf[...] = jnp.zeros(zbuf.shape, F32)
        for_each_pad(lambda e, r: pad_row(e, r).start(), lambda b: pad_block(b).start())
        for_each_pad(lambda e, r: pad_row(e, r).wait(), lambda b: pad_block(b).wait())

    slot = i % 2
    pos_copy(i, slot).wait()

    def body(t, carry):
        for kk in range(TOP_K):
            dst = pos_smem[slot, t * TOP_K + kk]
            pltpu.make_async_copy(h3_ref.at[t], x_hbm.at[dst], row_sem).start()
        return carry
    lax.fori_loop(0, DISP_T, body, 0)

    @pl.when(i + 2 < n)
    def _():
        pos_copy(i + 2, slot).start()

    for kk in range(TOP_K):
        pltpu.make_async_copy(h3_ref, x_hbm.at[pl.ds(0, DISP_T)], row_sem).wait()


def _dispatch(counts, pad_start, padded, n_used, pos, h3):
    n_tiles = N_TOK // DISP_T
    grid_spec = pltpu.PrefetchScalarGridSpec(
        num_scalar_prefetch=4,
        grid=(n_tiles,),
        in_specs=[pl.BlockSpec(memory_space=pl.ANY),
                  pl.BlockSpec((DISP_T, D_CHUNKS, LANES), lambda i, *_: (i, 0, 0))],
        out_specs=pl.BlockSpec(memory_space=pl.ANY),
        scratch_shapes=[pltpu.SMEM((2, DISP_T * TOP_K), I32),
                        pltpu.VMEM((SLOT_BLOCK, D_CHUNKS, LANES), F32),
                        pltpu.SemaphoreType.DMA((2,)),
                        pltpu.SemaphoreType.DMA,
                        pltpu.SemaphoreType.DMA])
    return pl.pallas_call(
        _dispatch_kernel,
        grid_spec=grid_spec,
        out_shape=jax.ShapeDtypeStruct((CAP, D_CHUNKS, LANES), F32),
        compiler_params=_cparams(("arbitrary",)),
        name="dispatch_scatter",
    )(counts, pad_start, padded, n_used, pos.reshape(n_tiles, DISP_T * TOP_K), h3)


def _moe_kernel(fb_ref, nb_ref, nu_ref, x_hbm, wg_ref, wu_ref, wd_ref, y_hbm,
                xbuf, ybuf, wg_bf, wu_bf, wd_bf, in_sem, out_sem):
    e = pl.program_id(0)
    first = fb_ref[e]
    nb = nb_ref[e]
    n_used = nu_ref[0]

    def rows(g):
        return pl.ds(pl.multiple_of(g * SLOT_BLOCK, SLOT_BLOCK), SLOT_BLOCK)

    def x_copy(g):
        slot = g % MOE_RING
        return pltpu.make_async_copy(x_hbm.at[rows(g)], xbuf.at[slot], in_sem.at[slot])

    def y_copy(g):
        slot = g % MOE_RING
        return pltpu.make_async_copy(ybuf.at[slot], y_hbm.at[rows(g)], out_sem.at[slot])

    @pl.when(e == 0)
    def _():
        for g in range(MOE_RING - 2):
            x_copy(g).start()

    def run_blocks(g, count):
        for j in range(count):
            x_copy(g + j).wait()
        for j in range(count):
            nxt = g + j + MOE_RING - 2

            @pl.when(nxt < n_used)
            def _():
                x_copy(nxt).start()

        x3 = jnp.concatenate([xbuf[(g + j) % MOE_RING] for j in range(count)], axis=0)
        x = _rows_to_matrix(x3).astype(BF16)
        hid = (_silu(_dot(x, wg_bf[...])) * _dot(x, wu_bf[...])).astype(BF16)
        y3 = _matrix_to_rows(_dot(hid, wd_bf[...]))
        for j in range(count):
            @pl.when(g + j >= MOE_RING)
            def _():
                y_copy(g + j - MOE_RING).wait()

            ybuf[(g + j) % MOE_RING] = y3[j * SLOT_BLOCK:(j + 1) * SLOT_BLOCK]
            y_copy(g + j).start()

    @pl.when(nb > 0)
    def _():
        wg_bf[...] = wg_ref[...].astype(BF16)
        wu_bf[...] = wu_ref[...].astype(BF16)
        wd_bf[...] = wd_ref[...].astype(BF16)

        def pair(p, carry):
            run_blocks(first + 2 * p, 2)
            return carry

        lax.fori_loop(0, nb // 2, pair, 0)

        @pl.when(nb % 2 == 1)
        def _():
            run_blocks(first + nb - 1, 1)

    @pl.when(e == N_EXPERTS - 1)
    def _():
        for j in range(MOE_RING):
            y_copy(n_used - 1 - j).wait()
        ybuf[0] = jnp.zeros(ybuf.shape[1:], F32)

        def tail_copy(b):
            return pltpu.make_async_copy(
                ybuf.at[0], y_hbm.at[pl.ds(pl.multiple_of(b * SLOT_BLOCK, SLOT_BLOCK), SLOT_BLOCK)],
                out_sem.at[0])

        def start(b, carry):
            tail_copy(b).start()
            return carry

        def wait(b, carry):
            tail_copy(b).wait()
            return carry

        lax.fori_loop(nu_ref[0], N_BLOCKS, start, 0)
        lax.fori_loop(nu_ref[0], N_BLOCKS, wait, 0)


def _moe(first_block, n_blocks, n_used, xs, wg, wu, wd, layer):
    wspec = lambda r, c: pl.BlockSpec((None, None, r, c), lambda e, *_: (layer, e, 0, 0))
    grid_spec = pltpu.PrefetchScalarGridSpec(
        num_scalar_prefetch=3,
        grid=(N_EXPERTS,),
        in_specs=[pl.BlockSpec(memory_space=pl.ANY),
                  wspec(D_MODEL, EXPERT_FF), wspec(D_MODEL, EXPERT_FF), wspec(EXPERT_FF, D_MODEL)],
        out_specs=pl.BlockSpec(memory_space=pl.ANY),
        scratch_shapes=[pltpu.VMEM((MOE_RING, SLOT_BLOCK, D_CHUNKS, LANES), F32),
                        pltpu.VMEM((MOE_RING, SLOT_BLOCK, D_CHUNKS, LANES), F32),
                        pltpu.VMEM((D_MODEL, EXPERT_FF), BF16),
                        pltpu.VMEM((D_MODEL, EXPERT_FF), BF16),
                        pltpu.VMEM((EXPERT_FF, D_MODEL), BF16),
                        pltpu.SemaphoreType.DMA((MOE_RING,)),
                        pltpu.SemaphoreType.DMA((MOE_RING,))])
    return pl.pallas_call(
        _moe_kernel,
        grid_spec=grid_spec,
        out_shape=jax.ShapeDtypeStruct((CAP, D_CHUNKS, LANES), F32),
        compiler_params=_cparams(("arbitrary",)),
        name="routed_experts",
    )(first_block, n_blocks, n_used, xs, wg, wu, wd)


COMB_T = 128


def _combine_kernel(pos_hbm, y_hbm, w_ref, base_ref, g_ref, b_ref, hf_ref, hb_ref,
                    pos_smem, buf, routed_ref, pos_sem, row_sem):
    i = pl.program_id(0)
    n = pl.num_programs(0)
    slot = i % 2
    nslot = (i + 1) % 2
    n_groups = COMB_T // SUBLANES

    def pos_copy(blk, s):
        return pltpu.make_async_copy(pos_hbm.at[blk], pos_smem.at[s], pos_sem.at[s])

    def issue_group(s, g):
        for j in range(SUBLANES):
            t = g * SUBLANES + j
            for kk in range(TOP_K):
                src = pos_smem[s, t * TOP_K + kk]
                pltpu.make_async_copy(y_hbm.at[src], buf.at[s, t, kk], row_sem.at[s]).start()

    def reduce_group(s, g):
        rows8 = pl.ds(pl.multiple_of(g * SUBLANES, SUBLANES), SUBLANES)
        acc = None
        for kk in range(TOP_K):
            term = w_ref[rows8, kk:kk + 1] * _rows_to_matrix(buf[s, rows8, kk])
            acc = term if acc is None else acc + term
        routed_ref[rows8, :] = acc

    def group_loop(fn):
        def body(g, carry):
            fn(g)
            return carry
        lax.fori_loop(0, n_groups, body, 0)

    @pl.when(i == 0)
    def _():
        pos_copy(0, 0).start()
        pos_copy(1, 1).start()
        pos_copy(0, 0).wait()
        group_loop(lambda g: issue_group(0, g))

    for kk in range(TOP_K):
        pltpu.make_async_copy(y_hbm.at[pl.ds(0, COMB_T)], buf.at[slot, :, kk], row_sem.at[slot]).wait()

    @pl.when(i + 1 < n)
    def _():
        pos_copy(i + 1, nslot).wait()

        def both(g):
            issue_group(nslot, g)
            reduce_group(slot, g)
        group_loop(both)

    @pl.when(i + 1 == n)
    def _():
        group_loop(lambda g: reduce_group(slot, g))

    @pl.when(i + 2 < n)
    def _():
        pos_copy(i + 2, slot).start()

    h2 = _layer_norm(base_ref[...] + routed_ref[...], g_ref[...], b_ref[...])
    hf_ref[...] = h2
    hb_ref[...] = h2.astype(BF16)


def _combine(pos, y3, wts, base, g, b):
    tm = COMB_T
    n_tiles = N_TOK // tm
    full = lambda r, c: pl.BlockSpec((r, c), lambda i: (0, 0))
    tile = lambda c: pl.BlockSpec((tm, c), lambda i: (i, 0))
    return pl.pallas_call(
        _combine_kernel,
        grid=(n_tiles,),
        in_specs=[pl.BlockSpec(memory_space=pl.ANY), pl.BlockSpec(memory_space=pl.ANY),
                  tile(TOP_K), tile(D_MODEL), full(1, D_MODEL), full(1, D_MODEL)],
        out_specs=[tile(D_MODEL), tile(D_MODEL)],
        out_shape=[jax.ShapeDtypeStruct((N_TOK, D_MODEL), F32),
                   jax.ShapeDtypeStruct((N_TOK, D_MODEL), BF16)],
        scratch_shapes=[pltpu.SMEM((2, tm * TOP_K), I32),
                        pltpu.VMEM((2, tm, TOP_K, D_CHUNKS, LANES), F32),
                        pltpu.VMEM((tm, D_MODEL), F32),
                        pltpu.SemaphoreType.DMA((2,)),
                        pltpu.SemaphoreType.DMA((2,))],
        compiler_params=_cparams(("arbitrary",)),
        name="combine_ln2",
    )(pos.reshape(n_tiles, tm * TOP_K), y3, wts, base, g, b)


def _pack_w_in(w):
    q_lat = w[:, 0:384]
    c_kv = w[:, 384:640]
    k_pe = w[:, 640:672]
    z = w[:, 672:1696]
    xbc = w[:, 1696:3232]
    dt = w[:, 3232:3248]
    gate_a = w[:, 3248:4272]
    gate_b = w[:, 4272:5296]
    zeros = lambda n: jnp.zeros((D_MODEL, n), w.dtype)
    small = jnp.concatenate([zeros(KPE_LANE), k_pe, dt, zeros(LANES - DT_LANE - SSM_HEADS)], axis=1)
    return jnp.concatenate([gate_a, gate_b, z, xbc, q_lat, small, c_kv], axis=1).astype(BF16)


def _pack_w_q(w):
    w = w.reshape(Q_RANK, MLA_HEADS, QK_NOPE + QK_ROPE)
    w = jnp.pad(w, ((0, 0), (0, 0), (0, HEAD_PAD - QK_NOPE - QK_ROPE)))
    return w.reshape(Q_RANK, MLA_HEADS * HEAD_PAD).astype(BF16)


def _pack_w_kv(w):
    w = w.reshape(KV_RANK, MLA_HEADS, QK_NOPE + V_DIM)
    wk = jnp.pad(w[:, :, :QK_NOPE], ((0, 0), (0, 0), (0, HEAD_PAD - QK_NOPE)))
    wv = w[:, :, QK_NOPE:]
    return (wk.reshape(KV_RANK, MLA_HEADS * HEAD_PAD).astype(BF16),
            wv.reshape(KV_RANK, MLA_HEADS * V_DIM).astype(BF16))


def _rope_tables(positions):
    half = QK_ROPE // 2
    inv_freq = jnp.power(ROPE_THETA, -jnp.arange(half, dtype=F32) * (2.0 / QK_ROPE))
    ang = positions.astype(F32).reshape(N_TOK, 1) * inv_freq
    cos, sin = jnp.cos(ang), jnp.sin(ang)
    z = lambda n: jnp.zeros((N_TOK, n), F32)
    tail = HEAD_PAD - QK_NOPE - QK_ROPE
    tc = jnp.concatenate([jnp.ones((N_TOK, QK_NOPE), F32), cos, cos, z(tail)], axis=1)
    ts1 = jnp.concatenate([z(QK_NOPE), -sin, z(half), z(tail)], axis=1)
    ts2 = jnp.concatenate([z(QK_NOPE), z(half), sin, z(tail)], axis=1)
    return tc, ts1, ts2


def _router_pieces(w):
    w_hi = lax.reduce_precision(w, exponent_bits=8, mantissa_bits=7)
    w_mid = (w - w_hi).astype(BF16)
    w_hi = w_hi.astype(BF16)
    return jnp.concatenate([w_hi, w_hi, w_mid], axis=0)


def _head_lane_row(v):
    return jnp.zeros((1, LANES), F32).at[0, DT_LANE:DT_LANE + SSM_HEADS].set(v.astype(F32))


def _expand_matrix():
    r = jnp.arange(LANES)[:, None]
    c = jnp.arange(SSM_INNER)[None, :]
    e = ((r - DT_LANE) == (c // SSM_HEADDIM)).astype(BF16)
    return jnp.concatenate([e, e, e], axis=0)


def kernel(x, p, positions, w_in, q_norm, w_q_up, kv_norm, w_kv_up, conv_w, conv_b, dt_bias, a_log, d_skip,
           ssm_norm, w_attn_br, w_ssm_br, w_o, ln1_g, ln1_b, w_router, router_bias, w_exp_gate, w_exp_up,
           w_exp_down, w_sh_gate, w_sh_up, w_sh_down, w_ple, w_ple_gate, ln2_g, ln2_b):
    tc, ts1, ts2 = _rope_tables(positions)
    e_mat = _expand_matrix()
    hf = x.reshape(N_TOK, D_MODEL)
    hb = hf.astype(BF16)
    row = lambda v: v.reshape(1, -1).astype(F32)
    for i in range(DEPTH):
        proj = _in_proj(hb, _pack_w_in(w_in[i]))
        q = _q_up(proj, row(q_norm[i]), _pack_w_q(w_q_up[i]), tc, ts1, ts2)
        wk, wv = _pack_w_kv(w_kv_up[i])
        k, v = _kv_up(proj, row(kv_norm[i]), wk, wv, tc, ts1, ts2)
        attn = _attention(q, k, v).reshape(N_TOK, MLA_HEADS * V_DIM)
        ssm_y = _ssd(proj, conv_w[i], row(conv_b[i]), _head_lane_row(dt_bias[i]), _head_lane_row(a_log[i]),
                     row(jnp.repeat(d_skip[i], SSM_HEADDIM)), row(ssm_norm[i]), e_mat)
        hf, hb = _merge(attn, ssm_y, proj, hf, w_attn_br[i].astype(BF16), w_ssm_br[i].astype(BF16),
                        w_o[i].astype(BF16), row(ln1_g[i]), row(ln1_b[i]))
        base, idx, wts, cnt = _post(hf, hb, p[i].reshape(N_TOK, PLE_DIM), _router_pieces(w_router[i]),
                                    row(router_bias[i]),
                                    w_sh_gate[i].astype(BF16), w_sh_up[i].astype(BF16),
                                    w_sh_down[i].astype(BF16), w_ple[i].astype(BF16),
                                    w_ple_gate[i].astype(BF16))
        base_slots, counts, pad_start, padded, n_used = _slot_layout(cnt)
        pos = _slot_positions(idx, base_slots)
        xs = _dispatch(counts, pad_start, padded, n_used, pos, hf.reshape(N_TOK, D_CHUNKS, LANES))
        y3 = _moe(pad_start // SLOT_BLOCK, padded // SLOT_BLOCK, n_used, xs,
                  w_exp_gate, w_exp_up, w_exp_down, i)
        hf, hb = _combine(pos, y3, wts, base, row(ln2_g[i]), row(ln2_b[i]))
    return hf.reshape(BATCH, SEQ, D_MODEL)
```

```python
import math

import jax
import jax.numpy as jnp
from jax import lax
from jax.experimental import pallas as pl
from jax.experimental.pallas import tpu as pltpu

F32 = jnp.float32
BF16 = jnp.bfloat16
I32 = jnp.int32
HIGHEST = lax.Precision.HIGHEST

D_MODEL = 1024
BATCH = 4
SEQ = 8192
DEPTH = 2
N_TOK = BATCH * SEQ
MLA_HEADS = 8
QK_NOPE = 64
QK_ROPE = 32
V_DIM = 64
Q_RANK = 384
KV_RANK = 256
ROPE_THETA = 10000.0
SSM_INNER = 1024
SSM_HEADDIM = 64
SSM_HEADS = 16
SSM_GROUPS = 2
SSM_STATE = 128
SSM_CONV = 4
SSM_CHUNK = 128
SSM_CONV_DIM = 1536
N_EXPERTS = 256
TOP_K = 8
N_GROUPS = 8
TOPK_GROUPS = 4
PER_GROUP = N_EXPERTS // N_GROUPS
EXPERT_FF = 256
ROUTED_SCALE = 2.5
PLE_DIM = 256
ALPHA = (2 * DEPTH) ** 0.25
LN_EPS = 1e-5
RMS_EPS = 1e-6

LANES = 128
SUBLANES = 8
VMEM_LIMIT = 48 * 1024 * 1024

COL_GATE_A = 0
COL_GATE_B = 1024
COL_Z = 2048
COL_XBC = 3072
COL_QLAT = 4608
COL_SMALL = 4992
COL_CKV = 5120
PACK_COLS = 5376
KPE_LANE = 64
DT_LANE = 96

HEAD_PAD = 128
N_PAIRS = MLA_HEADS // 2
V_SLAB = 2 * LANES

ROUTE_T = 256
N_RTILES = N_TOK // ROUTE_T
SLOT_BLOCK = 128
N_ASSIGN = N_TOK * TOP_K
N_BLOCKS = N_ASSIGN // SLOT_BLOCK + N_EXPERTS
CAP = N_BLOCKS * SLOT_BLOCK
D_CHUNKS = D_MODEL // LANES
MOE_RING = 6

NEG = float(jnp.finfo(jnp.float32).min)


def _cparams(sem):
    return pltpu.CompilerParams(dimension_semantics=sem, vmem_limit_bytes=VMEM_LIMIT)


def _sigmoid(x):
    return 1.0 / (1.0 + jnp.exp(-x))


def _silu(x):
    return x * _sigmoid(x)


def _layer_norm(x, g, b):
    mu = jnp.mean(x, axis=-1, keepdims=True)
    xc = x - mu
    var = jnp.mean(xc * xc, axis=-1, keepdims=True)
    return xc * lax.rsqrt(var + LN_EPS) * g + b


def _rms_norm(x, g):
    return x * lax.rsqrt(jnp.mean(x * x, axis=-1, keepdims=True) + RMS_EPS) * g


def _dot(a, b):
    return jnp.dot(a, b, preferred_element_type=F32)


def _swap_sublane_major(x):
    groups = x.shape[0]
    for d in (4, 2, 1):
        y = x.reshape(groups, SUBLANES // (2 * d), 2, d, SUBLANES, LANES)
        lo, hi = y[:, :, 0], y[:, :, 1]
        shp = lo.shape
        keep = (lax.broadcasted_iota(I32, shp, 3) & d) == 0
        rot = lambda v, s: pltpu.roll(v.reshape(-1, SUBLANES, LANES), s, 1).reshape(shp)
        new_lo = jnp.where(keep, lo, rot(hi, d))
        new_hi = jnp.where(keep, rot(lo, SUBLANES - d), hi)
        x = jnp.stack([new_lo, new_hi], axis=2).reshape(groups, SUBLANES, SUBLANES, LANES)
    return x


def _rows_to_matrix(x3):
    rows = x3.shape[0]
    t = _swap_sublane_major(x3.reshape(rows // SUBLANES, SUBLANES, D_CHUNKS, LANES))
    return jnp.concatenate([t[:, cc].reshape(rows, LANES) for cc in range(D_CHUNKS)], axis=1)


def _matrix_to_rows(y):
    rows = y.shape[0]
    t = jnp.stack([y[:, cc * LANES:(cc + 1) * LANES].reshape(rows // SUBLANES, SUBLANES, LANES)
                   for cc in range(D_CHUNKS)], axis=1)
    return _swap_sublane_major(t).reshape(rows, D_CHUNKS, LANES)


def _mm_kernel(x_ref, w_ref, o_ref):
    o_ref[...] = _dot(x_ref[...], w_ref[...]).astype(o_ref.dtype)


def _in_proj(hb, w_pack):
    tm, tn = 1024, 1792
    return pl.pallas_call(
        _mm_kernel,
        grid=(PACK_COLS // tn, N_TOK // tm),
        in_specs=[pl.BlockSpec((tm, D_MODEL), lambda j, i: (i, 0)),
                  pl.BlockSpec((D_MODEL, tn), lambda j, i: (0, j))],
        out_specs=pl.BlockSpec((tm, tn), lambda j, i: (i, j)),
        out_shape=jax.ShapeDtypeStruct((N_TOK, PACK_COLS), F32),
        compiler_params=_cparams(("parallel", "parallel")),
        name="in_proj",
    )(hb, w_pack)


def _rope128(x, c, s1, s2):
    return x * c + pltpu.roll(x, LANES - 16, 1) * s1 + pltpu.roll(x, 16, 1) * s2


def _q_up_kernel(ql_ref, g_ref, w_ref, c_ref, s1_ref, s2_ref, q_ref):
    y = _rms_norm(ql_ref[...], g_ref[...]).astype(BF16)
    q = _dot(y, w_ref[...])
    c, s1, s2 = c_ref[...], s1_ref[...], s2_ref[...]
    scale = (QK_NOPE + QK_ROPE) ** -0.5 * math.log2(math.e)
    for h in range(MLA_HEADS):
        qh = _rope128(q[:, h * HEAD_PAD:(h + 1) * HEAD_PAD], c, s1, s2)
        q_ref[h] = (qh * scale).astype(BF16)


def _q_up(proj, g, wq, tc, ts1, ts2):
    tm = 512
    spt = SEQ // tm
    tab = pl.BlockSpec((tm, LANES), lambda i: (i, 0))
    return pl.pallas_call(
        _q_up_kernel,
        grid=(N_TOK // tm,),
        in_specs=[pl.BlockSpec((tm, Q_RANK), lambda i: (i, COL_QLAT // Q_RANK)),
                  pl.BlockSpec((1, Q_RANK), lambda i: (0, 0)),
                  pl.BlockSpec((Q_RANK, MLA_HEADS * HEAD_PAD), lambda i: (0, 0)),
                  tab, tab, tab],
        out_specs=pl.BlockSpec((None, MLA_HEADS, tm, HEAD_PAD), lambda i: (i // spt, 0, i % spt, 0)),
        out_shape=jax.ShapeDtypeStruct((BATCH, MLA_HEADS, SEQ, HEAD_PAD), BF16),
        compiler_params=_cparams(("parallel",)),
        name="q_up",
    )(proj, g, wq, tc, ts1, ts2)


def _kv_up_kernel(ckv_ref, small_ref, g_ref, wk_ref, wv_ref, c_ref, s1_ref, s2_ref, k_ref, v_ref):
    y = _rms_norm(ckv_ref[...], g_ref[...]).astype(BF16)
    k_all = _dot(y, wk_ref[...])
    v_all = _dot(y, wv_ref[...])
    lane = lax.broadcasted_iota(I32, small_ref.shape, 1)
    kpe_raw = jnp.where((lane >= KPE_LANE) & (lane < KPE_LANE + QK_ROPE), small_ref[...], 0.0)
    kpe = _rope128(kpe_raw, c_ref[...], s1_ref[...], s2_ref[...])
    for h in range(MLA_HEADS):
        k_ref[h] = (k_all[:, h * HEAD_PAD:(h + 1) * HEAD_PAD] + kpe).astype(BF16)
    ones = jnp.ones((v_all.shape[0], LANES), F32)
    for j in range(N_PAIRS):
        v_ref[j] = jnp.concatenate([v_all[:, j * LANES:(j + 1) * LANES], ones], axis=1).astype(BF16)


def _kv_up(proj, g, wk, wv, tc, ts1, ts2):
    tm = 512
    spt = SEQ // tm
    tab = pl.BlockSpec((tm, LANES), lambda i: (i, 0))
    return pl.pallas_call(
        _kv_up_kernel,
        grid=(N_TOK // tm,),
        in_specs=[pl.BlockSpec((tm, KV_RANK), lambda i: (i, COL_CKV // KV_RANK)),
                  pl.BlockSpec((tm, LANES), lambda i: (i, COL_SMALL // LANES)),
                  pl.BlockSpec((1, KV_RANK), lambda i: (0, 0)),
                  pl.BlockSpec((KV_RANK, MLA_HEADS * HEAD_PAD), lambda i: (0, 0)),
                  pl.BlockSpec((KV_RANK, MLA_HEADS * V_DIM), lambda i: (0, 0)),
                  tab, tab, tab],
        out_specs=[pl.BlockSpec((None, MLA_HEADS, tm, HEAD_PAD), lambda i: (i // spt, 0, i % spt, 0)),
                   pl.BlockSpec((None, N_PAIRS, tm, V_SLAB), lambda i: (i // spt, 0, i % spt, 0))],
        out_shape=[jax.ShapeDtypeStruct((BATCH, MLA_HEADS, SEQ, HEAD_PAD), BF16),
                   jax.ShapeDtypeStruct((BATCH, N_PAIRS, SEQ, V_SLAB), BF16)],
        compiler_params=_cparams(("parallel",)),
        name="kv_up",
    )(proj, proj, g, wk, wv, tc, ts1, ts2)


ATT_T = 512


def _attn_kernel(q_ref, k_ref, v_ref, o_ref, m_ref, acc_ref):
    qi = pl.program_id(2)
    t = ATT_T
    m_ref[...] = jnp.full(m_ref.shape, -jnp.inf, F32)
    acc_ref[...] = jnp.zeros(acc_ref.shape, F32)

    def kv_tile(ks, width, masked):
        v = v_ref[pl.ds(ks, width), :]
        scores = [lax.dot_general(q_ref[hh], k_ref[hh, pl.ds(ks, width), :], (((1,), (1,)), ((), ())),
                                  preferred_element_type=F32) for hh in range(2)]
        alphas, pvs = [], []
        for hh in range(2):
            s = scores[hh]
            if masked:
                row = lax.broadcasted_iota(I32, s.shape, 0)
                col = lax.broadcasted_iota(I32, s.shape, 1)
                s = jnp.where(ks + col <= qi * t + row, s, NEG)
            m_prev = m_ref[hh]
            m_new = jnp.maximum(m_prev, jnp.max(s, axis=-1, keepdims=True))
            alphas.append(jnp.exp2(m_prev - m_new))
            p = jnp.exp2(s - jnp.concatenate([m_new] * (width // LANES), axis=1))
            pvs.append(_dot(p.astype(BF16), v))
            m_ref[hh] = m_new
        for hh in range(2):
            acc_ref[hh] = jnp.concatenate([alphas[hh], alphas[hh]], axis=1) * acc_ref[hh] + pvs[hh]

    def body(kp, carry):
        kv_tile(pl.multiple_of(kp * 2 * t, 2 * t), 2 * t, False)
        return carry

    lax.fori_loop(0, qi // 2, body, 0)

    @pl.when(qi % 2 == 0)
    def _():
        kv_tile(pl.multiple_of(qi * t, t), t, True)

    @pl.when(qi % 2 == 1)
    def _():
        kv_tile(pl.multiple_of((qi - 1) * t, t), 2 * t, True)

    lane = lax.broadcasted_iota(I32, o_ref.shape, 1)
    a0 = acc_ref[0]
    a1 = acc_ref[1]
    o0 = a0[:, :LANES] / a0[:, LANES:]
    o1 = a1[:, :LANES] / a1[:, LANES:]
    o_ref[...] = jnp.where(lane < V_DIM, o0, o1).astype(o_ref.dtype)


def _attention(q, k, v):
    t = ATT_T
    return pl.pallas_call(
        _attn_kernel,
        grid=(BATCH, N_PAIRS, SEQ // t),
        in_specs=[pl.BlockSpec((None, 2, t, HEAD_PAD), lambda b, j, qi: (b, j, qi, 0)),
                  pl.BlockSpec((None, 2, SEQ, HEAD_PAD), lambda b, j, qi: (b, j, 0, 0)),
                  pl.BlockSpec((None, None, SEQ, V_SLAB), lambda b, j, qi: (b, j, 0, 0))],
        out_specs=pl.BlockSpec((None, t, LANES), lambda b, j, qi: (b, qi, j)),
        out_shape=jax.ShapeDtypeStruct((BATCH, SEQ, MLA_HEADS * V_DIM), BF16),
        scratch_shapes=[pltpu.VMEM((2, t, LANES), F32), pltpu.VMEM((2, t, V_SLAB), F32)],
        compiler_params=_cparams(("parallel", "parallel", "parallel")),
        name="mla_attention",
    )(q, k, v)


HALO = SUBLANES
HEADS_PER_GROUP = SSM_HEADS // SSM_GROUPS
GROUP_W = SSM_INNER // SSM_GROUPS


def _ssd_kernel(xbc_ref, halo_ref, z_ref, small_ref, cw_ref, cb_ref, dtb_ref, alog_ref, dexp_ref,
                nw_ref, e_ref, y_ref, st_ref, cat_ref):
    c = pl.program_id(1)
    L = SSM_CHUNK

    @pl.when(c == 0)
    def _():
        st_ref[...] = jnp.zeros(st_ref.shape, F32)

    cat_ref[0:HALO, :] = jnp.where(c == 0, 0.0, halo_ref[...])
    cat_ref[HALO:HALO + L, :] = xbc_ref[...]
    acc = jnp.broadcast_to(cb_ref[...], (L, SSM_CONV_DIM))
    for kk in range(SSM_CONV):
        off = HALO - (SSM_CONV - 1) + kk
        acc = acc + cw_ref[kk:kk + 1, :] * cat_ref[off:off + L, :]
    xc = _silu(acc)
    xs = xc[:, :SSM_INNER]
    bm = xc[:, SSM_INNER:SSM_INNER + SSM_GROUPS * SSM_STATE]
    cm = xc[:, SSM_INNER + SSM_GROUPS * SSM_STATE:]

    lane = lax.broadcasted_iota(I32, (L, LANES), 1)
    dt_lanes = (lane >= DT_LANE) & (lane < DT_LANE + SSM_HEADS)
    dt_in = small_ref[...] + dtb_ref[...]
    dt = jnp.maximum(dt_in, 0.0) + jnp.log1p(jnp.exp(-jnp.abs(dt_in)))
    dt = jnp.where(dt_lanes, dt, 0.0)
    a = -jnp.exp(alog_ref[...])
    a_dt = dt * a
    row = lax.broadcasted_iota(I32, (L, L), 0)
    col = lax.broadcasted_iota(I32, (L, L), 1)
    tri = row >= col
    a_cum = jnp.dot(tri.astype(F32), a_dt, precision=HIGHEST, preferred_element_type=F32)
    a_last = a_cum[L - 1:L, :]
    t = jnp.concatenate([dt, jnp.exp(a_cum), jnp.exp(a_last - a_cum)], axis=0)
    t_hi = t.astype(BF16)
    r1 = t - t_hi.astype(F32)
    t_mid = r1.astype(BF16)
    t_lo = (r1 - t_mid.astype(F32)).astype(BF16)
    expanded = _dot(jnp.concatenate([t_hi, t_mid, t_lo], axis=1), e_ref[...])
    x_dt = xs * expanded[0:L]
    eac_x = expanded[L:2 * L]
    ds_x = expanded[2 * L:3 * L]
    cd_x = eac_x[L - 1:L, :]
    xd = (x_dt * ds_x).astype(BF16)
    x_bf = x_dt.astype(BF16)
    a_cum_t = a_cum.T
    lane_h = lax.broadcasted_iota(I32, (L, LANES), 1)

    y_parts = []
    for g in range(SSM_GROUPS):
        bg = bm[:, g * SSM_STATE:(g + 1) * SSM_STATE].astype(BF16)
        cg = cm[:, g * SSM_STATE:(g + 1) * SSM_STATE].astype(BF16)
        cb = lax.dot_general(cg, bg, (((1,), (1,)), ((), ())), preferred_element_type=F32)
        st_prev = st_ref[g]
        y_off = _dot(cg, st_prev.astype(BF16)) * eac_x[:, g * GROUP_W:(g + 1) * GROUP_W]
        for jp in range(HEADS_PER_GROUP // 2):
            pair = g * (HEADS_PER_GROUP // 2) + jp
            xp = x_bf[:, pair * LANES:(pair + 1) * LANES]
            outs = []
            for hh in range(2):
                hl = DT_LANE + 2 * pair + hh
                seg = a_cum[:, hl:hl + 1] - a_cum_t[hl:hl + 1, :]
                decay = jnp.where(tri, jnp.exp(seg), 0.0)
                outs.append(_dot((cb * decay).astype(BF16), xp))
            y_parts.append(jnp.where(lane_h < SSM_HEADDIM, outs[0], outs[1])
                           + y_off[:, jp * LANES:(jp + 1) * LANES])
        upd = lax.dot_general(bg, xd[:, g * GROUP_W:(g + 1) * GROUP_W], (((0,), (0,)), ((), ())),
                              preferred_element_type=F32)
        st_ref[g] = st_prev * cd_x[:, g * GROUP_W:(g + 1) * GROUP_W] + upd

    y = jnp.concatenate(y_parts, axis=1) + xs * dexp_ref[...]
    y = y * _silu(z_ref[...])
    normed = []
    for g in range(SSM_GROUPS):
        yg = y[:, g * GROUP_W:(g + 1) * GROUP_W]
        normed.append(yg * lax.rsqrt(jnp.mean(yg * yg, axis=-1, keepdims=True) + RMS_EPS))
    y_ref[...] = (jnp.concatenate(normed, axis=1) * nw_ref[...]).astype(y_ref.dtype)


def _ssd(proj, cw, cb, dtb, alog, dexp, nw, e_mat):
    L = SSM_CHUNK
    nc = SEQ // L
    xbc_blk = COL_XBC // SSM_CONV_DIM

    def row1(w):
        return pl.BlockSpec((1, w), lambda b, c: (0, 0))

    return pl.pallas_call(
        _ssd_kernel,
        grid=(BATCH, nc),
        in_specs=[pl.BlockSpec((L, SSM_CONV_DIM), lambda b, c: (b * nc + c, xbc_blk)),
                  pl.BlockSpec((HALO, SSM_CONV_DIM),
                               lambda b, c: (jnp.maximum((b * nc + c) * (L // HALO) - 1, 0), xbc_blk)),
                  pl.BlockSpec((L, SSM_INNER), lambda b, c: (b * nc + c, COL_Z // SSM_INNER)),
                  pl.BlockSpec((L, LANES), lambda b, c: (b * nc + c, COL_SMALL // LANES)),
                  pl.BlockSpec((SSM_CONV, SSM_CONV_DIM), lambda b, c: (0, 0)),
                  row1(SSM_CONV_DIM), row1(LANES), row1(LANES), row1(SSM_INNER), row1(SSM_INNER),
                  pl.BlockSpec((3 * LANES, SSM_INNER), lambda b, c: (0, 0))],
        out_specs=pl.BlockSpec((L, SSM_INNER), lambda b, c: (b * nc + c, 0)),
        out_shape=jax.ShapeDtypeStruct((N_TOK, SSM_INNER), BF16),
        scratch_shapes=[pltpu.VMEM((SSM_GROUPS, SSM_STATE, GROUP_W), F32),
                        pltpu.VMEM((HALO + L, SSM_CONV_DIM), F32)],
        compiler_params=_cparams(("parallel", "arbitrary")),
        name="mamba2_ssd",
    )(proj, proj, proj, proj, cw, cb, dtb, alog, dexp, nw, e_mat)


def _merge_kernel(attn_ref, ssm_ref, ga_ref, gb_ref, h_ref, wa_ref, ws_ref, wo_ref, g_ref, b_ref,
                  hf_ref, hb_ref):
    ya = _dot(attn_ref[...], wa_ref[...])
    yb = _dot(ssm_ref[...], ws_ref[...])
    mix = _sigmoid(ga_ref[...]) * ya + _sigmoid(gb_ref[...]) * yb
    mixed = _dot(mix.astype(BF16), wo_ref[...])
    h1 = _layer_norm(ALPHA * h_ref[...] + mixed, g_ref[...], b_ref[...])
    hf_ref[...] = h1
    hb_ref[...] = h1.astype(BF16)


def _merge(attn, ssm_y, proj, hf, wa, ws, wo, g, b):
    tm = 512
    full = lambda r, c: pl.BlockSpec((r, c), lambda i: (0, 0))
    tile = lambda c, j=0: pl.BlockSpec((tm, c), lambda i: (i, j))
    return pl.pallas_call(
        _merge_kernel,
        grid=(N_TOK // tm,),
        in_specs=[tile(MLA_HEADS * V_DIM), tile(SSM_INNER), tile(D_MODEL, COL_GATE_A // D_MODEL),
                  tile(D_MODEL, COL_GATE_B // D_MODEL), tile(D_MODEL),
                  full(MLA_HEADS * V_DIM, D_MODEL), full(SSM_INNER, D_MODEL), full(D_MODEL, D_MODEL),
                  full(1, D_MODEL), full(1, D_MODEL)],
        out_specs=[tile(D_MODEL), tile(D_MODEL)],
        out_shape=[jax.ShapeDtypeStruct((N_TOK, D_MODEL), F32),
                   jax.ShapeDtypeStruct((N_TOK, D_MODEL), BF16)],
        compiler_params=_cparams(("parallel",)),
        name="merge_ln1",
    )(attn, ssm_y, proj, proj, hf, wa, ws, wo, g, b)


def _first_argmax(vals, lane):
    m = jnp.max(vals, axis=-1, keepdims=True)
    idx = jnp.min(jnp.where(vals == m, lane, float(N_EXPERTS)), axis=-1, keepdims=True)
    return m, idx


def _post_kernel(hf_ref, hb_ref, p_ref, wr_ref, rb_ref, wsg_ref, wsu_ref, wsd_ref, wp_ref, wpg_ref,
                 base_ref, idx_ref, wt_ref, cnt_ref):
    hf = hf_ref[...]
    hb = hb_ref[...]
    tm = hf.shape[0]
    h_mid = (hf - hb.astype(F32)).astype(BF16)
    logits = _dot(jnp.concatenate([hb, h_mid, hb], axis=1), wr_ref[...])
    scores = _sigmoid(logits)
    sel = scores + rb_ref[...]
    lane_i = lax.broadcasted_iota(I32, (tm, N_EXPERTS), 1)
    grp = lane_i // PER_GROUP
    lane = lane_i.astype(F32)

    grp_scores = []
    for g in range(N_GROUPS):
        vals = jnp.where(grp == g, sel, -jnp.inf)
        m1, i1 = _first_argmax(vals, lane)
        m2 = jnp.max(jnp.where(lane == i1, -jnp.inf, vals), axis=-1, keepdims=True)
        grp_scores.append(m1 + m2)
    ranks = [jnp.zeros((tm, 1), I32) for _ in range(N_GROUPS)]
    for g in range(N_GROUPS):
        for o in range(g + 1, N_GROUPS):
            later_wins = (grp_scores[o] > grp_scores[g]).astype(I32)
            ranks[g] = ranks[g] + later_wins
            ranks[o] = ranks[o] + (1 - later_wins)
    keep = jnp.zeros((tm, N_EXPERTS), jnp.bool_)
    for g in range(N_GROUPS):
        keep = keep | ((grp == g) & (ranks[g] < TOPK_GROUPS))
    masked = jnp.where(keep, sel, -jnp.inf)

    lane_k = lax.broadcasted_iota(I32, (tm, TOP_K), 1)
    idx_out = jnp.zeros((tm, TOP_K), I32)
    w_out = jnp.zeros((tm, TOP_K), F32)
    chosen = jnp.zeros((tm, N_EXPERTS), F32)
    for kk in range(TOP_K):
        _, ik = _first_argmax(masked, lane)
        hit = lane == ik
        wk = jnp.sum(jnp.where(hit, scores, 0.0), axis=-1, keepdims=True)
        masked = jnp.where(hit, -jnp.inf, masked)
        chosen = jnp.where(hit, 1.0, chosen)
        idx_out = jnp.where(lane_k == kk, ik.astype(I32), idx_out)
        w_out = jnp.where(lane_k == kk, wk, w_out)
    w_out = w_out / jnp.sum(w_out, axis=-1, keepdims=True) * ROUTED_SCALE
    idx_ref[...] = idx_out
    wt_ref[...] = w_out
    cnt_ref[...] = jnp.sum(chosen, axis=0, keepdims=True)

    shared = _dot((_silu(_dot(hb, wsg_ref[...])) * _dot(hb, wsu_ref[...])).astype(BF16), wsd_ref[...])
    ple = _dot(p_ref[...].astype(BF16), wp_ref[...]) * _sigmoid(_dot(hb, wpg_ref[...]))
    base_ref[...] = ALPHA * hf + shared + ple


def _post(hf, hb, p, wr, rb, wsg, wsu, wsd, wp, wpg):
    tm = ROUTE_T
    full = lambda r, c: pl.BlockSpec((r, c), lambda i: (0, 0))
    tile = lambda c: pl.BlockSpec((tm, c), lambda i: (i, 0))
    return pl.pallas_call(
        _post_kernel,
        grid=(N_RTILES,),
        in_specs=[tile(D_MODEL), tile(D_MODEL), tile(PLE_DIM),
                  full(3 * D_MODEL, N_EXPERTS), full(1, N_EXPERTS),
                  full(D_MODEL, EXPERT_FF), full(D_MODEL, EXPERT_FF), full(EXPERT_FF, D_MODEL),
                  full(PLE_DIM, D_MODEL), full(D_MODEL, D_MODEL)],
        out_specs=[tile(D_MODEL), tile(TOP_K), tile(TOP_K),
                   pl.BlockSpec((None, 1, N_EXPERTS), lambda i: (i, 0, 0))],
        out_shape=[jax.ShapeDtypeStruct((N_TOK, D_MODEL), F32),
                   jax.ShapeDtypeStruct((N_TOK, TOP_K), I32),
                   jax.ShapeDtypeStruct((N_TOK, TOP_K), F32),
                   jax.ShapeDtypeStruct((N_RTILES, 1, N_EXPERTS), F32)],
        compiler_params=_cparams(("parallel",)),
        name="router_shared_ple",
    )(hf, hb, p, wr, rb, wsg, wsu, wsd, wp, wpg)


def _pos_kernel(idx_ref, base_ref, pos_ref):
    idx = idx_ref[...]
    tm = idx.shape[0]
    lane = lax.broadcasted_iota(I32, (tm, N_EXPERTS), 1)
    hits = [lane == idx[:, kk:kk + 1] for kk in range(TOP_K)]
    chosen = hits[0]
    for kk in range(1, TOP_K):
        chosen = chosen | hits[kk]
    r = lax.broadcasted_iota(I32, (tm, tm), 0)
    c = lax.broadcasted_iota(I32, (tm, tm), 1)
    earlier = jnp.where(c < r, 1.0, 0.0).astype(BF16)
    rank = _dot(earlier, jnp.where(chosen, 1.0, 0.0).astype(BF16))
    slot = rank + base_ref[...]
    lane_k = lax.broadcasted_iota(I32, (tm, TOP_K), 1)
    out = jnp.zeros((tm, TOP_K), I32)
    for kk in range(TOP_K):
        pk = jnp.sum(jnp.where(hits[kk], slot, 0.0), axis=-1, keepdims=True)
        out = jnp.where(lane_k == kk, pk.astype(I32), out)
    pos_ref[...] = out


def _slot_positions(idx, base):
    tm = ROUTE_T
    return pl.pallas_call(
        _pos_kernel,
        grid=(N_RTILES,),
        in_specs=[pl.BlockSpec((tm, TOP_K), lambda i: (i, 0)),
                  pl.BlockSpec((None, 1, N_EXPERTS), lambda i: (i, 0, 0))],
        out_specs=pl.BlockSpec((tm, TOP_K), lambda i: (i, 0)),
        out_shape=jax.ShapeDtypeStruct((N_TOK, TOP_K), I32),
        compiler_params=_cparams(("parallel",)),
        name="slot_positions",
    )(idx, base)


def _slot_layout(cnt):
    counts_te = cnt.reshape(N_RTILES, N_EXPERTS).astype(I32)
    counts = jnp.sum(counts_te, axis=0)
    padded = (counts + SLOT_BLOCK - 1) // SLOT_BLOCK * SLOT_BLOCK
    pad_end = jnp.cumsum(padded)
    pad_start = pad_end - padded
    tile_off = jnp.cumsum(counts_te, axis=0) - counts_te
    base = (pad_start[None, :] + tile_off).astype(F32).reshape(N_RTILES, 1, N_EXPERTS)
    n_used = (pad_end[-1] // SLOT_BLOCK).astype(I32).reshape(1)
    return base, counts, pad_start, padded, n_used


DISP_T = 512


def _dispatch_kernel(cnt_ref, ps_ref, pd_ref, nu_ref, pos_hbm, h3_ref, x_hbm,
                     pos_smem, zbuf, pos_sem, row_sem, pad_sem):
    i = pl.program_id(0)
    n = pl.num_programs(0)

    def pos_copy(blk, slot):
        return pltpu.make_async_copy(pos_hbm.at[blk], pos_smem.at[slot], pos_sem.at[slot])

    def pad_row(e, r):
        return pltpu.make_async_copy(zbuf.at[0], x_hbm.at[ps_ref[e] + r], pad_sem)

    def pad_block(b):
        return pltpu.make_async_copy(zbuf, x_hbm.at[pl.ds(b * SLOT_BLOCK, SLOT_BLOCK)], pad_sem)

    def for_each_pad(fn_row, fn_block):
        def per_expert(e, carry):
            def per_row(r, c2):
                fn_row(e, r)
                return c2
            return lax.fori_loop(cnt_ref[e], pd_ref[e], per_row, carry)
        lax.fori_loop(0, N_EXPERTS, per_expert, 0)

        def per_block(b, carry):
            fn_block(b)
            return carry
        lax.fori_loop(nu_ref[0], N_BLOCKS, per_block, 0)

    @pl.when(i == 0)
    def _():
        pos_copy(0, 0).start()
        pos_copy(1, 1).start()
        zbuf[...] = jnp.zeros(zbuf.shape, F32)
        for_each_pad(lambda e, r: pad_row(e, r).start(), lambda b: pad_block(b).start())
        for_each_pad(lambda e, r: pad_row(e, r).wait(), lambda b: pad_block(b).wait())

    slot = i % 2
    pos_copy(i, slot).wait()

    def body(t, carry):
        for kk in range(TOP_K):
            dst = pos_smem[slot, t * TOP_K + kk]
            pltpu.make_async_copy(h3_ref.at[t], x_hbm.at[dst], row_sem).start()
        return carry
    lax.fori_loop(0, DISP_T, body, 0)

    @pl.when(i + 2 < n)
    def _():
        pos_copy(i + 2, slot).start()

    for kk in range(TOP_K):
        pltpu.make_async_copy(h3_ref, x_hbm.at[pl.ds(0, DISP_T)], row_sem).wait()


def _dispatch(counts, pad_start, padded, n_used, pos, h3):
    n_tiles = N_TOK // DISP_T
    grid_spec = pltpu.PrefetchScalarGridSpec(
        num_scalar_prefetch=4,
        grid=(n_tiles,),
        in_specs=[pl.BlockSpec(memory_space=pl.ANY),
                  pl.BlockSpec((DISP_T, D_CHUNKS, LANES), lambda i, *_: (i, 0, 0))],
        out_specs=pl.BlockSpec(memory_space=pl.ANY),
        scratch_shapes=[pltpu.SMEM((2, DISP_T * TOP_K), I32),
                        pltpu.VMEM((SLOT_BLOCK, D_CHUNKS, LANES), F32),
                        pltpu.SemaphoreType.DMA((2,)),
                        pltpu.SemaphoreType.DMA,
                        pltpu.SemaphoreType.DMA])
    return pl.pallas_call(
        _dispatch_kernel,
        grid_spec=grid_spec,
        out_shape=jax.ShapeDtypeStruct((CAP, D_CHUNKS, LANES), F32),
        compiler_params=_cparams(("arbitrary",)),
        name="dispatch_scatter",
    )(counts, pad_start, padded, n_used, pos.reshape(n_tiles, DISP_T * TOP_K), h3)


def _moe_kernel(fb_ref, nb_ref, nu_ref, x_hbm, wg_ref, wu_ref, wd_ref, y_hbm,
                xbuf, ybuf, wg_bf, wu_bf, wd_bf, in_sem, out_sem):
    e = pl.program_id(0)
    first = fb_ref[e]
    nb = nb_ref[e]
    n_used = nu_ref[0]

    def rows(g):
        return pl.ds(pl.multiple_of(g * SLOT_BLOCK, SLOT_BLOCK), SLOT_BLOCK)

    def x_copy(g):
        slot = g % MOE_RING
        return pltpu.make_async_copy(x_hbm.at[rows(g)], xbuf.at[slot], in_sem.at[slot])

    def y_copy(g):
        slot = g % MOE_RING
        return pltpu.make_async_copy(ybuf.at[slot], y_hbm.at[rows(g)], out_sem.at[slot])

    @pl.when(e == 0)
    def _():
        for g in range(MOE_RING - 2):
            x_copy(g).start()

    def run_blocks(g, count):
        for j in range(count):
            x_copy(g + j).wait()
        for j in range(count):
            nxt = g + j + MOE_RING - 2

            @pl.when(nxt < n_used)
            def _():
                x_copy(nxt).start()

        x3 = jnp.concatenate([xbuf[(g + j) % MOE_RING] for j in range(count)], axis=0)
        x = _rows_to_matrix(x3).astype(BF16)
        hid = (_silu(_dot(x, wg_bf[...])) * _dot(x, wu_bf[...])).astype(BF16)
        y3 = _matrix_to_rows(_dot(hid, wd_bf[...]))
        for j in range(count):
            @pl.when(g + j >= MOE_RING)
            def _():
                y_copy(g + j - MOE_RING).wait()

            ybuf[(g + j) % MOE_RING] = y3[j * SLOT_BLOCK:(j + 1) * SLOT_BLOCK]
            y_copy(g + j).start()

    @pl.when(nb > 0)
    def _():
        wg_bf[...] = wg_ref[...].astype(BF16)
        wu_bf[...] = wu_ref[...].astype(BF16)
        wd_bf[...] = wd_ref[...].astype(BF16)

        def pair(p, carry):
            run_blocks(first + 2 * p, 2)
            return carry

        lax.fori_loop(0, nb // 2, pair, 0)

        @pl.when(nb % 2 == 1)
        def _():
            run_blocks(first + nb - 1, 1)

    @pl.when(e == N_EXPERTS - 1)
    def _():
        for j in range(MOE_RING):
            y_copy(n_used - 1 - j).wait()
        ybuf[0] = jnp.zeros(ybuf.shape[1:], F32)

        def tail_copy(b):
            return pltpu.make_async_copy(
                ybuf.at[0], y_hbm.at[pl.ds(pl.multiple_of(b * SLOT_BLOCK, SLOT_BLOCK), SLOT_BLOCK)],
                out_sem.at[0])

        def start(b, carry):
            tail_copy(b).start()
            return carry

        def wait(b, carry):
            tail_copy(b).wait()
            return carry

        lax.fori_loop(nu_ref[0], N_BLOCKS, start, 0)
        lax.fori_loop(nu_ref[0], N_BLOCKS, wait, 0)


def _moe(first_block, n_blocks, n_used, xs, wg, wu, wd, layer):
    wspec = lambda r, c: pl.BlockSpec((None, None, r, c), lambda e, *_: (layer, e, 0, 0))
    grid_spec = pltpu.PrefetchScalarGridSpec(
        num_scalar_prefetch=3,
        grid=(N_EXPERTS,),
        in_specs=[pl.BlockSpec(memory_space=pl.ANY),
                  wspec(D_MODEL, EXPERT_FF), wspec(D_MODEL, EXPERT_FF), wspec(EXPERT_FF, D_MODEL)],
        out_specs=pl.BlockSpec(memory_space=pl.ANY),
        scratch_shapes=[pltpu.VMEM((MOE_RING, SLOT_BLOCK, D_CHUNKS, LANES), F32),
                        pltpu.VMEM((MOE_RING, SLOT_BLOCK, D_CHUNKS, LANES), F32),
                        pltpu.VMEM((D_MODEL, EXPERT_FF), BF16),
                        pltpu.VMEM((D_MODEL, EXPERT_FF), BF16),
                        pltpu.VMEM((EXPERT_FF, D_MODEL), BF16),
                        pltpu.SemaphoreType.DMA((MOE_RING,)),
                        pltpu.SemaphoreType.DMA((MOE_RING,))])
    return pl.pallas_call(
        _moe_kernel,
        grid_spec=grid_spec,
        out_shape=jax.ShapeDtypeStruct((CAP, D_CHUNKS, LANES), F32),
        compiler_params=_cparams(("arbitrary",)),
        name="routed_experts",
    )(first_block, n_blocks, n_used, xs, wg, wu, wd)


COMB_T = 128


def _combine_kernel(pos_hbm, y_hbm, w_ref, base_ref, g_ref, b_ref, hf_ref, hb_ref,
                    pos_smem, buf, routed_ref, pos_sem, row_sem):
    i = pl.program_id(0)
    n = pl.num_programs(0)
    slot = i % 2
    nslot = (i + 1) % 2
    n_groups = COMB_T // SUBLANES

    def pos_copy(blk, s):
        return pltpu.make_async_copy(pos_hbm.at[blk], pos_smem.at[s], pos_sem.at[s])

    def issue_group(s, g):
        for j in range(SUBLANES):
            t = g * SUBLANES + j
            for kk in range(TOP_K):
                src = pos_smem[s, t * TOP_K + kk]
                pltpu.make_async_copy(y_hbm.at[src], buf.at[s, t, kk], row_sem.at[s]).start()

    def reduce_group(s, g):
        rows8 = pl.ds(pl.multiple_of(g * SUBLANES, SUBLANES), SUBLANES)
        acc = None
        for kk in range(TOP_K):
            term = w_ref[rows8, kk:kk + 1] * _rows_to_matrix(buf[s, rows8, kk])
            acc = term if acc is None else acc + term
        routed_ref[rows8, :] = acc

    def group_loop(fn):
        def body(g, carry):
            fn(g)
            return carry
        lax.fori_loop(0, n_groups, body, 0)

    @pl.when(i == 0)
    def _():
        pos_copy(0, 0).start()
        pos_copy(1, 1).start()
        pos_copy(0, 0).wait()
        group_loop(lambda g: issue_group(0, g))

    for kk in range(TOP_K):
        pltpu.make_async_copy(y_hbm.at[pl.ds(0, COMB_T)], buf.at[slot, :, kk], row_sem.at[slot]).wait()

    @pl.when(i + 1 < n)
    def _():
        pos_copy(i + 1, nslot).wait()

        def both(g):
            issue_group(nslot, g)
            reduce_group(slot, g)
        group_loop(both)

    @pl.when(i + 1 == n)
    def _():
        group_loop(lambda g: reduce_group(slot, g))

    @pl.when(i + 2 < n)
    def _():
        pos_copy(i + 2, slot).start()

    h2 = _layer_norm(base_ref[...] + routed_ref[...], g_ref[...], b_ref[...])
    hf_ref[...] = h2
    hb_ref[...] = h2.astype(BF16)


def _combine(pos, y3, wts, base, g, b):
    tm = COMB_T
    n_tiles = N_TOK // tm
    full = lambda r, c: pl.BlockSpec((r, c), lambda i: (0, 0))
    tile = lambda c: pl.BlockSpec((tm, c), lambda i: (i, 0))
    return pl.pallas_call(
        _combine_kernel,
        grid=(n_tiles,),
        in_specs=[pl.BlockSpec(memory_space=pl.ANY), pl.BlockSpec(memory_space=pl.ANY),
                  tile(TOP_K), tile(D_MODEL), full(1, D_MODEL), full(1, D_MODEL)],
        out_specs=[tile(D_MODEL), tile(D_MODEL)],
        out_shape=[jax.ShapeDtypeStruct((N_TOK, D_MODEL), F32),
                   jax.ShapeDtypeStruct((N_TOK, D_MODEL), BF16)],
        scratch_shapes=[pltpu.SMEM((2, tm * TOP_K), I32),
                        pltpu.VMEM((2, tm, TOP_K, D_CHUNKS, LANES), F32),
                        pltpu.VMEM((tm, D_MODEL), F32),
                        pltpu.SemaphoreType.DMA((2,)),
                        pltpu.SemaphoreType.DMA((2,))],
        compiler_params=_cparams(("arbitrary",)),
        name="combine_ln2",
    )(pos.reshape(n_tiles, tm * TOP_K), y3, wts, base, g, b)


def _pack_w_in(w):
    q_lat = w[:, 0:384]
    c_kv = w[:, 384:640]
    k_pe = w[:, 640:672]
    z = w[:, 672:1696]
    xbc = w[:, 1696:3232]
    dt = w[:, 3232:3248]
    gate_a = w[:, 3248:4272]
    gate_b = w[:, 4272:5296]
    zeros = lambda n: jnp.zeros((D_MODEL, n), w.dtype)
    small = jnp.concatenate([zeros(KPE_LANE), k_pe, dt, zeros(LANES - DT_LANE - SSM_HEADS)], axis=1)
    return jnp.concatenate([gate_a, gate_b, z, xbc, q_lat, small, c_kv], axis=1).astype(BF16)


def _pack_w_q(w):
    w = w.reshape(Q_RANK, MLA_HEADS, QK_NOPE + QK_ROPE)
    w = jnp.pad(w, ((0, 0), (0, 0), (0, HEAD_PAD - QK_NOPE - QK_ROPE)))
    return w.reshape(Q_RANK, MLA_HEADS * HEAD_PAD).astype(BF16)


def _pack_w_kv(w):
    w = w.reshape(KV_RANK, MLA_HEADS, QK_NOPE + V_DIM)
    wk = jnp.pad(w[:, :, :QK_NOPE], ((0, 0), (0, 0), (0, HEAD_PAD - QK_NOPE)))
    wv = w[:, :, QK_NOPE:]
    return (wk.reshape(KV_RANK, MLA_HEADS * HEAD_PAD).astype(BF16),
            wv.reshape(KV_RANK, MLA_HEADS * V_DIM).astype(BF16))


def _rope_tables(positions):
    half = QK_ROPE // 2
    inv_freq = jnp.power(ROPE_THETA, -jnp.arange(half, dtype=F32) * (2.0 / QK_ROPE))
    ang = positions.astype(F32).reshape(N_TOK, 1) * inv_freq
    cos, sin = jnp.cos(ang), jnp.sin(ang)
    z = lambda n: jnp.zeros((N_TOK, n), F32)
    tail = HEAD_PAD - QK_NOPE - QK_ROPE
    tc = jnp.concatenate([jnp.ones((N_TOK, QK_NOPE), F32), cos, cos, z(tail)], axis=1)
    ts1 = jnp.concatenate([z(QK_NOPE), -sin, z(half), z(tail)], axis=1)
    ts2 = jnp.concatenate([z(QK_NOPE), z(half), sin, z(tail)], axis=1)
    return tc, ts1, ts2


def _router_pieces(w):
    w_hi = lax.reduce_precision(w, exponent_bits=8, mantissa_bits=7)
    w_mid = (w - w_hi).astype(BF16)
    w_hi = w_hi.astype(BF16)
    return jnp.concatenate([w_hi, w_hi, w_mid], axis=0)


def _head_lane_row(v):
    return jnp.zeros((1, LANES), F32).at[0, DT_LANE:DT_LANE + SSM_HEADS].set(v.astype(F32))


def _expand_matrix():
    r = jnp.arange(LANES)[:, None]
    c = jnp.arange(SSM_INNER)[None, :]
    e = ((r - DT_LANE) == (c // SSM_HEADDIM)).astype(BF16)
    return jnp.concatenate([e, e, e], axis=0)


def kernel(x, p, positions, w_in, q_norm, w_q_up, kv_norm, w_kv_up, conv_w, conv_b, dt_bias, a_log, d_skip,
           ssm_norm, w_attn_br, w_ssm_br, w_o, ln1_g, ln1_b, w_router, router_bias, w_exp_gate, w_exp_up,
           w_exp_down, w_sh_gate, w_sh_up, w_sh_down, w_ple, w_ple_gate, ln2_g, ln2_b):
    tc, ts1, ts2 = _rope_tables(positions)
    e_mat = _expand_matrix()
    hf = x.reshape(N_TOK, D_MODEL)
    hb = hf.astype(BF16)
    row = lambda v: v.reshape(1, -1).astype(F32)
    for i in range(DEPTH):
        proj = _in_proj(hb, _pack_w_in(w_in[i]))
        q = _q_up(proj, row(q_norm[i]), _pack_w_q(w_q_up[i]), tc, ts1, ts2)
        wk, wv = _pack_w_kv(w_kv_up[i])
        k, v = _kv_up(proj, row(kv_norm[i]), wk, wv, tc, ts1, ts2)
        attn = _attention(q, k, v).reshape(N_TOK, MLA_HEADS * V_DIM)
        ssm_y = _ssd(proj, conv_w[i], row(conv_b[i]), _head_lane_row(dt_bias[i]), _head_lane_row(a_log[i]),
                     row(jnp.repeat(d_skip[i], SSM_HEADDIM)), row(ssm_norm[i]), e_mat)
        hf, hb = _merge(attn, ssm_y, proj, hf, w_attn_br[i].astype(BF16), w_ssm_br[i].astype(BF16),
                        w_o[i].astype(BF16), row(ln1_g[i]), row(ln1_b[i]))
        base, idx, wts, cnt = _post(hf, hb, p[i].reshape(N_TOK, PLE_DIM), _router_pieces(w_router[i]),
                                    row(router_bias[i]),
                                    w_sh_gate[i].astype(BF16), w_sh_up[i].astype(BF16),
                                    w_sh_down[i].astype(BF16), w_ple[i].astype(BF16),
                                    w_ple_gate[i].astype(BF16))
        base_slots, counts, pad_start, padded, n_used = _slot_layout(cnt)
        pos = _slot_positions(idx, base_slots)
        xs = _dispatch(counts, pad_start, padded, n_used, pos, hf.reshape(N_TOK, D_CHUNKS, LANES))
        y3 = _moe(pad_start // SLOT_BLOCK, padded // SLOT_BLOCK, n_used, xs,
                  w_exp_gate, w_exp_up, w_exp_down, i)
        hf, hb = _combine(pos, y3, wts, base, row(ln2_g[i]), row(ln2_b[i]))
    return hf.reshape(BATCH, SEQ, D_MODEL)
```

```python
import math

import jax
import jax.numpy as jnp
from jax import lax
from jax.experimental import pallas as pl
from jax.experimental.pallas import tpu as pltpu

F32 = jnp.float32
BF16 = jnp.bfloat16
I32 = jnp.int32
HIGHEST = lax.Precision.HIGHEST

D_MODEL = 1024
BATCH = 4
SEQ = 8192
DEPTH = 2
N_TOK = BATCH * SEQ
MLA_HEADS = 8
QK_NOPE = 64
QK_ROPE = 32
V_DIM = 64
Q_RANK = 384
KV_RANK = 256
ROPE_THETA = 10000.0
SSM_INNER = 1024
SSM_HEADDIM = 64
SSM_HEADS = 16
SSM_GROUPS = 2
SSM_STATE = 128
SSM_CONV = 4
SSM_CHUNK = 128
SSM_CONV_DIM = 1536
N_EXPERTS = 256
TOP_K = 8
N_GROUPS = 8
TOPK_GROUPS = 4
PER_GROUP = N_EXPERTS // N_GROUPS
EXPERT_FF = 256
ROUTED_SCALE = 2.5
PLE_DIM = 256
ALPHA = (2 * DEPTH) ** 0.25
LN_EPS = 1e-5
RMS_EPS = 1e-6

LANES = 128
SUBLANES = 8
VMEM_LIMIT = 48 * 1024 * 1024

COL_GATE_A = 0
COL_GATE_B = 1024
COL_Z = 2048
COL_XBC = 3072
COL_QLAT = 4608
COL_SMALL = 4992
COL_CKV = 5120
PACK_COLS = 5376
KPE_LANE = 64
DT_LANE = 96

HEAD_PAD = 128
N_PAIRS = MLA_HEADS // 2
V_SLAB = 2 * LANES

ROUTE_T = 256
N_RTILES = N_TOK // ROUTE_T
SLOT_BLOCK = 128
N_ASSIGN = N_TOK * TOP_K
N_BLOCKS = N_ASSIGN // SLOT_BLOCK + N_EXPERTS
CAP = N_BLOCKS * SLOT_BLOCK
D_CHUNKS = D_MODEL // LANES
MOE_RING = 6

NEG = float(jnp.finfo(jnp.float32).min)


def _cparams(sem):
    return pltpu.CompilerParams(dimension_semantics=sem, vmem_limit_bytes=VMEM_LIMIT)


def _sigmoid(x):
    return 1.0 / (1.0 + jnp.exp(-x))


def _silu(x):
    return x * _sigmoid(x)


def _layer_norm(x, g, b):
    mu = jnp.mean(x, axis=-1, keepdims=True)
    xc = x - mu
    var = jnp.mean(xc * xc, axis=-1, keepdims=True)
    return xc * lax.rsqrt(var + LN_EPS) * g + b


def _rms_norm(x, g):
    return x * lax.rsqrt(jnp.mean(x * x, axis=-1, keepdims=True) + RMS_EPS) * g


def _dot(a, b):
    return jnp.dot(a, b, preferred_element_type=F32)


def _swap_sublane_major(x):
    groups = x.shape[0]
    for d in (4, 2, 1):
        y = x.reshape(groups, SUBLANES // (2 * d), 2, d, SUBLANES, LANES)
        lo, hi = y[:, :, 0], y[:, :, 1]
        shp = lo.shape
        keep = (lax.broadcasted_iota(I32, shp, 3) & d) == 0
        rot = lambda v, s: pltpu.roll(v.reshape(-1, SUBLANES, LANES), s, 1).reshape(shp)
        new_lo = jnp.where(keep, lo, rot(hi, d))
        new_hi = jnp.where(keep, rot(lo, SUBLANES - d), hi)
        x = jnp.stack([new_lo, new_hi], axis=2).reshape(groups, SUBLANES, SUBLANES, LANES)
    return x


def _rows_to_matrix(x3):
    rows = x3.shape[0]
    t = _swap_sublane_major(x3.reshape(rows // SUBLANES, SUBLANES, D_CHUNKS, LANES))
    return jnp.concatenate([t[:, cc].reshape(rows, LANES) for cc in range(D_CHUNKS)], axis=1)


def _matrix_to_rows(y):
    rows = y.shape[0]
    t = jnp.stack([y[:, cc * LANES:(cc + 1) * LANES].reshape(rows // SUBLANES, SUBLANES, LANES)
                   for cc in range(D_CHUNKS)], axis=1)
    return _swap_sublane_major(t).reshape(rows, D_CHUNKS, LANES)


def _mm_kernel(x_ref, w_ref, o_ref):
    o_ref[...] = _dot(x_ref[...], w_ref[...]).astype(o_ref.dtype)


def _in_proj(hb, w_pack):
    tm, tn = 1024, 1792
    return pl.pallas_call(
        _mm_kernel,
        grid=(PACK_COLS // tn, N_TOK // tm),
        in_specs=[pl.BlockSpec((tm, D_MODEL), lambda j, i: (i, 0)),
                  pl.BlockSpec((D_MODEL, tn), lambda j, i: (0, j))],
        out_specs=pl.BlockSpec((tm, tn), lambda j, i: (i, j)),
        out_shape=jax.ShapeDtypeStruct((N_TOK, PACK_COLS), F32),
        compiler_params=_cparams(("parallel", "parallel")),
        name="in_proj",
    )(hb, w_pack)


def _rope128(x, c, s1, s2):
    return x * c + pltpu.roll(x, LANES - 16, 1) * s1 + pltpu.roll(x, 16, 1) * s2


def _q_up_kernel(ql_ref, g_ref, w_ref, c_ref, s1_ref, s2_ref, q_ref):
    y = _rms_norm(ql_ref[...], g_ref[...]).astype(BF16)
    q = _dot(y, w_ref[...])
    c, s1, s2 = c_ref[...], s1_ref[...], s2_ref[...]
    scale = (QK_NOPE + QK_ROPE) ** -0.5 * math.log2(math.e)
    for h in range(MLA_HEADS):
        qh = _rope128(q[:, h * HEAD_PAD:(h + 1) * HEAD_PAD], c, s1, s2)
        q_ref[h] = (qh * scale).astype(BF16)


def _q_up(proj, g, wq, tc, ts1, ts2):
    tm = 512
    spt = SEQ // tm
    tab = pl.BlockSpec((tm, LANES), lambda i: (i, 0))
    return pl.pallas_call(
        _q_up_kernel,
        grid=(N_TOK // tm,),
        in_specs=[pl.BlockSpec((tm, Q_RANK), lambda i: (i, COL_QLAT // Q_RANK)),
                  pl.BlockSpec((1, Q_RANK), lambda i: (0, 0)),
                  pl.BlockSpec((Q_RANK, MLA_HEADS * HEAD_PAD), lambda i: (0, 0)),
                  tab, tab, tab],
        out_specs=pl.BlockSpec((None, MLA_HEADS, tm, HEAD_PAD), lambda i: (i // spt, 0, i % spt, 0)),
        out_shape=jax.ShapeDtypeStruct((BATCH, MLA_HEADS, SEQ, HEAD_PAD), BF16),
        compiler_params=_cparams(("parallel",)),
        name="q_up",
    )(proj, g, wq, tc, ts1, ts2)


def _kv_up_kernel(ckv_ref, small_ref, g_ref, wk_ref, wv_ref, c_ref, s1_ref, s2_ref, k_ref, v_ref):
    y = _rms_norm(ckv_ref[...], g_ref[...]).astype(BF16)
    k_all = _dot(y, wk_ref[...])
    v_all = _dot(y, wv_ref[...])
    lane = lax.broadcasted_iota(I32, small_ref.shape, 1)
    kpe_raw = jnp.where((lane >= KPE_LANE) & (lane < KPE_LANE + QK_ROPE), small_ref[...], 0.0)
    kpe = _rope128(kpe_raw, c_ref[...], s1_ref[...], s2_ref[...])
    for h in range(MLA_HEADS):
        k_ref[h] = (k_all[:, h * HEAD_PAD:(h + 1) * HEAD_PAD] + kpe).astype(BF16)
    ones = jnp.ones((v_all.shape[0], LANES), F32)
    for j in range(N_PAIRS):
        v_ref[j] = jnp.concatenate([v_all[:, j * LANES:(j + 1) * LANES], ones], axis=1).astype(BF16)


def _kv_up(proj, g, wk, wv, tc, ts1, ts2):
    tm = 512
    spt = SEQ // tm
    tab = pl.BlockSpec((tm, LANES), lambda i: (i, 0))
    return pl.pallas_call(
        _kv_up_kernel,
        grid=(N_TOK // tm,),
        in_specs=[pl.BlockSpec((tm, KV_RANK), lambda i: (i, COL_CKV // KV_RANK)),
                  pl.BlockSpec((tm, LANES), lambda i: (i, COL_SMALL // LANES)),
                  pl.BlockSpec((1, KV_RANK), lambda i: (0, 0)),
                  pl.BlockSpec((KV_RANK, MLA_HEADS * HEAD_PAD), lambda i: (0, 0)),
                  pl.BlockSpec((KV_RANK, MLA_HEADS * V_DIM), lambda i: (0, 0)),
                  tab, tab, tab],
        out_specs=[pl.BlockSpec((None, MLA_HEADS, tm, HEAD_PAD), lambda i: (i // spt, 0, i % spt, 0)),
                   pl.BlockSpec((None, N_PAIRS, tm, V_SLAB), lambda i: (i // spt, 0, i % spt, 0))],
        out_shape=[jax.ShapeDtypeStruct((BATCH, MLA_HEADS, SEQ, HEAD_PAD), BF16),
                   jax.ShapeDtypeStruct((BATCH, N_PAIRS, SEQ, V_SLAB), BF16)],
        compiler_params=_cparams(("parallel",)),
        name="kv_up",
    )(proj, proj, g, wk, wv, tc, ts1, ts2)


ATT_T = 512


def _attn_kernel(q_ref, k_ref, v_ref, o_ref, m_ref, acc_ref):
    qi = pl.program_id(2)
    t = ATT_T
    m_ref[...] = jnp.full(m_ref.shape, -jnp.inf, F32)
    acc_ref[...] = jnp.zeros(acc_ref.shape, F32)

    def kv_tile(ks, width, masked):
        v = v_ref[pl.ds(ks, width), :]
        scores = [lax.dot_general(q_ref[hh], k_ref[hh, pl.ds(ks, width), :], (((1,), (1,)), ((), ())),
                                  preferred_element_type=F32) for hh in range(2)]
        alphas, pvs = [], []
        for hh in range(2):
            s = scores[hh]
            if masked:
                row = lax.broadcasted_iota(I32, s.shape, 0)
                col = lax.broadcasted_iota(I32, s.shape, 1)
                s = jnp.where(ks + col <= qi * t + row, s, NEG)
            m_prev = m_ref[hh]
            m_new = jnp.maximum(m_prev, jnp.max(s, axis=-1, keepdims=True))
            alphas.append(jnp.exp2(m_prev - m_new))
            p = jnp.exp2(s - jnp.concatenate([m_new] * (width // LANES), axis=1))
            pvs.append(_dot(p.astype(BF16), v))
            m_ref[hh] = m_new
        for hh in range(2):
            acc_ref[hh] = jnp.concatenate([alphas[hh], alphas[hh]], axis=1) * acc_ref[hh] + pvs[hh]

    def body(kp, carry):
        kv_tile(pl.multiple_of(kp * 2 * t, 2 * t), 2 * t, False)
        return carry

    lax.fori_loop(0, qi // 2, body, 0)

    @pl.when(qi % 2 == 0)
    def _():
        kv_tile(pl.multiple_of(qi * t, t), t, True)

    @pl.when(qi % 2 == 1)
    def _():
        kv_tile(pl.multiple_of((qi - 1) * t, t), 2 * t, True)

    lane = lax.broadcasted_iota(I32, o_ref.shape, 1)
    a0 = acc_ref[0]
    a1 = acc_ref[1]
    o0 = a0[:, :LANES] / a0[:, LANES:]
    o1 = a1[:, :LANES] / a1[:, LANES:]
    o_ref[...] = jnp.where(lane < V_DIM, o0, o1).astype(o_ref.dtype)


def _attention(q, k, v):
    t = ATT_T
    return pl.pallas_call(
        _attn_kernel,
        grid=(BATCH, N_PAIRS, SEQ // t),
        in_specs=[pl.BlockSpec((None, 2, t, HEAD_PAD), lambda b, j, qi: (b, j, qi, 0)),
                  pl.BlockSpec((None, 2, SEQ, HEAD_PAD), lambda b, j, qi: (b, j, 0, 0)),
                  pl.BlockSpec((None, None, SEQ, V_SLAB), lambda b, j, qi: (b, j, 0, 0))],
        out_specs=pl.BlockSpec((None, t, LANES), lambda b, j, qi: (b, qi, j)),
        out_shape=jax.ShapeDtypeStruct((BATCH, SEQ, MLA_HEADS * V_DIM), BF16),
        scratch_shapes=[pltpu.VMEM((2, t, LANES), F32), pltpu.VMEM((2, t, V_SLAB), F32)],
        compiler_params=_cparams(("parallel", "parallel", "parallel")),
        name="mla_attention",
    )(q, k, v)


HALO = SUBLANES
HEADS_PER_GROUP = SSM_HEADS // SSM_GROUPS
GROUP_W = SSM_INNER // SSM_GROUPS


def _ssd_kernel(xbc_ref, halo_ref, z_ref, small_ref, cw_ref, cb_ref, dtb_ref, alog_ref, dexp_ref,
                nw_ref, e_ref, y_ref, st_ref, cat_ref):
    c = pl.program_id(1)
    L = SSM_CHUNK

    @pl.when(c == 0)
    def _():
        st_ref[...] = jnp.zeros(st_ref.shape, F32)

    cat_ref[0:HALO, :] = jnp.where(c == 0, 0.0, halo_ref[...])
    cat_ref[HALO:HALO + L, :] = xbc_ref[...]
    acc = jnp.broadcast_to(cb_ref[...], (L, SSM_CONV_DIM))
    for kk in range(SSM_CONV):
        off = HALO - (SSM_CONV - 1) + kk
        acc = acc + cw_ref[kk:kk + 1, :] * cat_ref[off:off + L, :]
    xc = _silu(acc)
    xs = xc[:, :SSM_INNER]
    bm = xc[:, SSM_INNER:SSM_INNER + SSM_GROUPS * SSM_STATE]
    cm = xc[:, SSM_INNER + SSM_GROUPS * SSM_STATE:]

    lane = lax.broadcasted_iota(I32, (L, LANES), 1)
    dt_lanes = (lane >= DT_LANE) & (lane < DT_LANE + SSM_HEADS)
    dt_in = small_ref[...] + dtb_ref[...]
    dt = jnp.maximum(dt_in, 0.0) + jnp.log1p(jnp.exp(-jnp.abs(dt_in)))
    dt = jnp.where(dt_lanes, dt, 0.0)
    a = -jnp.exp(alog_ref[...])
    a_dt = dt * a
    row = lax.broadcasted_iota(I32, (L, L), 0)
    col = lax.broadcasted_iota(I32, (L, L), 1)
    tri = row >= col
    a_cum = jnp.dot(tri.astype(F32), a_dt, precision=HIGHEST, preferred_element_type=F32)
    a_last = a_cum[L - 1:L, :]
    t = jnp.concatenate([dt, jnp.exp(a_cum), jnp.exp(a_last - a_cum)], axis=0)
    t_hi = t.astype(BF16)
    r1 = t - t_hi.astype(F32)
    t_mid = r1.astype(BF16)
    t_lo = (r1 - t_mid.astype(F32)).astype(BF16)
    expanded = _dot(jnp.concatenate([t_hi, t_mid, t_lo], axis=1), e_ref[...])
    x_dt = xs * expanded[0:L]
    eac_x = expanded[L:2 * L]
    ds_x = expanded[2 * L:3 * L]
    cd_x = eac_x[L - 1:L, :]
    xd = (x_dt * ds_x).astype(BF16)
    x_bf = x_dt.astype(BF16)
    a_cum_t = a_cum.T
    lane_h = lax.broadcasted_iota(I32, (L, LANES), 1)

    bgs = [bm[:, g * SSM_STATE:(g + 1) * SSM_STATE].astype(BF16) for g in range(SSM_GROUPS)]
    cgs = [cm[:, g * SSM_STATE:(g + 1) * SSM_STATE].astype(BF16) for g in range(SSM_GROUPS)]
    cbs = [lax.dot_general(cgs[g], bgs[g], (((1,), (1,)), ((), ())), preferred_element_type=F32)
           for g in range(SSM_GROUPS)]
    st_prevs = [st_ref[g] for g in range(SSM_GROUPS)]
    y_offs = [_dot(cgs[g], st_prevs[g].astype(BF16)) * eac_x[:, g * GROUP_W:(g + 1) * GROUP_W]
              for g in range(SSM_GROUPS)]
    upds = [lax.dot_general(bgs[g], xd[:, g * GROUP_W:(g + 1) * GROUP_W], (((0,), (0,)), ((), ())),
                            preferred_element_type=F32) for g in range(SSM_GROUPS)]

    y_parts = []
    for g in range(SSM_GROUPS):
        cb, y_off = cbs[g], y_offs[g]
        for jp in range(HEADS_PER_GROUP // 2):
            pair = g * (HEADS_PER_GROUP // 2) + jp
            xp = x_bf[:, pair * LANES:(pair + 1) * LANES]
            outs = []
            for hh in range(2):
                hl = DT_LANE + 2 * pair + hh
                seg = a_cum[:, hl:hl + 1] - a_cum_t[hl:hl + 1, :]
                decay = jnp.where(tri, jnp.exp(seg), 0.0)
                outs.append(_dot((cb * decay).astype(BF16), xp))
            y_parts.append(jnp.where(lane_h < SSM_HEADDIM, outs[0], outs[1])
                           + y_off[:, jp * LANES:(jp + 1) * LANES])
        st_ref[g] = st_prevs[g] * cd_x[:, g * GROUP_W:(g + 1) * GROUP_W] + upds[g]

    y = jnp.concatenate(y_parts, axis=1) + xs * dexp_ref[...]
    y = y * _silu(z_ref[...])
    normed = []
    for g in range(SSM_GROUPS):
        yg = y[:, g * GROUP_W:(g + 1) * GROUP_W]
        normed.append(yg * lax.rsqrt(jnp.mean(yg * yg, axis=-1, keepdims=True) + RMS_EPS))
    y_ref[...] = (jnp.concatenate(normed, axis=1) * nw_ref[...]).astype(y_ref.dtype)


def _ssd(proj, cw, cb, dtb, alog, dexp, nw, e_mat):
    L = SSM_CHUNK
    nc = SEQ // L
    xbc_blk = COL_XBC // SSM_CONV_DIM

    def row1(w):
        return pl.BlockSpec((1, w), lambda b, c: (0, 0))

    return pl.pallas_call(
        _ssd_kernel,
        grid=(BATCH, nc),
        in_specs=[pl.BlockSpec((L, SSM_CONV_DIM), lambda b, c: (b * nc + c, xbc_blk)),
                  pl.BlockSpec((HALO, SSM_CONV_DIM),
                               lambda b, c: (jnp.maximum((b * nc + c) * (L // HALO) - 1, 0), xbc_blk)),
                  pl.BlockSpec((L, SSM_INNER), lambda b, c: (b * nc + c, COL_Z // SSM_INNER)),
                  pl.BlockSpec((L, LANES), lambda b, c: (b * nc + c, COL_SMALL // LANES)),
                  pl.BlockSpec((SSM_CONV, SSM_CONV_DIM), lambda b, c: (0, 0)),
                  row1(SSM_CONV_DIM), row1(LANES), row1(LANES), row1(SSM_INNER), row1(SSM_INNER),
                  pl.BlockSpec((3 * LANES, SSM_INNER), lambda b, c: (0, 0))],
        out_specs=pl.BlockSpec((L, SSM_INNER), lambda b, c: (b * nc + c, 0)),
        out_shape=jax.ShapeDtypeStruct((N_TOK, SSM_INNER), BF16),
        scratch_shapes=[pltpu.VMEM((SSM_GROUPS, SSM_STATE, GROUP_W), F32),
                        pltpu.VMEM((HALO + L, SSM_CONV_DIM), F32)],
        compiler_params=_cparams(("parallel", "arbitrary")),
        name="mamba2_ssd",
    )(proj, proj, proj, proj, cw, cb, dtb, alog, dexp, nw, e_mat)


def _merge_kernel(attn_ref, ssm_ref, ga_ref, gb_ref, h_ref, wa_ref, ws_ref, wo_ref, g_ref, b_ref,
                  hf_ref, hb_ref):
    ya = _dot(attn_ref[...], wa_ref[...])
    yb = _dot(ssm_ref[...], ws_ref[...])
    mix = _sigmoid(ga_ref[...]) * ya + _sigmoid(gb_ref[...]) * yb
    mixed = _dot(mix.astype(BF16), wo_ref[...])
    h1 = _layer_norm(ALPHA * h_ref[...] + mixed, g_ref[...], b_ref[...])
    hf_ref[...] = h1
    hb_ref[...] = h1.astype(BF16)


def _merge(attn, ssm_y, proj, hf, wa, ws, wo, g, b):
    tm = 512
    full = lambda r, c: pl.BlockSpec((r, c), lambda i: (0, 0))
    tile = lambda c, j=0: pl.BlockSpec((tm, c), lambda i: (i, j))
    return pl.pallas_call(
        _merge_kernel,
        grid=(N_TOK // tm,),
        in_specs=[tile(MLA_HEADS * V_DIM), tile(SSM_INNER), tile(D_MODEL, COL_GATE_A // D_MODEL),
                  tile(D_MODEL, COL_GATE_B // D_MODEL), tile(D_MODEL),
                  full(MLA_HEADS * V_DIM, D_MODEL), full(SSM_INNER, D_MODEL), full(D_MODEL, D_MODEL),
                  full(1, D_MODEL), full(1, D_MODEL)],
        out_specs=[tile(D_MODEL), tile(D_MODEL)],
        out_shape=[jax.ShapeDtypeStruct((N_TOK, D_MODEL), F32),
                   jax.ShapeDtypeStruct((N_TOK, D_MODEL), BF16)],
        compiler_params=_cparams(("parallel",)),
        name="merge_ln1",
    )(attn, ssm_y, proj, proj, hf, wa, ws, wo, g, b)


def _first_argmax(vals, lane):
    m = jnp.max(vals, axis=-1, keepdims=True)
    idx = jnp.min(jnp.where(vals == m, lane, float(N_EXPERTS)), axis=-1, keepdims=True)
    return m, idx


def _post_kernel(hf_ref, hb_ref, p_ref, wr_ref, rb_ref, wsg_ref, wsu_ref, wsd_ref, wp_ref, wpg_ref,
                 base_ref, idx_ref, wt_ref, cnt_ref):
    hf = hf_ref[...]
    hb = hb_ref[...]
    tm = hf.shape[0]
    h_mid = (hf - hb.astype(F32)).astype(BF16)
    logits = _dot(jnp.concatenate([hb, h_mid, hb], axis=1), wr_ref[...])
    scores = _sigmoid(logits)
    sel = scores + rb_ref[...]
    lane_i = lax.broadcasted_iota(I32, (tm, N_EXPERTS), 1)
    grp = lane_i // PER_GROUP
    lane = lane_i.astype(F32)

    grp_scores = []
    for g in range(N_GROUPS):
        vals = jnp.where(grp == g, sel, -jnp.inf)
        m1, i1 = _first_argmax(vals, lane)
        m2 = jnp.max(jnp.where(lane == i1, -jnp.inf, vals), axis=-1, keepdims=True)
        grp_scores.append(m1 + m2)
    ranks = [jnp.zeros((tm, 1), I32) for _ in range(N_GROUPS)]
    for g in range(N_GROUPS):
        for o in range(g + 1, N_GROUPS):
            later_wins = (grp_scores[o] > grp_scores[g]).astype(I32)
            ranks[g] = ranks[g] + later_wins
            ranks[o] = ranks[o] + (1 - later_wins)
    keep = jnp.zeros((tm, N_EXPERTS), jnp.bool_)
    for g in range(N_GROUPS):
        keep = keep | ((grp == g) & (ranks[g] < TOPK_GROUPS))
    masked = jnp.where(keep, sel, -jnp.inf)

    lane_k = lax.broadcasted_iota(I32, (tm, TOP_K), 1)
    idx_out = jnp.zeros((tm, TOP_K), I32)
    w_out = jnp.zeros((tm, TOP_K), F32)
    chosen = jnp.zeros((tm, N_EXPERTS), F32)
    for kk in range(TOP_K):
        _, ik = _first_argmax(masked, lane)
        hit = lane == ik
        wk = jnp.sum(jnp.where(hit, scores, 0.0), axis=-1, keepdims=True)
        masked = jnp.where(hit, -jnp.inf, masked)
        chosen = jnp.where(hit, 1.0, chosen)
        idx_out = jnp.where(lane_k == kk, ik.astype(I32), idx_out)
        w_out = jnp.where(lane_k == kk, wk, w_out)
    w_out = w_out / jnp.sum(w_out, axis=-1, keepdims=True) * ROUTED_SCALE
    idx_ref[...] = idx_out
    wt_ref[...] = w_out
    cnt_ref[...] = jnp.sum(chosen, axis=0, keepdims=True)

    shared = _dot((_silu(_dot(hb, wsg_ref[...])) * _dot(hb, wsu_ref[...])).astype(BF16), wsd_ref[...])
    ple = _dot(p_ref[...].astype(BF16), wp_ref[...]) * _sigmoid(_dot(hb, wpg_ref[...]))
    base_ref[...] = ALPHA * hf + shared + ple


def _post(hf, hb, p, wr, rb, wsg, wsu, wsd, wp, wpg):
    tm = ROUTE_T
    full = lambda r, c: pl.BlockSpec((r, c), lambda i: (0, 0))
    tile = lambda c: pl.BlockSpec((tm, c), lambda i: (i, 0))
    return pl.pallas_call(
        _post_kernel,
        grid=(N_RTILES,),
        in_specs=[tile(D_MODEL), tile(D_MODEL), tile(PLE_DIM),
                  full(3 * D_MODEL, N_EXPERTS), full(1, N_EXPERTS),
                  full(D_MODEL, EXPERT_FF), full(D_MODEL, EXPERT_FF), full(EXPERT_FF, D_MODEL),
                  full(PLE_DIM, D_MODEL), full(D_MODEL, D_MODEL)],
        out_specs=[tile(D_MODEL), tile(TOP_K), tile(TOP_K),
                   pl.BlockSpec((None, 1, N_EXPERTS), lambda i: (i, 0, 0))],
        out_shape=[jax.ShapeDtypeStruct((N_TOK, D_MODEL), F32),
                   jax.ShapeDtypeStruct((N_TOK, TOP_K), I32),
                   jax.ShapeDtypeStruct((N_TOK, TOP_K), F32),
                   jax.ShapeDtypeStruct((N_RTILES, 1, N_EXPERTS), F32)],
        compiler_params=_cparams(("parallel",)),
        name="router_shared_ple",
    )(hf, hb, p, wr, rb, wsg, wsu, wsd, wp, wpg)


def _pos_kernel(idx_ref, base_ref, pos_ref):
    idx = idx_ref[...]
    tm = idx.shape[0]
    lane = lax.broadcasted_iota(I32, (tm, N_EXPERTS), 1)
    hits = [lane == idx[:, kk:kk + 1] for kk in range(TOP_K)]
    chosen = hits[0]
    for kk in range(1, TOP_K):
        chosen = chosen | hits[kk]
    r = lax.broadcasted_iota(I32, (tm, tm), 0)
    c = lax.broadcasted_iota(I32, (tm, tm), 1)
    earlier = jnp.where(c < r, 1.0, 0.0).astype(BF16)
    rank = _dot(earlier, jnp.where(chosen, 1.0, 0.0).astype(BF16))
    slot = rank + base_ref[...]
    lane_k = lax.broadcasted_iota(I32, (tm, TOP_K), 1)
    out = jnp.zeros((tm, TOP_K), I32)
    for kk in range(TOP_K):
        pk = jnp.sum(jnp.where(hits[kk], slot, 0.0), axis=-1, keepdims=True)
        out = jnp.where(lane_k == kk, pk.astype(I32), out)
    pos_ref[...] = out


def _slot_positions(idx, base):
    tm = ROUTE_T
    return pl.pallas_call(
        _pos_kernel,
        grid=(N_RTILES,),
        in_specs=[pl.BlockSpec((tm, TOP_K), lambda i: (i, 0)),
                  pl.BlockSpec((None, 1, N_EXPERTS), lambda i: (i, 0, 0))],
        out_specs=pl.BlockSpec((tm, TOP_K), lambda i: (i, 0)),
        out_shape=jax.ShapeDtypeStruct((N_TOK, TOP_K), I32),
        compiler_params=_cparams(("parallel",)),
        name="slot_positions",
    )(idx, base)


def _slot_layout(cnt):
    counts_te = cnt.reshape(N_RTILES, N_EXPERTS).astype(I32)
    counts = jnp.sum(counts_te, axis=0)
    padded = (counts + SLOT_BLOCK - 1) // SLOT_BLOCK * SLOT_BLOCK
    pad_end = jnp.cumsum(padded)
    pad_start = pad_end - padded
    tile_off = jnp.cumsum(counts_te, axis=0) - counts_te
    base = (pad_start[None, :] + tile_off).astype(F32).reshape(N_RTILES, 1, N_EXPERTS)
    n_used = (pad_end[-1] // SLOT_BLOCK).astype(I32).reshape(1)
    return base, counts, pad_start, padded, n_used


DISP_T = 1024


def _dispatch_kernel(cnt_ref, ps_ref, pd_ref, nu_ref, pos_hbm, h3_ref, x_hbm,
                     pos_smem, zbuf, pos_sem, row_sem, pad_sem):
    i = pl.program_id(0)
    n = pl.num_programs(0)

    def pos_copy(blk, slot):
        return pltpu.make_async_copy(pos_hbm.at[blk], pos_smem.at[slot], pos_sem.at[slot])

    def pad_row(e, r):
        return pltpu.make_async_copy(zbuf.at[0], x_hbm.at[ps_ref[e] + r], pad_sem)

    def pad_block(b):
        return pltpu.make_async_copy(zbuf, x_hbm.at[pl.ds(b * SLOT_BLOCK, SLOT_BLOCK)], pad_sem)

    def for_each_pad(fn_row, fn_block):
        def per_expert(e, carry):
            def per_row(r, c2):
                fn_row(e, r)
                return c2
            return lax.fori_loop(cnt_ref[e], pd_ref[e], per_row, carry)
        lax.fori_loop(0, N_EXPERTS, per_expert, 0)

        def per_block(b, carry):
            fn_block(b)
            return carry
        lax.fori_loop(nu_ref[0], N_BLOCKS, per_block, 0)

    @pl.when(i == 0)
    def _():
        pos_copy(0, 0).start()
        pos_copy(1, 1).start()
        zbuf[...] = jnp.zeros(zbuf.shape, F32)
        for_each_pad(lambda e, r: pad_row(e, r).start(), lambda b: pad_block(b).start())
        for_each_pad(lambda e, r: pad_row(e, r).wait(), lambda b: pad_block(b).wait())

    slot = i % 2
    pos_copy(i, slot).wait()

    def body(t, carry):
        for kk in range(TOP_K):
            dst = pos_smem[slot, t * TOP_K + kk]
            pltpu.make_async_copy(h3_ref.at[t], x_hbm.at[dst], row_sem).start()
        return carry
    lax.fori_loop(0, DISP_T, body, 0)

    @pl.when(i + 2 < n)
    def _():
        pos_copy(i + 2, slot).start()

    for kk in range(TOP_K):
        pltpu.make_async_copy(h3_ref, x_hbm.at[pl.ds(0, DISP_T)], row_sem).wait()


def _dispatch(counts, pad_start, padded, n_used, pos, h3):
    n_tiles = N_TOK // DISP_T
    grid_spec = pltpu.PrefetchScalarGridSpec(
        num_scalar_prefetch=4,
        grid=(n_tiles,),
        in_specs=[pl.BlockSpec(memory_space=pl.ANY),
                  pl.BlockSpec((DISP_T, D_CHUNKS, LANES), lambda i, *_: (i, 0, 0))],
        out_specs=pl.BlockSpec(memory_space=pl.ANY),
        scratch_shapes=[pltpu.SMEM((2, DISP_T * TOP_K), I32),
                        pltpu.VMEM((SLOT_BLOCK, D_CHUNKS, LANES), F32),
                        pltpu.SemaphoreType.DMA((2,)),
                        pltpu.SemaphoreType.DMA,
                        pltpu.SemaphoreType.DMA])
    return pl.pallas_call(
        _dispatch_kernel,
        grid_spec=grid_spec,
        out_shape=jax.ShapeDtypeStruct((CAP, D_CHUNKS, LANES), F32),
        compiler_params=_cparams(("arbitrary",)),
        name="dispatch_scatter",
    )(counts, pad_start, padded, n_used, pos.reshape(n_tiles, DISP_T * TOP_K), h3)


def _moe_kernel(fb_ref, nb_ref, nu_ref, x_hbm, wg_ref, wu_ref, wd_ref, y_hbm,
                xbuf, ybuf, wg_bf, wu_bf, wd_bf, in_sem, out_sem):
    e = pl.program_id(0)
    first = fb_ref[e]
    nb = nb_ref[e]
    n_used = nu_ref[0]

    def rows(g):
        return pl.ds(pl.multiple_of(g * SLOT_BLOCK, SLOT_BLOCK), SLOT_BLOCK)

    def x_copy(g):
        slot = g % MOE_RING
        return pltpu.make_async_copy(x_hbm.at[rows(g)], xbuf.at[slot], in_sem.at[slot])

    def y_copy(g):
        slot = g % MOE_RING
        return pltpu.make_async_copy(ybuf.at[slot], y_hbm.at[rows(g)], out_sem.at[slot])

    @pl.when(e == 0)
    def _():
        for g in range(MOE_RING - 2):
            x_copy(g).start()

    def run_blocks(g, count):
        for j in range(count):
            x_copy(g + j).wait()
        for j in range(count):
            nxt = g + j + MOE_RING - 2

            @pl.when(nxt < n_used)
            def _():
                x_copy(nxt).start()

        x3 = jnp.concatenate([xbuf[(g + j) % MOE_RING] for j in range(count)], axis=0)
        x = _rows_to_matrix(x3).astype(BF16)
        hid = (_silu(_dot(x, wg_bf[...])) * _dot(x, wu_bf[...])).astype(BF16)
        y3 = _matrix_to_rows(_dot(hid, wd_bf[...]))
        for j in range(count):
            @pl.when(g + j >= MOE_RING)
            def _():
                y_copy(g + j - MOE_RING).wait()

            ybuf[(g + j) % MOE_RING] = y3[j * SLOT_BLOCK:(j + 1) * SLOT_BLOCK]
            y_copy(g + j).start()

    @pl.when(nb > 0)
    def _():
        wg_bf[...] = wg_ref[...].astype(BF16)
        wu_bf[...] = wu_ref[...].astype(BF16)
        wd_bf[...] = wd_ref[...].astype(BF16)

        def pair(p, carry):
            run_blocks(first + 2 * p, 2)
            return carry

        lax.fori_loop(0, nb // 2, pair, 0)

        @pl.when(nb % 2 == 1)
        def _():
            run_blocks(first + nb - 1, 1)

    @pl.when(e == N_EXPERTS - 1)
    def _():
        for j in range(MOE_RING):
            y_copy(n_used - 1 - j).wait()
        ybuf[0] = jnp.zeros(ybuf.shape[1:], F32)

        def tail_copy(b):
            return pltpu.make_async_copy(
                ybuf.at[0], y_hbm.at[pl.ds(pl.multiple_of(b * SLOT_BLOCK, SLOT_BLOCK), SLOT_BLOCK)],
                out_sem.at[0])

        def start(b, carry):
            tail_copy(b).start()
            return carry

        def wait(b, carry):
            tail_copy(b).wait()
            return carry

        lax.fori_loop(nu_ref[0], N_BLOCKS, start, 0)
        lax.fori_loop(nu_ref[0], N_BLOCKS, wait, 0)


def _moe(first_block, n_blocks, n_used, xs, wg, wu, wd, layer):
    wspec = lambda r, c: pl.BlockSpec((None, None, r, c), lambda e, *_: (layer, e, 0, 0))
    grid_spec = pltpu.PrefetchScalarGridSpec(
        num_scalar_prefetch=3,
        grid=(N_EXPERTS,),
        in_specs=[pl.BlockSpec(memory_space=pl.ANY),
                  wspec(D_MODEL, EXPERT_FF), wspec(D_MODEL, EXPERT_FF), wspec(EXPERT_FF, D_MODEL)],
        out_specs=pl.BlockSpec(memory_space=pl.ANY),
        scratch_shapes=[pltpu.VMEM((MOE_RING, SLOT_BLOCK, D_CHUNKS, LANES), F32),
                        pltpu.VMEM((MOE_RING, SLOT_BLOCK, D_CHUNKS, LANES), F32),
                        pltpu.VMEM((D_MODEL, EXPERT_FF), BF16),
                        pltpu.VMEM((D_MODEL, EXPERT_FF), BF16),
                        pltpu.VMEM((EXPERT_FF, D_MODEL), BF16),
                        pltpu.SemaphoreType.DMA((MOE_RING,)),
                        pltpu.SemaphoreType.DMA((MOE_RING,))])
    return pl.pallas_call(
        _moe_kernel,
        grid_spec=grid_spec,
        out_shape=jax.ShapeDtypeStruct((CAP, D_CHUNKS, LANES), F32),
        compiler_params=_cparams(("arbitrary",)),
        name="routed_experts",
    )(first_block, n_blocks, n_used, xs, wg, wu, wd)


COMB_T = 128


def _combine_kernel(pos_hbm, y_hbm, w_ref, base_ref, g_ref, b_ref, hf_ref, hb_ref,
                    pos_smem, buf, routed_ref, pos_sem, row_sem):
    i = pl.program_id(0)
    n = pl.num_programs(0)
    slot = i % 2
    nslot = (i + 1) % 2
    n_groups = COMB_T // SUBLANES

    def pos_copy(blk, s):
        return pltpu.make_async_copy(pos_hbm.at[blk], pos_smem.at[s], pos_sem.at[s])

    def issue_group(s, g):
        for j in range(SUBLANES):
            t = g * SUBLANES + j
            for kk in range(TOP_K):
                src = pos_smem[s, t * TOP_K + kk]
                pltpu.make_async_copy(y_hbm.at[src], buf.at[s, t, kk], row_sem.at[s]).start()

    def reduce_group(s, g):
        rows8 = pl.ds(pl.multiple_of(g * SUBLANES, SUBLANES), SUBLANES)
        acc = None
        for kk in range(TOP_K):
            term = w_ref[rows8, kk:kk + 1] * _rows_to_matrix(buf[s, rows8, kk])
            acc = term if acc is None else acc + term
        routed_ref[rows8, :] = acc

    def group_loop(fn):
        def body(g, carry):
            fn(g)
            return carry
        lax.fori_loop(0, n_groups, body, 0)

    @pl.when(i == 0)
    def _():
        pos_copy(0, 0).start()
        pos_copy(1, 1).start()
        pos_copy(0, 0).wait()
        group_loop(lambda g: issue_group(0, g))

    for kk in range(TOP_K):
        pltpu.make_async_copy(y_hbm.at[pl.ds(0, COMB_T)], buf.at[slot, :, kk], row_sem.at[slot]).wait()

    @pl.when(i + 1 < n)
    def _():
        pos_copy(i + 1, nslot).wait()

        def both(g):
            issue_group(nslot, g)
            reduce_group(slot, g)
        group_loop(both)

    @pl.when(i + 1 == n)
    def _():
        group_loop(lambda g: reduce_group(slot, g))

    @pl.when(i + 2 < n)
    def _():
        pos_copy(i + 2, slot).start()

    h2 = _layer_norm(base_ref[...] + routed_ref[...], g_ref[...], b_ref[...])
    hf_ref[...] = h2
    hb_ref[...] = h2.astype(BF16)


def _combine(pos, y3, wts, base, g, b):
    tm = COMB_T
    n_tiles = N_TOK // tm
    full = lambda r, c: pl.BlockSpec((r, c), lambda i: (0, 0))
    tile = lambda c: pl.BlockSpec((tm, c), lambda i: (i, 0))
    return pl.pallas_call(
        _combine_kernel,
        grid=(n_tiles,),
        in_specs=[pl.BlockSpec(memory_space=pl.ANY), pl.BlockSpec(memory_space=pl.ANY),
                  tile(TOP_K), tile(D_MODEL), full(1, D_MODEL), full(1, D_MODEL)],
        out_specs=[tile(D_MODEL), tile(D_MODEL)],
        out_shape=[jax.ShapeDtypeStruct((N_TOK, D_MODEL), F32),
                   jax.ShapeDtypeStruct((N_TOK, D_MODEL), BF16)],
        scratch_shapes=[pltpu.SMEM((2, tm * TOP_K), I32),
                        pltpu.VMEM((2, tm, TOP_K, D_CHUNKS, LANES), F32),
                        pltpu.VMEM((tm, D_MODEL), F32),
                        pltpu.SemaphoreType.DMA((2,)),
                        pltpu.SemaphoreType.DMA((2,))],
        compiler_params=_cparams(("arbitrary",)),
        name="combine_ln2",
    )(pos.reshape(n_tiles, tm * TOP_K), y3, wts, base, g, b)


def _pack_w_in(w):
    q_lat = w[:, 0:384]
    c_kv = w[:, 384:640]
    k_pe = w[:, 640:672]
    z = w[:, 672:1696]
    xbc = w[:, 1696:3232]
    dt = w[:, 3232:3248]
    gate_a = w[:, 3248:4272]
    gate_b = w[:, 4272:5296]
    zeros = lambda n: jnp.zeros((D_MODEL, n), w.dtype)
    small = jnp.concatenate([zeros(KPE_LANE), k_pe, dt, zeros(LANES - DT_LANE - SSM_HEADS)], axis=1)
    return jnp.concatenate([gate_a, gate_b, z, xbc, q_lat, small, c_kv], axis=1).astype(BF16)


def _pack_w_q(w):
    w = w.reshape(Q_RANK, MLA_HEADS, QK_NOPE + QK_ROPE)
    w = jnp.pad(w, ((0, 0), (0, 0), (0, HEAD_PAD - QK_NOPE - QK_ROPE)))
    return w.reshape(Q_RANK, MLA_HEADS * HEAD_PAD).astype(BF16)


def _pack_w_kv(w):
    w = w.reshape(KV_RANK, MLA_HEADS, QK_NOPE + V_DIM)
    wk = jnp.pad(w[:, :, :QK_NOPE], ((0, 0), (0, 0), (0, HEAD_PAD - QK_NOPE)))
    wv = w[:, :, QK_NOPE:]
    return (wk.reshape(KV_RANK, MLA_HEADS * HEAD_PAD).astype(BF16),
            wv.reshape(KV_RANK, MLA_HEADS * V_DIM).astype(BF16))


def _rope_tables(positions):
    half = QK_ROPE // 2
    inv_freq = jnp.power(ROPE_THETA, -jnp.arange(half, dtype=F32) * (2.0 / QK_ROPE))
    ang = positions.astype(F32).reshape(N_TOK, 1) * inv_freq
    cos, sin = jnp.cos(ang), jnp.sin(ang)
    z = lambda n: jnp.zeros((N_TOK, n), F32)
    tail = HEAD_PAD - QK_NOPE - QK_ROPE
    tc = jnp.concatenate([jnp.ones((N_TOK, QK_NOPE), F32), cos, cos, z(tail)], axis=1)
    ts1 = jnp.concatenate([z(QK_NOPE), -sin, z(half), z(tail)], axis=1)
    ts2 = jnp.concatenate([z(QK_NOPE), z(half), sin, z(tail)], axis=1)
    return tc, ts1, ts2


def _router_pieces(w):
    w_hi = lax.reduce_precision(w, exponent_bits=8, mantissa_bits=7)
    w_mid = (w - w_hi).astype(BF16)
    w_hi = w_hi.astype(BF16)
    return jnp.concatenate([w_hi, w_hi, w_mid], axis=0)


def _head_lane_row(v):
    return jnp.zeros((1, LANES), F32).at[0, DT_LANE:DT_LANE + SSM_HEADS].set(v.astype(F32))


def _expand_matrix():
    r = jnp.arange(LANES)[:, None]
    c = jnp.arange(SSM_INNER)[None, :]
    e = ((r - DT_LANE) == (c // SSM_HEADDIM)).astype(BF16)
    return jnp.concatenate([e, e, e], axis=0)


def kernel(x, p, positions, w_in, q_norm, w_q_up, kv_norm, w_kv_up, conv_w, conv_b, dt_bias, a_log, d_skip,
           ssm_norm, w_attn_br, w_ssm_br, w_o, ln1_g, ln1_b, w_router, router_bias, w_exp_gate, w_exp_up,
           w_exp_down, w_sh_gate, w_sh_up, w_sh_down, w_ple, w_ple_gate, ln2_g, ln2_b):
    tc, ts1, ts2 = _rope_tables(positions)
    e_mat = _expand_matrix()
    hf = x.reshape(N_TOK, D_MODEL)
    hb = hf.astype(BF16)
    row = lambda v: v.reshape(1, -1).astype(F32)
    for i in range(DEPTH):
        proj = _in_proj(hb, _pack_w_in(w_in[i]))
        q = _q_up(proj, row(q_norm[i]), _pack_w_q(w_q_up[i]), tc, ts1, ts2)
        wk, wv = _pack_w_kv(w_kv_up[i])
        k, v = _kv_up(proj, row(kv_norm[i]), wk, wv, tc, ts1, ts2)
        attn = _attention(q, k, v).reshape(N_TOK, MLA_HEADS * V_DIM)
        ssm_y = _ssd(proj, conv_w[i], row(conv_b[i]), _head_lane_row(dt_bias[i]), _head_lane_row(a_log[i]),
                     row(jnp.repeat(d_skip[i], SSM_HEADDIM)), row(ssm_norm[i]), e_mat)
        hf, hb = _merge(attn, ssm_y, proj, hf, w_attn_br[i].astype(BF16), w_ssm_br[i].astype(BF16),
                        w_o[i].astype(BF16), row(ln1_g[i]), row(ln1_b[i]))
        base, idx, wts, cnt = _post(hf, hb, p[i].reshape(N_TOK, PLE_DIM), _router_pieces(w_router[i]),
                                    row(router_bias[i]),
                                    w_sh_gate[i].astype(BF16), w_sh_up[i].astype(BF16),
                                    w_sh_down[i].astype(BF16), w_ple[i].astype(BF16),
                                    w_ple_gate[i].astype(BF16))
        base_slots, counts, pad_start, padded, n_used = _slot_layout(cnt)
        pos = _slot_positions(idx, base_slots)
        xs = _dispatch(counts, pad_start, padded, n_used, pos, hf.reshape(N_TOK, D_CHUNKS, LANES))
        y3 = _moe(pad_start // SLOT_BLOCK, padded // SLOT_BLOCK, n_used, xs,
                  w_exp_gate, w_exp_up, w_exp_down, i)
        hf, hb = _combine(pos, y3, wts, base, row(ln2_g[i]), row(ln2_b[i]))
    return hf.reshape(BATCH, SEQ, D_MODEL)
```

```python
import math

import jax
import jax.numpy as jnp
from jax import lax
from jax.experimental import pallas as pl
from jax.experimental.pallas import tpu as pltpu

F32 = jnp.float32
BF16 = jnp.bfloat16
I32 = jnp.int32
HIGHEST = lax.Precision.HIGHEST

D_MODEL = 1024
BATCH = 4
SEQ = 8192
DEPTH = 2
N_TOK = BATCH * SEQ
MLA_HEADS = 8
QK_NOPE = 64
QK_ROPE = 32
V_DIM = 64
Q_RANK = 384
KV_RANK = 256
ROPE_THETA = 10000.0
SSM_INNER = 1024
SSM_HEADDIM = 64
SSM_HEADS = 16
SSM_GROUPS = 2
SSM_STATE = 128
SSM_CONV = 4
SSM_CHUNK = 128
SSM_CONV_DIM = 1536
N_EXPERTS = 256
TOP_K = 8
N_GROUPS = 8
TOPK_GROUPS = 4
PER_GROUP = N_EXPERTS // N_GROUPS
EXPERT_FF = 256
ROUTED_SCALE = 2.5
PLE_DIM = 256
ALPHA = (2 * DEPTH) ** 0.25
LN_EPS = 1e-5
RMS_EPS = 1e-6

LANES = 128
SUBLANES = 8
VMEM_LIMIT = 48 * 1024 * 1024

COL_GATE_A = 0
COL_GATE_B = 1024
COL_Z = 2048
COL_XBC = 3072
COL_QLAT = 4608
COL_SMALL = 4992
COL_CKV = 5120
PACK_COLS = 5376
KPE_LANE = 64
DT_LANE = 96

HEAD_PAD = 128
N_PAIRS = MLA_HEADS // 2
V_SLAB = 2 * LANES

ROUTE_T = 256
N_RTILES = N_TOK // ROUTE_T
SLOT_BLOCK = 128
N_ASSIGN = N_TOK * TOP_K
N_BLOCKS = N_ASSIGN // SLOT_BLOCK + N_EXPERTS
CAP = N_BLOCKS * SLOT_BLOCK
D_CHUNKS = D_MODEL // LANES
MOE_RING = 6

NEG = float(jnp.finfo(jnp.float32).min)


def _cparams(sem):
    return pltpu.CompilerParams(dimension_semantics=sem, vmem_limit_bytes=VMEM_LIMIT)


def _sigmoid(x):
    return 1.0 / (1.0 + jnp.exp(-x))


def _silu(x):
    return x * _sigmoid(x)


def _layer_norm(x, g, b):
    mu = jnp.mean(x, axis=-1, keepdims=True)
    xc = x - mu
    var = jnp.mean(xc * xc, axis=-1, keepdims=True)
    return xc * lax.rsqrt(var + LN_EPS) * g + b


def _rms_norm(x, g):
    return x * lax.rsqrt(jnp.mean(x * x, axis=-1, keepdims=True) + RMS_EPS) * g


def _dot(a, b):
    return jnp.dot(a, b, preferred_element_type=F32)


def _swap_sublane_major(x):
    groups = x.shape[0]
    for d in (4, 2, 1):
        y = x.reshape(groups, SUBLANES // (2 * d), 2, d, SUBLANES, LANES)
        lo, hi = y[:, :, 0], y[:, :, 1]
        shp = lo.shape
        keep = (lax.broadcasted_iota(I32, shp, 3) & d) == 0
        rot = lambda v, s: pltpu.roll(v.reshape(-1, SUBLANES, LANES), s, 1).reshape(shp)
        new_lo = jnp.where(keep, lo, rot(hi, d))
        new_hi = jnp.where(keep, rot(lo, SUBLANES - d), hi)
        x = jnp.stack([new_lo, new_hi], axis=2).reshape(groups, SUBLANES, SUBLANES, LANES)
    return x


def _rows_to_matrix(x3):
    rows = x3.shape[0]
    t = _swap_sublane_major(x3.reshape(rows // SUBLANES, SUBLANES, D_CHUNKS, LANES))
    return jnp.concatenate([t[:, cc].reshape(rows, LANES) for cc in range(D_CHUNKS)], axis=1)


def _matrix_to_rows(y):
    rows = y.shape[0]
    t = jnp.stack([y[:, cc * LANES:(cc + 1) * LANES].reshape(rows // SUBLANES, SUBLANES, LANES)
                   for cc in range(D_CHUNKS)], axis=1)
    return _swap_sublane_major(t).reshape(rows, D_CHUNKS, LANES)


def _mm_kernel(x_ref, w_ref, o_ref):
    o_ref[...] = _dot(x_ref[...], w_ref[...]).astype(o_ref.dtype)


def _in_proj(hb, w_pack):
    tm, tn = 1024, 1792
    return pl.pallas_call(
        _mm_kernel,
        grid=(PACK_COLS // tn, N_TOK // tm),
        in_specs=[pl.BlockSpec((tm, D_MODEL), lambda j, i: (i, 0)),
                  pl.BlockSpec((D_MODEL, tn), lambda j, i: (0, j))],
        out_specs=pl.BlockSpec((tm, tn), lambda j, i: (i, j)),
        out_shape=jax.ShapeDtypeStruct((N_TOK, PACK_COLS), F32),
        compiler_params=_cparams(("parallel", "parallel")),
        name="in_proj",
    )(hb, w_pack)


def _rope128(x, c, s1, s2):
    return x * c + pltpu.roll(x, LANES - 16, 1) * s1 + pltpu.roll(x, 16, 1) * s2


def _q_up_kernel(ql_ref, g_ref, w_ref, c_ref, s1_ref, s2_ref, q_ref):
    y = _rms_norm(ql_ref[...], g_ref[...]).astype(BF16)
    q = _dot(y, w_ref[...])
    c, s1, s2 = c_ref[...], s1_ref[...], s2_ref[...]
    scale = (QK_NOPE + QK_ROPE) ** -0.5 * math.log2(math.e)
    for h in range(MLA_HEADS):
        qh = _rope128(q[:, h * HEAD_PAD:(h + 1) * HEAD_PAD], c, s1, s2)
        q_ref[h] = (qh * scale).astype(BF16)


def _q_up(proj, g, wq, tc, ts1, ts2):
    tm = 512
    spt = SEQ // tm
    tab = pl.BlockSpec((tm, LANES), lambda i: (i, 0))
    return pl.pallas_call(
        _q_up_kernel,
        grid=(N_TOK // tm,),
        in_specs=[pl.BlockSpec((tm, Q_RANK), lambda i: (i, COL_QLAT // Q_RANK)),
                  pl.BlockSpec((1, Q_RANK), lambda i: (0, 0)),
                  pl.BlockSpec((Q_RANK, MLA_HEADS * HEAD_PAD), lambda i: (0, 0)),
                  tab, tab, tab],
        out_specs=pl.BlockSpec((None, MLA_HEADS, tm, HEAD_PAD), lambda i: (i // spt, 0, i % spt, 0)),
        out_shape=jax.ShapeDtypeStruct((BATCH, MLA_HEADS, SEQ, HEAD_PAD), BF16),
        compiler_params=_cparams(("parallel",)),
        name="q_up",
    )(proj, g, wq, tc, ts1, ts2)


def _kv_up_kernel(ckv_ref, small_ref, g_ref, wk_ref, wv_ref, c_ref, s1_ref, s2_ref, k_ref, v_ref):
    y = _rms_norm(ckv_ref[...], g_ref[...]).astype(BF16)
    k_all = _dot(y, wk_ref[...])
    v_all = _dot(y, wv_ref[...])
    lane = lax.broadcasted_iota(I32, small_ref.shape, 1)
    kpe_raw = jnp.where((lane >= KPE_LANE) & (lane < KPE_LANE + QK_ROPE), small_ref[...], 0.0)
    kpe = _rope128(kpe_raw, c_ref[...], s1_ref[...], s2_ref[...])
    for h in range(MLA_HEADS):
        k_ref[h] = (k_all[:, h * HEAD_PAD:(h + 1) * HEAD_PAD] + kpe).astype(BF16)
    ones = jnp.ones((v_all.shape[0], LANES), F32)
    for j in range(N_PAIRS):
        v_ref[j] = jnp.concatenate([v_all[:, j * LANES:(j + 1) * LANES], ones], axis=1).astype(BF16)


def _kv_up(proj, g, wk, wv, tc, ts1, ts2):
    tm = 512
    spt = SEQ // tm
    tab = pl.BlockSpec((tm, LANES), lambda i: (i, 0))
    return pl.pallas_call(
        _kv_up_kernel,
        grid=(N_TOK // tm,),
        in_specs=[pl.BlockSpec((tm, KV_RANK), lambda i: (i, COL_CKV // KV_RANK)),
                  pl.BlockSpec((tm, LANES), lambda i: (i, COL_SMALL // LANES)),
                  pl.BlockSpec((1, KV_RANK), lambda i: (0, 0)),
                  pl.BlockSpec((KV_RANK, MLA_HEADS * HEAD_PAD), lambda i: (0, 0)),
                  pl.BlockSpec((KV_RANK, MLA_HEADS * V_DIM), lambda i: (0, 0)),
                  tab, tab, tab],
        out_specs=[pl.BlockSpec((None, MLA_HEADS, tm, HEAD_PAD), lambda i: (i // spt, 0, i % spt, 0)),
                   pl.BlockSpec((None, N_PAIRS, tm, V_SLAB), lambda i: (i // spt, 0, i % spt, 0))],
        out_shape=[jax.ShapeDtypeStruct((BATCH, MLA_HEADS, SEQ, HEAD_PAD), BF16),
                   jax.ShapeDtypeStruct((BATCH, N_PAIRS, SEQ, V_SLAB), BF16)],
        compiler_params=_cparams(("parallel",)),
        name="kv_up",
    )(proj, proj, g, wk, wv, tc, ts1, ts2)


ATT_T = 512


def _attn_kernel(q_ref, k_ref, v_ref, o_ref, m_ref, acc_ref):
    qi = pl.program_id(2)
    t = ATT_T
    m_ref[...] = jnp.full(m_ref.shape, -jnp.inf, F32)
    acc_ref[...] = jnp.zeros(acc_ref.shape, F32)

    def kv_tile(ks, width, masked):
        v = v_ref[pl.ds(ks, width), :]
        scores = [lax.dot_general(q_ref[hh], k_ref[hh, pl.ds(ks, width), :], (((1,), (1,)), ((), ())),
                                  preferred_element_type=F32) for hh in range(2)]
        alphas, pvs = [], []
        for hh in range(2):
            s = scores[hh]
            if masked:
                row = lax.broadcasted_iota(I32, s.shape, 0)
                col = lax.broadcasted_iota(I32, s.shape, 1)
                s = jnp.where(ks + col <= qi * t + row, s, NEG)
            m_prev = m_ref[hh]
            m_new = jnp.maximum(m_prev, jnp.max(s, axis=-1, keepdims=True))
            alphas.append(jnp.exp2(m_prev - m_new))
            p = jnp.exp2(s - jnp.concatenate([m_new] * (width // LANES), axis=1))
            pvs.append(_dot(p.astype(BF16), v))
            m_ref[hh] = m_new
        for hh in range(2):
            acc_ref[hh] = jnp.concatenate([alphas[hh], alphas[hh]], axis=1) * acc_ref[hh] + pvs[hh]

    def body(kp, carry):
        kv_tile(pl.multiple_of(kp * 2 * t, 2 * t), 2 * t, False)
        return carry

    lax.fori_loop(0, qi // 2, body, 0)

    @pl.when(qi % 2 == 0)
    def _():
        kv_tile(pl.multiple_of(qi * t, t), t, True)

    @pl.when(qi % 2 == 1)
    def _():
        kv_tile(pl.multiple_of((qi - 1) * t, t), 2 * t, True)

    lane = lax.broadcasted_iota(I32, o_ref.shape, 1)
    a0 = acc_ref[0]
    a1 = acc_ref[1]
    o0 = a0[:, :LANES] / a0[:, LANES:]
    o1 = a1[:, :LANES] / a1[:, LANES:]
    o_ref[...] = jnp.where(lane < V_DIM, o0, o1).astype(o_ref.dtype)


def _attention(q, k, v):
    t = ATT_T
    return pl.pallas_call(
        _attn_kernel,
        grid=(BATCH, N_PAIRS, SEQ // t),
        in_specs=[pl.BlockSpec((None, 2, t, HEAD_PAD), lambda b, j, qi: (b, j, qi, 0)),
                  pl.BlockSpec((None, 2, SEQ, HEAD_PAD), lambda b, j, qi: (b, j, 0, 0)),
                  pl.BlockSpec((None, None, SEQ, V_SLAB), lambda b, j, qi: (b, j, 0, 0))],
        out_specs=pl.BlockSpec((None, t, LANES), lambda b, j, qi: (b, qi, j)),
        out_shape=jax.ShapeDtypeStruct((BATCH, SEQ, MLA_HEADS * V_DIM), BF16),
        scratch_shapes=[pltpu.VMEM((2, t, LANES), F32), pltpu.VMEM((2, t, V_SLAB), F32)],
        compiler_params=_cparams(("parallel", "parallel", "parallel")),
        name="mla_attention",
    )(q, k, v)


HALO = SUBLANES
HEADS_PER_GROUP = SSM_HEADS // SSM_GROUPS
GROUP_W = SSM_INNER // SSM_GROUPS


def _ssd_kernel(xbc_ref, halo_ref, z_ref, small_ref, cw_ref, cb_ref, dtb_ref, alog_ref, dexp_ref,
                nw_ref, e_ref, y_ref, st_ref, cat_ref):
    c = pl.program_id(1)
    L = SSM_CHUNK

    @pl.when(c == 0)
    def _():
        st_ref[...] = jnp.zeros(st_ref.shape, F32)

    cat_ref[0:HALO, :] = jnp.where(c == 0, 0.0, halo_ref[...])
    cat_ref[HALO:HALO + L, :] = xbc_ref[...]
    acc = jnp.broadcast_to(cb_ref[...], (L, SSM_CONV_DIM))
    for kk in range(SSM_CONV):
        off = HALO - (SSM_CONV - 1) + kk
        acc = acc + cw_ref[kk:kk + 1, :] * cat_ref[off:off + L, :]
    xc = _silu(acc)
    xs = xc[:, :SSM_INNER]
    bm = xc[:, SSM_INNER:SSM_INNER + SSM_GROUPS * SSM_STATE]
    cm = xc[:, SSM_INNER + SSM_GROUPS * SSM_STATE:]

    lane = lax.broadcasted_iota(I32, (L, LANES), 1)
    dt_lanes = (lane >= DT_LANE) & (lane < DT_LANE + SSM_HEADS)
    dt_in = small_ref[...] + dtb_ref[...]
    dt = jnp.maximum(dt_in, 0.0) + jnp.log1p(jnp.exp(-jnp.abs(dt_in)))
    dt = jnp.where(dt_lanes, dt, 0.0)
    a = -jnp.exp(alog_ref[...])
    a_dt = dt * a
    row = lax.broadcasted_iota(I32, (L, L), 0)
    col = lax.broadcasted_iota(I32, (L, L), 1)
    tri = row >= col
    a_cum = jnp.dot(tri.astype(F32), a_dt, precision=HIGHEST, preferred_element_type=F32)
    a_last = a_cum[L - 1:L, :]
    t = jnp.concatenate([dt, jnp.exp(a_cum), jnp.exp(a_last - a_cum)], axis=0)
    t_hi = t.astype(BF16)
    r1 = t - t_hi.astype(F32)
    t_mid = r1.astype(BF16)
    t_lo = (r1 - t_mid.astype(F32)).astype(BF16)
    expanded = _dot(jnp.concatenate([t_hi, t_mid, t_lo], axis=1), e_ref[...])
    x_dt = xs * expanded[0:L]
    eac_x = expanded[L:2 * L]
    ds_x = expanded[2 * L:3 * L]
    cd_x = eac_x[L - 1:L, :]
    xd = (x_dt * ds_x).astype(BF16)
    x_bf = x_dt.astype(BF16)
    a_cum_t = a_cum.T
    lane_h = lax.broadcasted_iota(I32, (L, LANES), 1)

    bgs = [bm[:, g * SSM_STATE:(g + 1) * SSM_STATE].astype(BF16) for g in range(SSM_GROUPS)]
    cgs = [cm[:, g * SSM_STATE:(g + 1) * SSM_STATE].astype(BF16) for g in range(SSM_GROUPS)]
    cbs = [lax.dot_general(cgs[g], bgs[g], (((1,), (1,)), ((), ())), preferred_element_type=F32)
           for g in range(SSM_GROUPS)]
    st_prevs = [st_ref[g] for g in range(SSM_GROUPS)]
    y_offs = [_dot(cgs[g], st_prevs[g].astype(BF16)) * eac_x[:, g * GROUP_W:(g + 1) * GROUP_W]
              for g in range(SSM_GROUPS)]
    upds = [lax.dot_general(bgs[g], xd[:, g * GROUP_W:(g + 1) * GROUP_W], (((0,), (0,)), ((), ())),
                            preferred_element_type=F32) for g in range(SSM_GROUPS)]

    y_parts = []
    for g in range(SSM_GROUPS):
        cb, y_off = cbs[g], y_offs[g]
        for jp in range(HEADS_PER_GROUP // 2):
            pair = g * (HEADS_PER_GROUP // 2) + jp
            xp = x_bf[:, pair * LANES:(pair + 1) * LANES]
            outs = []
            for hh in range(2):
                hl = DT_LANE + 2 * pair + hh
                seg = a_cum[:, hl:hl + 1] - a_cum_t[hl:hl + 1, :]
                decay = jnp.where(tri, jnp.exp(seg), 0.0)
                outs.append(_dot((cb * decay).astype(BF16), xp))
            y_parts.append(jnp.where(lane_h < SSM_HEADDIM, outs[0], outs[1])
                           + y_off[:, jp * LANES:(jp + 1) * LANES])
        st_ref[g] = st_prevs[g] * cd_x[:, g * GROUP_W:(g + 1) * GROUP_W] + upds[g]

    y = jnp.concatenate(y_parts, axis=1) + xs * dexp_ref[...]
    y = y * _silu(z_ref[...])
    normed = []
    for g in range(SSM_GROUPS):
        yg = y[:, g * GROUP_W:(g + 1) * GROUP_W]
        normed.append(yg * lax.rsqrt(jnp.mean(yg * yg, axis=-1, keepdims=True) + RMS_EPS))
    y_ref[...] = (jnp.concatenate(normed, axis=1) * nw_ref[...]).astype(y_ref.dtype)


def _ssd(proj, cw, cb, dtb, alog, dexp, nw, e_mat):
    L = SSM_CHUNK
    nc = SEQ // L
    xbc_blk = COL_XBC // SSM_CONV_DIM

    def row1(w):
        return pl.BlockSpec((1, w), lambda b, c: (0, 0))

    return pl.pallas_call(
        _ssd_kernel,
        grid=(BATCH, nc),
        in_specs=[pl.BlockSpec((L, SSM_CONV_DIM), lambda b, c: (b * nc + c, xbc_blk)),
                  pl.BlockSpec((HALO, SSM_CONV_DIM),
                               lambda b, c: (jnp.maximum((b * nc + c) * (L // HALO) - 1, 0), xbc_blk)),
                  pl.BlockSpec((L, SSM_INNER), lambda b, c: (b * nc + c, COL_Z // SSM_INNER)),
                  pl.BlockSpec((L, LANES), lambda b, c: (b * nc + c, COL_SMALL // LANES)),
                  pl.BlockSpec((SSM_CONV, SSM_CONV_DIM), lambda b, c: (0, 0)),
                  row1(SSM_CONV_DIM), row1(LANES), row1(LANES), row1(SSM_INNER), row1(SSM_INNER),
                  pl.BlockSpec((3 * LANES, SSM_INNER), lambda b, c: (0, 0))],
        out_specs=pl.BlockSpec((L, SSM_INNER), lambda b, c: (b * nc + c, 0)),
        out_shape=jax.ShapeDtypeStruct((N_TOK, SSM_INNER), BF16),
        scratch_shapes=[pltpu.VMEM((SSM_GROUPS, SSM_STATE, GROUP_W), F32),
                        pltpu.VMEM((HALO + L, SSM_CONV_DIM), F32)],
        compiler_params=_cparams(("parallel", "arbitrary")),
        name="mamba2_ssd",
    )(proj, proj, proj, proj, cw, cb, dtb, alog, dexp, nw, e_mat)


def _merge_kernel(attn_ref, ssm_ref, ga_ref, gb_ref, h_ref, wa_ref, ws_ref, wo_ref, g_ref, b_ref,
                  hf_ref, hb_ref):
    ya = _dot(attn_ref[...], wa_ref[...])
    yb = _dot(ssm_ref[...], ws_ref[...])
    mix = _sigmoid(ga_ref[...]) * ya + _sigmoid(gb_ref[...]) * yb
    mixed = _dot(mix.astype(BF16), wo_ref[...])
    h1 = _layer_norm(ALPHA * h_ref[...] + mixed, g_ref[...], b_ref[...])
    hf_ref[...] = h1
    hb_ref[...] = h1.astype(BF16)


def _merge(attn, ssm_y, proj, hf, wa, ws, wo, g, b):
    tm = 512
    full = lambda r, c: pl.BlockSpec((r, c), lambda i: (0, 0))
    tile = lambda c, j=0: pl.BlockSpec((tm, c), lambda i: (i, j))
    return pl.pallas_call(
        _merge_kernel,
        grid=(N_TOK // tm,),
        in_specs=[tile(MLA_HEADS * V_DIM), tile(SSM_INNER), tile(D_MODEL, COL_GATE_A // D_MODEL),
                  tile(D_MODEL, COL_GATE_B // D_MODEL), tile(D_MODEL),
                  full(MLA_HEADS * V_DIM, D_MODEL), full(SSM_INNER, D_MODEL), full(D_MODEL, D_MODEL),
                  full(1, D_MODEL), full(1, D_MODEL)],
        out_specs=[tile(D_MODEL), tile(D_MODEL)],
        out_shape=[jax.ShapeDtypeStruct((N_TOK, D_MODEL), F32),
                   jax.ShapeDtypeStruct((N_TOK, D_MODEL), BF16)],
        compiler_params=_cparams(("parallel",)),
        name="merge_ln1",
    )(attn, ssm_y, proj, proj, hf, wa, ws, wo, g, b)


def _first_argmax(vals, lane):
    m = jnp.max(vals, axis=-1, keepdims=True)
    idx = jnp.min(jnp.where(vals == m, lane, float(N_EXPERTS)), axis=-1, keepdims=True)
    return m, idx


def _post_kernel(hf_ref, hb_ref, p_ref, wr_ref, rb_ref, wsg_ref, wsu_ref, wsd_ref, wp_ref, wpg_ref,
                 base_ref, idx_ref, wt_ref, cnt_ref):
    hf = hf_ref[...]
    hb = hb_ref[...]
    tm = hf.shape[0]
    h_mid = (hf - hb.astype(F32)).astype(BF16)
    logits = _dot(jnp.concatenate([hb, h_mid, hb], axis=1), wr_ref[...])
    scores = _sigmoid(logits)
    sel = scores + rb_ref[...]
    lane_i = lax.broadcasted_iota(I32, (tm, N_EXPERTS), 1)
    grp = lane_i // PER_GROUP
    lane = lane_i.astype(F32)

    grp_scores = []
    for g in range(N_GROUPS):
        vals = jnp.where(grp == g, sel, -jnp.inf)
        m1, i1 = _first_argmax(vals, lane)
        m2 = jnp.max(jnp.where(lane == i1, -jnp.inf, vals), axis=-1, keepdims=True)
        grp_scores.append(m1 + m2)
    ranks = [jnp.zeros((tm, 1), I32) for _ in range(N_GROUPS)]
    for g in range(N_GROUPS):
        for o in range(g + 1, N_GROUPS):
            later_wins = (grp_scores[o] > grp_scores[g]).astype(I32)
            ranks[g] = ranks[g] + later_wins
            ranks[o] = ranks[o] + (1 - later_wins)
    keep = jnp.zeros((tm, N_EXPERTS), jnp.bool_)
    for g in range(N_GROUPS):
        keep = keep | ((grp == g) & (ranks[g] < TOPK_GROUPS))
    masked = jnp.where(keep, sel, -jnp.inf)

    lane_k = lax.broadcasted_iota(I32, (tm, TOP_K), 1)
    idx_out = jnp.zeros((tm, TOP_K), I32)
    w_out = jnp.zeros((tm, TOP_K), F32)
    chosen = jnp.zeros((tm, N_EXPERTS), F32)
    for kk in range(TOP_K):
        _, ik = _first_argmax(masked, lane)
        hit = lane == ik
        wk = jnp.sum(jnp.where(hit, scores, 0.0), axis=-1, keepdims=True)
        masked = jnp.where(hit, -jnp.inf, masked)
        chosen = jnp.where(hit, 1.0, chosen)
        idx_out = jnp.where(lane_k == kk, ik.astype(I32), idx_out)
        w_out = jnp.where(lane_k == kk, wk, w_out)
    w_out = w_out / jnp.sum(w_out, axis=-1, keepdims=True) * ROUTED_SCALE
    idx_ref[...] = idx_out
    wt_ref[...] = w_out
    cnt_ref[...] = jnp.sum(chosen, axis=0, keepdims=True)

    shared = _dot((_silu(_dot(hb, wsg_ref[...])) * _dot(hb, wsu_ref[...])).astype(BF16), wsd_ref[...])
    ple = _dot(p_ref[...].astype(BF16), wp_ref[...]) * _sigmoid(_dot(hb, wpg_ref[...]))
    base_ref[...] = ALPHA * hf + shared + ple


def _post(hf, hb, p, wr, rb, wsg, wsu, wsd, wp, wpg):
    tm = ROUTE_T
    full = lambda r, c: pl.BlockSpec((r, c), lambda i: (0, 0))
    tile = lambda c: pl.BlockSpec((tm, c), lambda i: (i, 0))
    return pl.pallas_call(
        _post_kernel,
        grid=(N_RTILES,),
        in_specs=[tile(D_MODEL), tile(D_MODEL), tile(PLE_DIM),
                  full(3 * D_MODEL, N_EXPERTS), full(1, N_EXPERTS),
                  full(D_MODEL, EXPERT_FF), full(D_MODEL, EXPERT_FF), full(EXPERT_FF, D_MODEL),
                  full(PLE_DIM, D_MODEL), full(D_MODEL, D_MODEL)],
        out_specs=[tile(D_MODEL), tile(TOP_K), tile(TOP_K),
                   pl.BlockSpec((None, 1, N_EXPERTS), lambda i: (i, 0, 0))],
        out_shape=[jax.ShapeDtypeStruct((N_TOK, D_MODEL), F32),
                   jax.ShapeDtypeStruct((N_TOK, TOP_K), I32),
                   jax.ShapeDtypeStruct((N_TOK, TOP_K), F32),
                   jax.ShapeDtypeStruct((N_RTILES, 1, N_EXPERTS), F32)],
        compiler_params=_cparams(("parallel",)),
        name="router_shared_ple",
    )(hf, hb, p, wr, rb, wsg, wsu, wsd, wp, wpg)


def _pos_kernel(idx_ref, base_ref, pos_ref):
    idx = idx_ref[...]
    tm = idx.shape[0]
    lane = lax.broadcasted_iota(I32, (tm, N_EXPERTS), 1)
    hits = [lane == idx[:, kk:kk + 1] for kk in range(TOP_K)]
    chosen = hits[0]
    for kk in range(1, TOP_K):
        chosen = chosen | hits[kk]
    r = lax.broadcasted_iota(I32, (tm, tm), 0)
    c = lax.broadcasted_iota(I32, (tm, tm), 1)
    earlier = jnp.where(c < r, 1.0, 0.0).astype(BF16)
    rank = _dot(earlier, jnp.where(chosen, 1.0, 0.0).astype(BF16))
    slot = rank + base_ref[...]
    lane_k = lax.broadcasted_iota(I32, (tm, TOP_K), 1)
    out = jnp.zeros((tm, TOP_K), I32)
    for kk in range(TOP_K):
        pk = jnp.sum(jnp.where(hits[kk], slot, 0.0), axis=-1, keepdims=True)
        out = jnp.where(lane_k == kk, pk.astype(I32), out)
    pos_ref[...] = out


def _slot_positions(idx, base):
    tm = ROUTE_T
    return pl.pallas_call(
        _pos_kernel,
        grid=(N_RTILES,),
        in_specs=[pl.BlockSpec((tm, TOP_K), lambda i: (i, 0)),
                  pl.BlockSpec((None, 1, N_EXPERTS), lambda i: (i, 0, 0))],
        out_specs=pl.BlockSpec((tm, TOP_K), lambda i: (i, 0)),
        out_shape=jax.ShapeDtypeStruct((N_TOK, TOP_K), I32),
        compiler_params=_cparams(("parallel",)),
        name="slot_positions",
    )(idx, base)


def _slot_layout(cnt):
    counts_te = cnt.reshape(N_RTILES, N_EXPERTS).astype(I32)
    counts = jnp.sum(counts_te, axis=0)
    padded = (counts + SLOT_BLOCK - 1) // SLOT_BLOCK * SLOT_BLOCK
    pad_end = jnp.cumsum(padded)
    pad_start = pad_end - padded
    tile_off = jnp.cumsum(counts_te, axis=0) - counts_te
    base = (pad_start[None, :] + tile_off).astype(F32).reshape(N_RTILES, 1, N_EXPERTS)
    n_used = (pad_end[-1] // SLOT_BLOCK).astype(I32).reshape(1)
    return base, counts, pad_start, padded, n_used


DISP_T = 1024


def _dispatch_kernel(cnt_ref, ps_ref, pd_ref, nu_ref, pos_hbm, h3_ref, x_hbm,
                     pos_smem, zbuf, pos_sem, row_sem, pad_sem):
    i = pl.program_id(0)
    n = pl.num_programs(0)

    def pos_copy(blk, slot):
        return pltpu.make_async_copy(pos_hbm.at[blk], pos_smem.at[slot], pos_sem.at[slot])

    def pad_row(e, r):
        return pltpu.make_async_copy(zbuf.at[0], x_hbm.at[ps_ref[e] + r], pad_sem)

    def pad_block(b):
        return pltpu.make_async_copy(zbuf, x_hbm.at[pl.ds(b * SLOT_BLOCK, SLOT_BLOCK)], pad_sem)

    def for_each_pad(fn_row, fn_block):
        def per_expert(e, carry):
            def per_row(r, c2):
                fn_row(e, r)
                return c2
            return lax.fori_loop(cnt_ref[e], pd_ref[e], per_row, carry)
        lax.fori_loop(0, N_EXPERTS, per_expert, 0)

        def per_block(b, carry):
            fn_block(b)
            return carry
        lax.fori_loop(nu_ref[0], N_BLOCKS, per_block, 0)

    @pl.when(i == 0)
    def _():
        pos_copy(0, 0).start()
        pos_copy(1, 1).start()
        zbuf[...] = jnp.zeros(zbuf.shape, F32)
        for_each_pad(lambda e, r: pad_row(e, r).start(), lambda b: pad_block(b).start())
        for_each_pad(lambda e, r: pad_row(e, r).wait(), lambda b: pad_block(b).wait())

    slot = i % 2
    pos_copy(i, slot).wait()

    def body(t, carry):
        for kk in range(TOP_K):
            dst = pos_smem[slot, t * TOP_K + kk]
            pltpu.make_async_copy(h3_ref.at[t], x_hbm.at[dst], row_sem).start()
        return carry
    lax.fori_loop(0, DISP_T, body, 0)

    @pl.when(i + 2 < n)
    def _():
        pos_copy(i + 2, slot).start()

    for kk in range(TOP_K):
        pltpu.make_async_copy(h3_ref, x_hbm.at[pl.ds(0, DISP_T)], row_sem).wait()


def _dispatch(counts, pad_start, padded, n_used, pos, h3):
    n_tiles = N_TOK // DISP_T
    grid_spec = pltpu.PrefetchScalarGridSpec(
        num_scalar_prefetch=4,
        grid=(n_tiles,),
        in_specs=[pl.BlockSpec(memory_space=pl.ANY),
                  pl.BlockSpec((DISP_T, D_CHUNKS, LANES), lambda i, *_: (i, 0, 0))],
        out_specs=pl.BlockSpec(memory_space=pl.ANY),
        scratch_shapes=[pltpu.SMEM((2, DISP_T * TOP_K), I32),
                        pltpu.VMEM((SLOT_BLOCK, D_CHUNKS, LANES), F32),
                        pltpu.SemaphoreType.DMA((2,)),
                        pltpu.SemaphoreType.DMA,
                        pltpu.SemaphoreType.DMA])
    return pl.pallas_call(
        _dispatch_kernel,
        grid_spec=grid_spec,
        out_shape=jax.ShapeDtypeStruct((CAP, D_CHUNKS, LANES), F32),
        compiler_params=_cparams(("arbitrary",)),
        name="dispatch_scatter",
    )(counts, pad_start, padded, n_used, pos.reshape(n_tiles, DISP_T * TOP_K), h3)


def _moe_kernel(fb_ref, nb_ref, nu_ref, x_hbm, wg_ref, wu_ref, wd_ref, y_hbm,
                xbuf, ybuf, wg_bf, wu_bf, wd_bf, in_sem, out_sem):
    e = pl.program_id(0)
    first = fb_ref[e]
    nb = nb_ref[e]
    n_used = nu_ref[0]

    def rows(g):
        return pl.ds(pl.multiple_of(g * SLOT_BLOCK, SLOT_BLOCK), SLOT_BLOCK)

    def x_copy(g):
        slot = g % MOE_RING
        return pltpu.make_async_copy(x_hbm.at[rows(g)], xbuf.at[slot], in_sem.at[slot])

    def y_copy(g):
        slot = g % MOE_RING
        return pltpu.make_async_copy(ybuf.at[slot], y_hbm.at[rows(g)], out_sem.at[slot])

    @pl.when(e == 0)
    def _():
        for g in range(MOE_RING - 2):
            x_copy(g).start()

    def run_blocks(g, count):
        for j in range(count):
            x_copy(g + j).wait()
        for j in range(count):
            nxt = g + j + MOE_RING - 2

            @pl.when(nxt < n_used)
            def _():
                x_copy(nxt).start()

        x3 = jnp.concatenate([xbuf[(g + j) % MOE_RING] for j in range(count)], axis=0)
        x = _rows_to_matrix(x3).astype(BF16)
        hid = (_silu(_dot(x, wg_bf[...])) * _dot(x, wu_bf[...])).astype(BF16)
        y3 = _matrix_to_rows(_dot(hid, wd_bf[...]))
        for j in range(count):
            @pl.when(g + j >= MOE_RING)
            def _():
                y_copy(g + j - MOE_RING).wait()

            ybuf[(g + j) % MOE_RING] = y3[j * SLOT_BLOCK:(j + 1) * SLOT_BLOCK]
            y_copy(g + j).start()

    @pl.when(nb > 0)
    def _():
        wg_bf[...] = wg_ref[...].astype(BF16)
        wu_bf[...] = wu_ref[...].astype(BF16)
        wd_bf[...] = wd_ref[...].astype(BF16)

        def pair(p, carry):
            run_blocks(first + 2 * p, 2)
            return carry

        lax.fori_loop(0, nb // 2, pair, 0)

        @pl.when(nb % 2 == 1)
        def _():
            run_blocks(first + nb - 1, 1)

    @pl.when(e == N_EXPERTS - 1)
    def _():
        for j in range(MOE_RING):
            y_copy(n_used - 1 - j).wait()
        ybuf[0] = jnp.zeros(ybuf.shape[1:], F32)

        def tail_copy(b):
            return pltpu.make_async_copy(
                ybuf.at[0], y_hbm.at[pl.ds(pl.multiple_of(b * SLOT_BLOCK, SLOT_BLOCK), SLOT_BLOCK)],
                out_sem.at[0])

        def start(b, carry):
            tail_copy(b).start()
            return carry

        def wait(b, carry):
            tail_copy(b).wait()
            return carry

        lax.fori_loop(nu_ref[0], N_BLOCKS, start, 0)
        lax.fori_loop(nu_ref[0], N_BLOCKS, wait, 0)


def _moe(first_block, n_blocks, n_used, xs, wg, wu, wd, layer):
    wspec = lambda r, c: pl.BlockSpec((None, None, r, c), lambda e, *_: (layer, e, 0, 0))
    grid_spec = pltpu.PrefetchScalarGridSpec(
        num_scalar_prefetch=3,
        grid=(N_EXPERTS,),
        in_specs=[pl.BlockSpec(memory_space=pl.ANY),
                  wspec(D_MODEL, EXPERT_FF), wspec(D_MODEL, EXPERT_FF), wspec(EXPERT_FF, D_MODEL)],
        out_specs=pl.BlockSpec(memory_space=pl.ANY),
        scratch_shapes=[pltpu.VMEM((MOE_RING, SLOT_BLOCK, D_CHUNKS, LANES), F32),
                        pltpu.VMEM((MOE_RING, SLOT_BLOCK, D_CHUNKS, LANES), F32),
                        pltpu.VMEM((D_MODEL, EXPERT_FF), BF16),
                        pltpu.VMEM((D_MODEL, EXPERT_FF), BF16),
                        pltpu.VMEM((EXPERT_FF, D_MODEL), BF16),
                        pltpu.SemaphoreType.DMA((MOE_RING,)),
                        pltpu.SemaphoreType.DMA((MOE_RING,))])
    return pl.pallas_call(
        _moe_kernel,
        grid_spec=grid_spec,
        out_shape=jax.ShapeDtypeStruct((CAP, D_CHUNKS, LANES), F32),
        compiler_params=_cparams(("arbitrary",)),
        name="routed_experts",
    )(first_block, n_blocks, n_used, xs, wg, wu, wd)


COMB_T = 256


def _combine_kernel(pos_hbm, y_hbm, w_ref, base_ref, g_ref, b_ref, hf_ref, hb_ref,
                    pos_smem, buf, routed_ref, pos_sem, row_sem):
    i = pl.program_id(0)
    n = pl.num_programs(0)
    slot = i % 2
    nslot = (i + 1) % 2
    n_groups = COMB_T // SUBLANES

    def pos_copy(blk, s):
        return pltpu.make_async_copy(pos_hbm.at[blk], pos_smem.at[s], pos_sem.at[s])

    def issue_group(s, g):
        for j in range(SUBLANES):
            t = g * SUBLANES + j
            for kk in range(TOP_K):
                src = pos_smem[s, t * TOP_K + kk]
                pltpu.make_async_copy(y_hbm.at[src], buf.at[s, t, kk], row_sem.at[s]).start()

    def reduce_group(s, g):
        rows8 = pl.ds(pl.multiple_of(g * SUBLANES, SUBLANES), SUBLANES)
        acc = None
        for kk in range(TOP_K):
            term = w_ref[rows8, kk:kk + 1] * _rows_to_matrix(buf[s, rows8, kk])
            acc = term if acc is None else acc + term
        routed_ref[rows8, :] = acc

    def group_loop(fn):
        def body(g, carry):
            fn(g)
            return carry
        lax.fori_loop(0, n_groups, body, 0)

    @pl.when(i == 0)
    def _():
        pos_copy(0, 0).start()
        pos_copy(1, 1).start()
        pos_copy(0, 0).wait()
        group_loop(lambda g: issue_group(0, g))

    for kk in range(TOP_K):
        pltpu.make_async_copy(y_hbm.at[pl.ds(0, COMB_T)], buf.at[slot, :, kk], row_sem.at[slot]).wait()

    @pl.when(i + 1 < n)
    def _():
        pos_copy(i + 1, nslot).wait()

        def both(g):
            issue_group(nslot, g)
            reduce_group(slot, g)
        group_loop(both)

    @pl.when(i + 1 == n)
    def _():
        group_loop(lambda g: reduce_group(slot, g))

    @pl.when(i + 2 < n)
    def _():
        pos_copy(i + 2, slot).start()

    h2 = _layer_norm(base_ref[...] + routed_ref[...], g_ref[...], b_ref[...])
    hf_ref[...] = h2
    hb_ref[...] = h2.astype(BF16)


def _combine(pos, y3, wts, base, g, b):
    tm = COMB_T
    n_tiles = N_TOK // tm
    full = lambda r, c: pl.BlockSpec((r, c), lambda i: (0, 0))
    tile = lambda c: pl.BlockSpec((tm, c), lambda i: (i, 0))
    return pl.pallas_call(
        _combine_kernel,
        grid=(n_tiles,),
        in_specs=[pl.BlockSpec(memory_space=pl.ANY), pl.BlockSpec(memory_space=pl.ANY),
                  tile(TOP_K), tile(D_MODEL), full(1, D_MODEL), full(1, D_MODEL)],
        out_specs=[tile(D_MODEL), tile(D_MODEL)],
        out_shape=[jax.ShapeDtypeStruct((N_TOK, D_MODEL), F32),
                   jax.ShapeDtypeStruct((N_TOK, D_MODEL), BF16)],
        scratch_shapes=[pltpu.SMEM((2, tm * TOP_K), I32),
                        pltpu.VMEM((2, tm, TOP_K, D_CHUNKS, LANES), F32),
                        pltpu.VMEM((tm, D_MODEL), F32),
                        pltpu.SemaphoreType.DMA((2,)),
                        pltpu.SemaphoreType.DMA((2,))],
        compiler_params=_cparams(("arbitrary",)),
        name="combine_ln2",
    )(pos.reshape(n_tiles, tm * TOP_K), y3, wts, base, g, b)


def _pack_w_in(w):
    q_lat = w[:, 0:384]
    c_kv = w[:, 384:640]
    k_pe = w[:, 640:672]
    z = w[:, 672:1696]
    xbc = w[:, 1696:3232]
    dt = w[:, 3232:3248]
    gate_a = w[:, 3248:4272]
    gate_b = w[:, 4272:5296]
    zeros = lambda n: jnp.zeros((D_MODEL, n), w.dtype)
    small = jnp.concatenate([zeros(KPE_LANE), k_pe, dt, zeros(LANES - DT_LANE - SSM_HEADS)], axis=1)
    return jnp.concatenate([gate_a, gate_b, z, xbc, q_lat, small, c_kv], axis=1).astype(BF16)


def _pack_w_q(w):
    w = w.reshape(Q_RANK, MLA_HEADS, QK_NOPE + QK_ROPE)
    w = jnp.pad(w, ((0, 0), (0, 0), (0, HEAD_PAD - QK_NOPE - QK_ROPE)))
    return w.reshape(Q_RANK, MLA_HEADS * HEAD_PAD).astype(BF16)


def _pack_w_kv(w):
    w = w.reshape(KV_RANK, MLA_HEADS, QK_NOPE + V_DIM)
    wk = jnp.pad(w[:, :, :QK_NOPE], ((0, 0), (0, 0), (0, HEAD_PAD - QK_NOPE)))
    wv = w[:, :, QK_NOPE:]
    return (wk.reshape(KV_RANK, MLA_HEADS * HEAD_PAD).astype(BF16),
            wv.reshape(KV_RANK, MLA_HEADS * V_DIM).astype(BF16))


def _rope_tables(positions):
    half = QK_ROPE // 2
    inv_freq = jnp.power(ROPE_THETA, -jnp.arange(half, dtype=F32) * (2.0 / QK_ROPE))
    ang = positions.astype(F32).reshape(N_TOK, 1) * inv_freq
    cos, sin = jnp.cos(ang), jnp.sin(ang)
    z = lambda n: jnp.zeros((N_TOK, n), F32)
    tail = HEAD_PAD - QK_NOPE - QK_ROPE
    tc = jnp.concatenate([jnp.ones((N_TOK, QK_NOPE), F32), cos, cos, z(tail)], axis=1)
    ts1 = jnp.concatenate([z(QK_NOPE), -sin, z(half), z(tail)], axis=1)
    ts2 = jnp.concatenate([z(QK_NOPE), z(half), sin, z(tail)], axis=1)
    return tc, ts1, ts2


def _router_pieces(w):
    w_hi = lax.reduce_precision(w, exponent_bits=8, mantissa_bits=7)
    w_mid = (w - w_hi).astype(BF16)
    w_hi = w_hi.astype(BF16)
    return jnp.concatenate([w_hi, w_hi, w_mid], axis=0)


def _head_lane_row(v):
    return jnp.zeros((1, LANES), F32).at[0, DT_LANE:DT_LANE + SSM_HEADS].set(v.astype(F32))


def _expand_matrix():
    r = jnp.arange(LANES)[:, None]
    c = jnp.arange(SSM_INNER)[None, :]
    e = ((r - DT_LANE) == (c // SSM_HEADDIM)).astype(BF16)
    return jnp.concatenate([e, e, e], axis=0)


def kernel(x, p, positions, w_in, q_norm, w_q_up, kv_norm, w_kv_up, conv_w, conv_b, dt_bias, a_log, d_skip,
           ssm_norm, w_attn_br, w_ssm_br, w_o, ln1_g, ln1_b, w_router, router_bias, w_exp_gate, w_exp_up,
           w_exp_down, w_sh_gate, w_sh_up, w_sh_down, w_ple, w_ple_gate, ln2_g, ln2_b):
    tc, ts1, ts2 = _rope_tables(positions)
    e_mat = _expand_matrix()
    hf = x.reshape(N_TOK, D_MODEL)
    hb = hf.astype(BF16)
    row = lambda v: v.reshape(1, -1).astype(F32)
    for i in range(DEPTH):
        proj = _in_proj(hb, _pack_w_in(w_in[i]))
        q = _q_up(proj, row(q_norm[i]), _pack_w_q(w_q_up[i]), tc, ts1, ts2)
        wk, wv = _pack_w_kv(w_kv_up[i])
        k, v = _kv_up(proj, row(kv_norm[i]), wk, wv, tc, ts1, ts2)
        attn = _attention(q, k, v).reshape(N_TOK, MLA_HEADS * V_DIM)
        ssm_y = _ssd(proj, conv_w[i], row(conv_b[i]), _head_lane_row(dt_bias[i]), _head_lane_row(a_log[i]),
                     row(jnp.repeat(d_skip[i], SSM_HEADDIM)), row(ssm_norm[i]), e_mat)
        hf, hb = _merge(attn, ssm_y, proj, hf, w_attn_br[i].astype(BF16), w_ssm_br[i].astype(BF16),
                        w_o[i].astype(BF16), row(ln1_g[i]), row(ln1_b[i]))
        base, idx, wts, cnt = _post(hf, hb, p[i].reshape(N_TOK, PLE_DIM), _router_pieces(w_router[i]),
                                    row(router_bias[i]),
                                    w_sh_gate[i].astype(BF16), w_sh_up[i].astype(BF16),
                                    w_sh_down[i].astype(BF16), w_ple[i].astype(BF16),
                                    w_ple_gate[i].astype(BF16))
        base_slots, counts, pad_start, padded, n_used = _slot_layout(cnt)
        pos = _slot_positions(idx, base_slots)
        xs = _dispatch(counts, pad_start, padded, n_used, pos, hf.reshape(N_TOK, D_CHUNKS, LANES))
        y3 = _moe(pad_start // SLOT_BLOCK, padded // SLOT_BLOCK, n_used, xs,
                  w_exp_gate, w_exp_up, w_exp_down, i)
        hf, hb = _combine(pos, y3, wts, base, row(ln2_g[i]), row(ln2_b[i]))
    return hf.reshape(BATCH, SEQ, D_MODEL)
```
